```python
import math
import jax, jax.numpy as jnp
from jax import lax
import numpy as np

D_MODEL = 1024
BATCH = 4
SEQ = 4096
DEPTH = 4

CHUNK = 64
Q_BLOCK = 128
MIX_WIDTH = D_MODEL
DIFF_WIDTH = MIX_WIDTH // 2
HGRN_WIDTH = MIX_WIDTH - DIFF_WIDTH
DIFF_HEAD_DIM = 64
DIFF_HEADS = DIFF_WIDTH // (2 * DIFF_HEAD_DIM)
HGRN_EXPAND = 128
HGRN_HEADS = HGRN_WIDTH // HGRN_EXPAND
HGRN_DK = HGRN_EXPAND
HGRN_DV = HGRN_WIDTH // HGRN_HEADS
D_FF = ((8 * D_MODEL + 3 * 256 - 1) // (3 * 256)) * 256
NORM_EPS = 1e-6

IN_WIDTHS = (
    2 * DIFF_HEADS * DIFF_HEAD_DIM,
    2 * DIFF_HEADS * DIFF_HEAD_DIM,
    DIFF_HEADS * 2 * DIFF_HEAD_DIM,
    HGRN_HEADS * HGRN_DK,
    HGRN_HEADS * HGRN_DK,
    HGRN_HEADS * HGRN_DV,
    HGRN_HEADS * HGRN_DV,
)
D_IN = sum(IN_WIDTHS)
SPLIT_POINTS = tuple(int(v) for v in np.cumsum(IN_WIDTHS)[:-1])

kernel_name = "hymba_diffattn_hgrn2_trunk"


def rms_norm(x, g):
    xf = x.astype(jnp.float32)
    y = xf * lax.rsqrt(jnp.mean(xf * xf, axis=-1, keepdims=True) + NORM_EPS)
    return (y * g.astype(jnp.float32)).astype(x.dtype)


def diff_attention(q, k, v, lam):
    B, H2, S, Dh = q.shape
    H = H2 // 2
    nb = S // Q_BLOCK
    qb = q.reshape(B, H2, nb, Q_BLOCK, Dh).transpose(2, 0, 1, 3, 4)
    key_chunk = jnp.arange(S) // CHUNK

    def one_block(args):
        q_blk, blk = args
        q_chunk = (blk * Q_BLOCK + jnp.arange(Q_BLOCK)) // CHUNK
        mask = key_chunk[None, :] <= q_chunk[:, None]
        s = jnp.einsum('bhqd,bhkd->bhqk', q_blk, k).astype(jnp.float32)
        p = jax.nn.softmax(jnp.where(mask, s, -jnp.inf), axis=-1)
        p = p.reshape(B, H, 2, Q_BLOCK, S)
        w = p[:, :, 0] - lam * p[:, :, 1]
        return jnp.einsum('bhqk,bhkv->bhqv', w.astype(v.dtype), v)

    out = lax.map(one_block, (qb, jnp.arange(nb)))
    return out.transpose(1, 2, 0, 3, 4).reshape(B, H, S, v.shape[-1])


def hgrn2_chunked(q, k, v, lf):
    B, S, H, dk = q.shape
    dv = v.shape[-1]
    nc = S // CHUNK

    def to_chunks(t):
        return t.reshape(B, nc, CHUNK, H, t.shape[-1]).transpose(1, 0, 3, 2, 4)

    qc, kc, vc, lfc = to_chunks(q), to_chunks(k), to_chunks(v), to_chunks(lf)
    bc = jnp.cumsum(lfc, axis=3)
    causal = jnp.tril(jnp.ones((CHUNK, CHUNK), dtype=bool))

    def step(state, inp):
        q_c, k_c, v_c, b_c = inp
        diff = b_c[:, :, :, None, :] - b_c[:, :, None, :, :]
        decay = jnp.exp(jnp.where(causal[:, :, None], diff, -jnp.inf))
        scores = jnp.einsum('bhtk,bhsk,bhtsk->bhts', q_c, k_c, decay)
        o = (jnp.einsum('bhts,bhsv->bhtv', scores, v_c)
             + jnp.einsum('bhtk,bhkv->bhtv', q_c * jnp.exp(b_c), state))
        b_last = b_c[:, :, -1:, :]
        state = (jnp.exp(b_last[:, :, 0, :, None]) * state
                 + jnp.einsum('bhsk,bhsv->bhkv', k_c * jnp.exp(b_last - b_c), v_c))
        return state, o

    state0 = jnp.zeros((B, H, dk, dv), jnp.float32)
    _, o = lax.scan(step, state0, (qc, kc, vc, bc))
    return o.transpose(1, 0, 3, 2, 4).reshape(B, S, H, dv)


def setup_inputs(seed: int = 0) -> dict:
    key = jax.random.key(seed)
    ks = jax.random.split(key, 17)

    def normal(k, shape, scale):
        return jax.random.normal(k, shape, jnp.float32) * scale

    return {
        "x": normal(ks[0], (BATCH, SEQ, D_MODEL), 1.0),
        "attn_norm_g": 1.0 + normal(ks[1], (DEPTH, D_MODEL), 0.02),
        "w_in": normal(ks[2], (DEPTH, D_MODEL, D_IN), D_MODEL ** -0.5),
        "q_norm_g": 1.0 + normal(ks[3], (DEPTH, DIFF_HEAD_DIM), 0.02),
        "k_norm_g": 1.0 + normal(ks[4], (DEPTH, DIFF_HEAD_DIM), 0.02),
        "lambda_q1": normal(ks[5], (DEPTH, DIFF_HEAD_DIM), 0.1),
        "lambda_k1": normal(ks[6], (DEPTH, DIFF_HEAD_DIM), 0.1),
        "lambda_q2": normal(ks[7], (DEPTH, DIFF_HEAD_DIM), 0.1),
        "lambda_k2": normal(ks[8], (DEPTH, DIFF_HEAD_DIM), 0.1),
        "subln_g": 1.0 + normal(ks[9], (DEPTH, 2 * DIFF_HEAD_DIM), 0.02),
        "lower_bounds": normal(ks[10], (DEPTH, HGRN_HEADS * HGRN_DK), 0.1),
        "hgrn_norm_g": 1.0 + normal(ks[11], (DEPTH, HGRN_DV), 0.02),
        "w_out": normal(ks[12], (DEPTH, MIX_WIDTH, D_MODEL), MIX_WIDTH ** -0.5),
        "ffn_norm_g": 1.0 + normal(ks[13], (DEPTH, D_MODEL), 0.02),
        "w_gate": normal(ks[14], (DEPTH, D_MODEL, D_FF), D_MODEL ** -0.5),
        "w_up": normal(ks[15], (DEPTH, D_MODEL, D_FF), D_MODEL ** -0.5),
        "w_down": normal(ks[16], (DEPTH, D_FF, D_MODEL), D_FF ** -0.5),
    }


def reference(x, attn_norm_g, w_in, q_norm_g, k_norm_g, lambda_q1, lambda_k1,
              lambda_q2, lambda_k2, subln_g, lower_bounds, hgrn_norm_g, w_out,
              ffn_norm_g, w_gate, w_up, w_down):
    B, S, _ = x.shape
    gamma = jax.nn.softmax(lower_bounds.astype(jnp.float32), axis=0)
    lbs = jnp.cumsum(gamma, axis=0) - gamma[0]

    for l in range(DEPTH):
        h = rms_norm(x, attn_norm_g[l])
        proj = jnp.einsum('bsd,de->bse', h, w_in[l])
        qd, kd, vd, qr, fr, ir, gr = jnp.split(proj, SPLIT_POINTS, axis=-1)

        qd = rms_norm(qd.reshape(B, S, 2 * DIFF_HEADS, DIFF_HEAD_DIM), q_norm_g[l]) * (DIFF_HEAD_DIM ** -0.5)
        kd = rms_norm(kd.reshape(B, S, 2 * DIFF_HEADS, DIFF_HEAD_DIM), k_norm_g[l])
        vd = vd.reshape(B, S, DIFF_HEADS, 2 * DIFF_HEAD_DIM)
        lam_init = 0.8 - 0.6 * math.exp(-0.3 * l)
        lam = (jnp.exp(jnp.sum(lambda_q1[l].astype(jnp.float32) * lambda_k1[l].astype(jnp.float32)))
               - jnp.exp(jnp.sum(lambda_q2[l].astype(jnp.float32) * lambda_k2[l].astype(jnp.float32)))
               + lam_init)
        od = diff_attention(qd.transpose(0, 2, 1, 3), kd.transpose(0, 2, 1, 3),
                            vd.transpose(0, 2, 1, 3), lam)
        od = rms_norm(od, subln_g[l]) * (1.0 - lam_init)
        od = od.transpose(0, 2, 1, 3).reshape(B, S, DIFF_WIDTH)

        fz = fr.astype(jnp.float32).reshape(B, S, HGRN_HEADS, HGRN_DK)
        if l == 0:
            lf = jax.nn.log_sigmoid(fz)
        else:
            lb = lbs[l].reshape(HGRN_HEADS, HGRN_DK)
            lf = jnp.log(lb + (1.0 - lb) * jax.nn.sigmoid(fz))
        kr = -jnp.expm1(lf)
        qr = jax.nn.silu(qr.astype(jnp.float32)).reshape(B, S, HGRN_HEADS, HGRN_DK)
        vr = ir.astype(jnp.float32).reshape(B, S, HGRN_HEADS, HGRN_DV)
        orr = hgrn2_chunked(qr, kr, vr, lf).astype(x.dtype)
        orr = rms_norm(orr, hgrn_norm_g[l]) * jax.nn.silu(gr.reshape(B, S, HGRN_HEADS, HGRN_DV))
        orr = orr.reshape(B, S, HGRN_WIDTH)

        mix = jnp.concatenate([od.astype(x.dtype), orr.astype(x.dtype)], axis=-1)
        x = x + jnp.einsum('bse,ed->bsd', mix, w_out[l])

        h = rms_norm(x, ffn_norm_g[l])
        ff = jax.nn.silu(jnp.einsum('bsd,df->bsf', h, w_gate[l])) * jnp.einsum('bsd,df->bsf', h, w_up[l])
        x = x + jnp.einsum('bsf,fd->bsd', ff, w_down[l])
    return x
```

```python
import functools
import math

import numpy as np
import jax
import jax.numpy as jnp
from jax import lax
from jax.experimental import pallas as pl
from jax.experimental.pallas import tpu as pltpu

F32 = jnp.float32
BF16 = jnp.bfloat16

NORM_EPS = 1e-6
CHUNK = 64
CHUNK_SHIFT = 6
HEAD_LANES = 128
N_HEADS = 4
SEG = N_HEADS * HEAD_LANES
DIFF_HEAD_DIM = 64
N_SEG = 7
N_LEVELS = CHUNK_SHIFT

VMEM_LIMIT_BYTES = 52 * 1024 * 1024
FFN_TILE = 256


def _rsqrt_mean_sq(y):
    return lax.rsqrt(jnp.mean(y * y, axis=-1, keepdims=True) + NORM_EPS)


def _sigmoid(z):
    return 1.0 / (1.0 + jnp.exp(-z))


def _resident(block_shape, index_map):
    return pl.BlockSpec(block_shape, index_map, pipeline_mode=pl.Buffered(1))


def _in_proj_kernel(x_ref, g_ref, w_ref, qg_ref, kg_ref, gsum_ref,
                    q_ref, k_ref, v_ref, hq_ref, hf_ref, hi_ref, hg_ref):
    x = x_ref[...]
    h = (x * _rsqrt_mean_sq(x) * g_ref[...]).astype(BF16)

    def seg(s):
        return jnp.dot(h, w_ref[:, s * SEG:(s + 1) * SEG], preferred_element_type=F32)

    def head_norm(y, gain_ref, out_ref):
        for c in range(N_HEADS):
            yc = y[:, c * HEAD_LANES:(c + 1) * HEAD_LANES]
            ss = jnp.dot((yc * yc).astype(BF16), gsum_ref[...], preferred_element_type=F32)
            yn = yc * lax.rsqrt(ss * (1.0 / DIFF_HEAD_DIM) + NORM_EPS) * gain_ref[...]
            out_ref[:, c * HEAD_LANES:(c + 1) * HEAD_LANES] = yn.astype(BF16)

    head_norm(seg(0), qg_ref, q_ref)
    head_norm(seg(1), kg_ref, k_ref)
    v_ref[...] = seg(2).astype(BF16)
    qr = seg(3)
    hq_ref[...] = (qr * _sigmoid(qr)).astype(BF16)
    hf_ref[...] = seg(4)
    hi_ref[...] = seg(5).astype(BF16)
    gr = seg(6)
    hg_ref[...] = (gr * _sigmoid(gr)).astype(BF16)


def _in_proj(x, norm_g, w_in, layer, qg, kg, gsum, tm):
    n, d = x.shape
    row = lambda i: (i, 0)
    const2 = lambda i: (0, 0)
    out_bf = jax.ShapeDtypeStruct((n, SEG), BF16)
    out_f32 = jax.ShapeDtypeStruct((n, SEG), F32)
    seg_spec = pl.BlockSpec((tm, SEG), row)
    return pl.pallas_call(
        _in_proj_kernel,
        grid=(n // tm,),
        in_specs=[
            pl.BlockSpec((tm, d), row),
            _resident((1, d), const2),
            _resident((None, d, N_SEG * SEG), lambda i: (layer, 0, 0)),
            _resident((1, HEAD_LANES), const2),
            _resident((1, HEAD_LANES), const2),
            _resident((HEAD_LANES, HEAD_LANES), const2),
        ],
        out_specs=[seg_spec] * N_SEG,
        out_shape=[out_bf, out_bf, out_bf, out_bf, out_f32, out_bf, out_bf],
        compiler_params=pltpu.CompilerParams(
            dimension_semantics=("arbitrary",), vmem_limit_bytes=VMEM_LIMIT_BYTES),
        name="in_proj",
    )(x, norm_g, w_in, qg, kg, gsum)


def _attn_kernel(q_ref, k_ref, v_ref, lq1_ref, lk1_ref, lq2_ref, lk2_ref, sg_ref,
                 o_ref, *, tq, lam_init):
    i = pl.program_id(2)
    q = q_ref[...]
    lane = lax.broadcasted_iota(jnp.int32, q.shape, 1)
    zero = jnp.zeros_like(q)
    q2 = jnp.concatenate([jnp.where(lane < DIFF_HEAD_DIM, q, zero),
                          jnp.where(lane >= DIFF_HEAD_DIM, q, zero)], axis=0)

    def scores(kj):
        return lax.dot_general(q2, kj, (((1,), (1,)), ((), ())), preferred_element_type=F32)

    start = pl.multiple_of(i * tq, tq)
    s = scores(k_ref[pl.ds(start, tq), :])
    row = lax.broadcasted_iota(jnp.int32, s.shape, 0)
    col = lax.broadcasted_iota(jnp.int32, s.shape, 1)
    visible = (col >> CHUNK_SHIFT) <= ((row & (tq - 1)) >> CHUNK_SHIFT)
    s = jnp.where(visible, s, -jnp.inf)
    m = jnp.max(s, axis=-1, keepdims=True)
    p = jnp.exp(s - m)
    l = jnp.sum(p, axis=-1, keepdims=True)
    acc = jnp.dot(p.astype(BF16), v_ref[pl.ds(start, tq), :], preferred_element_type=F32)

    def body(j, carry):
        m, l, acc = carry
        off = pl.multiple_of(j * tq, tq)
        s = scores(k_ref[pl.ds(off, tq), :])
        m_new = jnp.maximum(m, jnp.max(s, axis=-1, keepdims=True))
        alpha = jnp.exp(m - m_new)
        p = jnp.exp(s - m_new)
        l = alpha * l + jnp.sum(p, axis=-1, keepdims=True)
        acc = alpha * acc + jnp.dot(p.astype(BF16), v_ref[pl.ds(off, tq), :],
                                    preferred_element_type=F32)
        return m_new, l, acc

    m, l, acc = lax.fori_loop(0, i, body, (m, l, acc))
    o = acc / l
    lam = (jnp.exp(jnp.sum(lq1_ref[...] * lk1_ref[...], axis=-1, keepdims=True))
           - jnp.exp(jnp.sum(lq2_ref[...] * lk2_ref[...], axis=-1, keepdims=True))
           + lam_init)
    od = o[:tq] - lam * o[tq:]
    od = od * _rsqrt_mean_sq(od) * sg_ref[...] * (1.0 - lam_init)
    o_ref[...] = od.astype(BF16)


def _diff_attn(q, k, v, lq1, lk1, lq2, lk2, subln_g, lam_init, tq):
    b, s, _ = q.shape
    const2 = lambda bi, hi, qi: (0, 0)
    kv_spec = pl.BlockSpec((None, s, HEAD_LANES), lambda bi, hi, qi: (bi, 0, hi))
    tile_spec = pl.BlockSpec((None, tq, HEAD_LANES), lambda bi, hi, qi: (bi, qi, hi))
    lam_spec = _resident((1, DIFF_HEAD_DIM), const2)
    return pl.pallas_call(
        functools.partial(_attn_kernel, tq=tq, lam_init=lam_init),
        grid=(b, N_HEADS, s // tq),
        in_specs=[tile_spec, kv_spec, kv_spec, lam_spec, lam_spec, lam_spec, lam_spec,
                  _resident((1, HEAD_LANES), const2)],
        out_specs=tile_spec,
        out_shape=jax.ShapeDtypeStruct((b, s, SEG), BF16),
        compiler_params=pltpu.CompilerParams(
            dimension_semantics=("arbitrary", "arbitrary", "arbitrary"),
            vmem_limit_bytes=VMEM_LIMIT_BYTES),
        name="diff_attn",
    )(q, k, v, lq1, lk1, lq2, lk2, subln_g)


def _hgrn_tables():
    c = CHUNK
    sums = np.zeros((N_LEVELS + 2, c, c), np.float32)
    masks = np.zeros((N_LEVELS + 1, c, c), np.float32)
    upper = np.zeros((N_LEVELS, c, 1), np.float32)
    for lev in range(N_LEVELS):
        size = 2 << lev
        half = size // 2
        for r in range(c):
            mid = r - r % size + half
            if r % size >= half:
                upper[lev, r, 0] = 1.0
                sums[lev, r, mid:r + 1] = 1.0
                masks[lev, r, mid - half:mid] = 1.0
            else:
                sums[lev, r, r + 1:mid] = 1.0
    for r in range(c):
        sums[N_LEVELS, r, :r + 1] = 1.0
        sums[N_LEVELS + 1, r, r + 1:] = 1.0
        masks[N_LEVELS, r, r] = 1.0
    return sums.reshape((N_LEVELS + 2) * c, c), masks, upper


def _hgrn_kernel(hq_ref, hf_ref, hi_ref, hg_ref, lb_ref, g_ref, sums_ref, masks_ref,
                 upper_ref, o_ref, state_ref, *, layer, n_chunks):
    @pl.when(pl.program_id(1) == 0)
    def _():
        state_ref[...] = jnp.zeros_like(state_ref)

    if layer > 0:
        lbp = lb_ref[...]
        ex = jnp.exp(lbp - jnp.max(lbp, axis=0, keepdims=True))
        gamma = ex / jnp.sum(ex, axis=0, keepdims=True)
        lb = jnp.sum(gamma[1:layer + 1], axis=0, keepdims=True)

    nt = (((1,), (1,)), ((), ()))
    tn = (((0,), (0,)), ((), ()))

    def chunk_body(c, carry):
        rows = pl.ds(pl.multiple_of(c * CHUNK, CHUNK), CHUNK)
        z = hf_ref[rows, :]
        e = jnp.exp(-jnp.abs(z))
        r = 1.0 / (1.0 + e)
        sig_pos = jnp.where(z >= 0, r, e * r)
        sig_neg = jnp.where(z >= 0, e * r, r)
        if layer == 0:
            lf = jnp.minimum(z, 0.0) - jnp.log1p(e)
            kk = sig_neg
        else:
            lf = jnp.log(lb + (1.0 - lb) * sig_pos)
            kk = (1.0 - lb) * sig_neg
        lf_hi = lf.astype(BF16)
        lf_lo = (lf - lf_hi.astype(F32)).astype(BF16)
        args = (jnp.dot(sums_ref[...], lf_hi, preferred_element_type=F32)
                + jnp.dot(sums_ref[...], lf_lo, preferred_element_type=F32))
        decay = jnp.exp(args)
        qq = hq_ref[rows, :].astype(F32)
        for h in range(N_HEADS):
            hs = slice(h * HEAD_LANES, (h + 1) * HEAD_LANES)
            q_h = qq[:, hs]
            k_h = kk[:, hs]
            v_h = hi_ref[rows, hs]
            a = jnp.where(masks_ref[N_LEVELS] > 0.5,
                          lax.dot_general(q_h.astype(BF16), k_h.astype(BF16), nt,
                                          preferred_element_type=F32), 0.0)
            for lev in range(N_LEVELS):
                x = (jnp.where(upper_ref[lev] > 0.5, q_h, k_h)
                     * decay[lev * CHUNK:(lev + 1) * CHUNK, hs]).astype(BF16)
                a = a + jnp.where(masks_ref[lev] > 0.5,
                                  lax.dot_general(x, x, nt, preferred_element_type=F32), 0.0)
            pre = decay[N_LEVELS * CHUNK:(N_LEVELS + 1) * CHUNK, hs]
            suf = decay[(N_LEVELS + 1) * CHUNK:(N_LEVELS + 2) * CHUNK, hs]
            st = state_ref[h]
            o = (jnp.dot(a.astype(BF16), v_h, preferred_element_type=F32)
                 + lax.dot_general((q_h * pre).astype(BF16), st.astype(BF16), nt,
                                   preferred_element_type=F32))
            state_ref[h] = (st * pre[CHUNK - 1:CHUNK, :]
                            + lax.dot_general(v_h, (k_h * suf).astype(BF16), tn,
                                              preferred_element_type=F32))
            o = o * _rsqrt_mean_sq(o) * g_ref[...] * hg_ref[rows, hs].astype(F32)
            o_ref[rows, hs] = o.astype(BF16)
        return carry

    lax.fori_loop(0, n_chunks, chunk_body, 0)


def _hgrn(hq, hf, hi, hg, lower_bounds, norm_g, tables, layer, tc):
    b, s, _ = hq.shape
    sums, masks, upper = tables
    tile = pl.BlockSpec((None, tc, SEG), lambda bi, ci: (bi, ci, 0))
    const2 = lambda bi, ci: (0, 0)
    const3 = lambda bi, ci: (0, 0, 0)
    return pl.pallas_call(
        functools.partial(_hgrn_kernel, layer=layer, n_chunks=tc // CHUNK),
        grid=(b, s // tc),
        in_specs=[tile, tile, tile, tile,
                  _resident(lower_bounds.shape, const2),
                  _resident((1, HEAD_LANES), const2),
                  _resident(sums.shape, const2),
                  _resident(masks.shape, const3),
                  _resident(upper.shape, const3)],
        out_specs=tile,
        out_shape=jax.ShapeDtypeStruct((b, s, SEG), BF16),
        scratch_shapes=[pltpu.VMEM((N_HEADS, HEAD_LANES, HEAD_LANES), F32)],
        compiler_params=pltpu.CompilerParams(
            dimension_semantics=("arbitrary", "arbitrary"),
            vmem_limit_bytes=VMEM_LIMIT_BYTES),
        name="hgrn2",
    )(hq, hf, hi, hg, lower_bounds, norm_g, sums, masks, upper)


def _out_ffn_kernel(x_ref, od_ref, or_ref, wo_ref, g_ref, wg_ref, wu_ref, wd_ref,
                    o_ref, ff_ref, *, d_ff):
    xn = (x_ref[...]
          + jnp.dot(od_ref[...], wo_ref[:SEG, :], preferred_element_type=F32)
          + jnp.dot(or_ref[...], wo_ref[SEG:, :], preferred_element_type=F32))
    h = (xn * _rsqrt_mean_sq(xn) * g_ref[...]).astype(BF16)
    for j in range(d_ff // FFN_TILE):
        cols = slice(j * FFN_TILE, (j + 1) * FFN_TILE)
        gate = jnp.dot(h, wg_ref[:, cols], preferred_element_type=F32)
        up = jnp.dot(h, wu_ref[:, cols], preferred_element_type=F32)
        ff_ref[:, cols] = (gate * _sigmoid(gate) * up).astype(BF16)
    o_ref[...] = xn + jnp.dot(ff_ref[...], wd_ref[...], preferred_element_type=F32)


def _out_ffn(x, od, orr, w_out, norm_g, w_gate, w_up, w_down, layer, tm):
    n, d = x.shape
    d_ff = w_gate.shape[-1]
    row = lambda i: (i, 0)
    lay = lambda i: (layer, 0, 0)
    return pl.pallas_call(
        functools.partial(_out_ffn_kernel, d_ff=d_ff),
        grid=(n // tm,),
        in_specs=[
            pl.BlockSpec((tm, d), row),
            pl.BlockSpec((tm, SEG), row),
            pl.BlockSpec((tm, SEG), row),
            _resident((None, d, d), lay),
            _resident((1, d), lambda i: (0, 0)),
            _resident((None, d, d_ff), lay),
            _resident((None, d, d_ff), lay),
            _resident((None, d_ff, d), lay),
        ],
        out_specs=pl.BlockSpec((tm, d), row),
        out_shape=jax.ShapeDtypeStruct((n, d), F32),
        scratch_shapes=[pltpu.VMEM((tm, d_ff), BF16)],
        compiler_params=pltpu.CompilerParams(
            dimension_semantics=("arbitrary",), vmem_limit_bytes=VMEM_LIMIT_BYTES),
        name="out_ffn",
    )(x, od, orr, w_out, norm_g, w_gate, w_up, w_down)


def _tile(n, want):
    return min(n, want)


def kernel(x, attn_norm_g, w_in, q_norm_g, k_norm_g, lambda_q1, lambda_k1, lambda_q2, lambda_k2, subln_g, lower_bounds, hgrn_norm_g, w_out, ffn_norm_g, w_gate, w_up, w_down):
    b, s, d = x.shape
    depth = w_in.shape[0]
    n = b * s
    assert d == 2 * SEG and w_in.shape[-1] == N_SEG * SEG and s % CHUNK == 0

    w_in_b = w_in.astype(BF16)
    w_out_b = w_out.astype(BF16)
    w_gate_b = w_gate.astype(BF16)
    w_up_b = w_up.astype(BF16)
    w_down_b = w_down.astype(BF16)

    group = np.arange(HEAD_LANES) // DIFF_HEAD_DIM
    gsum = jnp.asarray(group[:, None] == group[None, :], BF16)
    sums, masks, upper = _hgrn_tables()
    tables = (jnp.asarray(sums, BF16), jnp.asarray(masks), jnp.asarray(upper))
    lower_bounds = lower_bounds.astype(F32)

    tm = _tile(n, 512)
    tq = _tile(s, 256)
    tc = _tile(s, 256)

    xf = x.reshape(n, d)
    for l in range(depth):
        lam_init = 0.8 - 0.6 * math.exp(-0.3 * l)
        qg = jnp.tile(q_norm_g[l], 2)[None] * (DIFF_HEAD_DIM ** -0.5)
        kg = jnp.tile(k_norm_g[l], 2)[None]
        q, k, v, hq, hf, hi, hg = _in_proj(xf, attn_norm_g[l][None], w_in_b, l, qg, kg, gsum, tm)
        to3 = lambda t: t.reshape(b, s, SEG)
        od = _diff_attn(to3(q), to3(k), to3(v), lambda_q1[l][None], lambda_k1[l][None],
                        lambda_q2[l][None], lambda_k2[l][None], subln_g[l][None], lam_init, tq)
        orr = _hgrn(to3(hq), to3(hf), to3(hi), to3(hg), lower_bounds, hgrn_norm_g[l][None],
                    tables, l, tc)
        xf = _out_ffn(xf, od.reshape(n, SEG), orr.reshape(n, SEG), w_out_b, ffn_norm_g[l][None],
                      w_gate_b, w_up_b, w_down_b, l, tm)
    return xf.reshape(b, s, d)
```

```python
import functools
import math

import numpy as np
import jax
import jax.numpy as jnp
from jax import lax
from jax.experimental import pallas as pl
from jax.experimental.pallas import tpu as pltpu

F32 = jnp.float32
BF16 = jnp.bfloat16

NORM_EPS = 1e-6
CHUNK = 64
CHUNK_SHIFT = 6
HEAD_LANES = 128
N_HEADS = 4
SEG = N_HEADS * HEAD_LANES
DIFF_HEAD_DIM = 64
N_SEG = 7
N_LEVELS = CHUNK_SHIFT

VMEM_LIMIT_BYTES = 52 * 1024 * 1024
FFN_TILE = 256


def _rsqrt_mean_sq(y):
    return lax.rsqrt(jnp.mean(y * y, axis=-1, keepdims=True) + NORM_EPS)


def _sigmoid(z):
    return 1.0 / (1.0 + jnp.exp(-z))


def _resident(block_shape, index_map):
    return pl.BlockSpec(block_shape, index_map, pipeline_mode=pl.Buffered(1))


def _in_proj_kernel(x_ref, g_ref, w_ref, qg_ref, kg_ref, gsum_ref,
                    q_ref, k_ref, v_ref, hq_ref, hf_ref, hi_ref, hg_ref):
    x = x_ref[...]
    h = (x * _rsqrt_mean_sq(x) * g_ref[...]).astype(BF16)

    def seg(s):
        return jnp.dot(h, w_ref[:, s * SEG:(s + 1) * SEG], preferred_element_type=F32)

    def head_norm(y, gain_ref, out_ref):
        for c in range(N_HEADS):
            yc = y[:, c * HEAD_LANES:(c + 1) * HEAD_LANES]
            ss = jnp.dot((yc * yc).astype(BF16), gsum_ref[...], preferred_element_type=F32)
            yn = yc * lax.rsqrt(ss * (1.0 / DIFF_HEAD_DIM) + NORM_EPS) * gain_ref[...]
            out_ref[:, c * HEAD_LANES:(c + 1) * HEAD_LANES] = yn.astype(BF16)

    head_norm(seg(0), qg_ref, q_ref)
    head_norm(seg(1), kg_ref, k_ref)
    v_ref[...] = seg(2).astype(BF16)
    qr = seg(3)
    hq_ref[...] = (qr * _sigmoid(qr)).astype(BF16)
    hf_ref[...] = seg(4)
    hi_ref[...] = seg(5).astype(BF16)
    gr = seg(6)
    hg_ref[...] = (gr * _sigmoid(gr)).astype(BF16)


def _in_proj(x, norm_g, w_in, layer, qg, kg, gsum, tm):
    n, d = x.shape
    row = lambda i: (i, 0)
    const2 = lambda i: (0, 0)
    out_bf = jax.ShapeDtypeStruct((n, SEG), BF16)
    out_f32 = jax.ShapeDtypeStruct((n, SEG), F32)
    seg_spec = pl.BlockSpec((tm, SEG), row)
    return pl.pallas_call(
        _in_proj_kernel,
        grid=(n // tm,),
        in_specs=[
            pl.BlockSpec((tm, d), row),
            _resident((1, d), const2),
            _resident((None, d, N_SEG * SEG), lambda i: (layer, 0, 0)),
            _resident((1, HEAD_LANES), const2),
            _resident((1, HEAD_LANES), const2),
            _resident((HEAD_LANES, HEAD_LANES), const2),
        ],
        out_specs=[seg_spec] * N_SEG,
        out_shape=[out_bf, out_bf, out_bf, out_bf, out_f32, out_bf, out_bf],
        compiler_params=pltpu.CompilerParams(
            dimension_semantics=("arbitrary",), vmem_limit_bytes=VMEM_LIMIT_BYTES),
        name="in_proj",
    )(x, norm_g, w_in, qg, kg, gsum)


def _split_maps(q):
    lane = lax.broadcasted_iota(jnp.int32, q.shape, 1)
    zero = jnp.zeros_like(q)
    return jnp.concatenate([jnp.where(lane < DIFF_HEAD_DIM, q, zero),
                            jnp.where(lane >= DIFF_HEAD_DIM, q, zero)], axis=0)


def _chunk_visible(shape, tq):
    row = lax.broadcasted_iota(jnp.int32, shape, 0)
    col = lax.broadcasted_iota(jnp.int32, shape, 1)
    return (col >> CHUNK_SHIFT) <= ((row & (tq - 1)) >> CHUNK_SHIFT)


def _attn_epilogue(o, lq1_ref, lk1_ref, lq2_ref, lk2_ref, sg_ref, o_ref, tq, lam_init):
    lam = (jnp.exp(jnp.sum(lq1_ref[...] * lk1_ref[...], axis=-1, keepdims=True))
           - jnp.exp(jnp.sum(lq2_ref[...] * lk2_ref[...], axis=-1, keepdims=True))
           + lam_init)
    od = o[:tq] - lam * o[tq:]
    od = od * _rsqrt_mean_sq(od) * sg_ref[...] * (1.0 - lam_init)
    o_ref[...] = od.astype(BF16)


_NT = (((1,), (1,)), ((), ()))


def _attn_online_kernel(q_ref, k_ref, v_ref, lq1_ref, lk1_ref, lq2_ref, lk2_ref, sg_ref,
                        o_ref, *, tq, lam_init):
    i = pl.program_id(2)
    q2 = _split_maps(q_ref[...])

    def scores(kj):
        return lax.dot_general(q2, kj, _NT, preferred_element_type=F32)

    start = pl.multiple_of(i * tq, tq)
    s = scores(k_ref[pl.ds(start, tq), :])
    s = jnp.where(_chunk_visible(s.shape, tq), s, -jnp.inf)
    m = jnp.max(s, axis=-1, keepdims=True)
    p = jnp.exp2(s - m)
    l = jnp.sum(p, axis=-1, keepdims=True)
    acc = jnp.dot(p.astype(BF16), v_ref[pl.ds(start, tq), :], preferred_element_type=F32)

    def body(j, carry):
        m, l, acc = carry
        off = pl.multiple_of(j * tq, tq)
        s = scores(k_ref[pl.ds(off, tq), :])
        m_new = jnp.maximum(m, jnp.max(s, axis=-1, keepdims=True))
        alpha = jnp.exp2(m - m_new)
        p = jnp.exp2(s - m_new)
        l = alpha * l + jnp.sum(p, axis=-1, keepdims=True)
        acc = alpha * acc + jnp.dot(p.astype(BF16), v_ref[pl.ds(off, tq), :],
                                    preferred_element_type=F32)
        return m_new, l, acc

    m, l, acc = lax.fori_loop(0, i, body, (m, l, acc))
    _attn_epilogue(acc / l, lq1_ref, lk1_ref, lq2_ref, lk2_ref, sg_ref, o_ref, tq, lam_init)


def _attn_bounded_kernel(q_ref, k_ref, v_ref, lq1_ref, lk1_ref, lq2_ref, lk2_ref, sg_ref,
                         o_ref, acc_ref, l_ref, *, tq, lam_init):
    i = pl.program_id(2)
    q2 = _split_maps(q_ref[...])

    def tile(off, diagonal):
        s = lax.dot_general(q2, k_ref[pl.ds(off, tq), :], _NT, preferred_element_type=F32)
        p = jnp.exp2(s)
        if diagonal:
            p = jnp.where(_chunk_visible(p.shape, tq), p, 0.0)
        lsum = p[:, :HEAD_LANES]
        for c in range(1, tq // HEAD_LANES):
            lsum = lsum + p[:, c * HEAD_LANES:(c + 1) * HEAD_LANES]
        pv = jnp.dot(p.astype(BF16), v_ref[pl.ds(off, tq), :], preferred_element_type=F32)
        return lsum, pv

    lsum, pv = tile(pl.multiple_of(i * tq, tq), True)
    l_ref[...] = lsum
    acc_ref[...] = pv

    def body(j, carry):
        lsum, pv = tile(pl.multiple_of(j * tq, tq), False)
        l_ref[...] += lsum
        acc_ref[...] += pv
        return carry

    lax.fori_loop(0, i, body, 0)
    l = jnp.sum(l_ref[...], axis=-1, keepdims=True)
    _attn_epilogue(acc_ref[...] / l, lq1_ref, lk1_ref, lq2_ref, lk2_ref, sg_ref, o_ref,
                   tq, lam_init)


SCORE_BOUND_MAX = 40.0


def _diff_attn(q, k, v, lq1, lk1, lq2, lk2, subln_g, lam_init, tq, score_bound):
    b, s, _ = q.shape
    const2 = lambda bi, hi, qi: (0, 0)
    kv_spec = pl.BlockSpec((None, s, HEAD_LANES), lambda bi, hi, qi: (bi, 0, hi))
    tile_spec = pl.BlockSpec((None, tq, HEAD_LANES), lambda bi, hi, qi: (bi, qi, hi))
    lam_spec = _resident((1, DIFF_HEAD_DIM), const2)

    def call(body, scratch, name):
        return pl.pallas_call(
            functools.partial(body, tq=tq, lam_init=lam_init),
            grid=(b, N_HEADS, s // tq),
            in_specs=[tile_spec, kv_spec, kv_spec, lam_spec, lam_spec, lam_spec, lam_spec,
                      _resident((1, HEAD_LANES), const2)],
            out_specs=tile_spec,
            out_shape=jax.ShapeDtypeStruct((b, s, SEG), BF16),
            scratch_shapes=scratch,
            compiler_params=pltpu.CompilerParams(
                dimension_semantics=("arbitrary", "arbitrary", "arbitrary"),
                vmem_limit_bytes=VMEM_LIMIT_BYTES),
            name=name,
        )

    stats = [pltpu.VMEM((2 * tq, HEAD_LANES), F32), pltpu.VMEM((2 * tq, HEAD_LANES), F32)]
    args = (q, k, v, lq1, lk1, lq2, lk2, subln_g)
    return lax.cond(score_bound <= SCORE_BOUND_MAX,
                    call(_attn_bounded_kernel, stats, "diff_attn_bounded"),
                    call(_attn_online_kernel, [], "diff_attn_online"),
                    *args)


def _hgrn_tables():
    c = CHUNK
    sums = np.zeros((N_LEVELS + 2, c, c), np.float32)
    masks = np.zeros((N_LEVELS + 1, c, c), np.float32)
    upper = np.zeros((N_LEVELS, c, 1), np.float32)
    for lev in range(N_LEVELS):
        size = 2 << lev
        half = size // 2
        for r in range(c):
            mid = r - r % size + half
            if r % size >= half:
                upper[lev, r, 0] = 1.0
                sums[lev, r, mid:r + 1] = 1.0
                masks[lev, r, mid - half:mid] = 1.0
            else:
                sums[lev, r, r + 1:mid] = 1.0
    for r in range(c):
        sums[N_LEVELS, r, :r + 1] = 1.0
        sums[N_LEVELS + 1, r, r + 1:] = 1.0
        masks[N_LEVELS, r, r] = 1.0
    return sums.reshape((N_LEVELS + 2) * c, c), masks, upper


def _hgrn_kernel(hq_ref, hf_ref, hi_ref, hg_ref, lb_ref, g_ref, sums_ref, masks_ref,
                 upper_ref, o_ref, state_ref, *, layer, n_chunks):
    @pl.when(pl.program_id(1) == 0)
    def _():
        state_ref[...] = jnp.zeros_like(state_ref)

    if layer > 0:
        lbp = lb_ref[...]
        ex = jnp.exp(lbp - jnp.max(lbp, axis=0, keepdims=True))
        gamma = ex / jnp.sum(ex, axis=0, keepdims=True)
        lb = jnp.sum(gamma[1:layer + 1], axis=0, keepdims=True)

    nt = (((1,), (1,)), ((), ()))
    tn = (((0,), (0,)), ((), ()))

    def chunk_body(c, carry):
        rows = pl.ds(pl.multiple_of(c * CHUNK, CHUNK), CHUNK)
        z = hf_ref[rows, :]
        e = jnp.exp(-jnp.abs(z))
        r = 1.0 / (1.0 + e)
        sig_pos = jnp.where(z >= 0, r, e * r)
        sig_neg = jnp.where(z >= 0, e * r, r)
        if layer == 0:
            lf = jnp.minimum(z, 0.0) - jnp.log1p(e)
            kk = sig_neg
        else:
            lf = jnp.log(lb + (1.0 - lb) * sig_pos)
            kk = (1.0 - lb) * sig_neg
        lf_hi = lf.astype(BF16)
        lf_lo = (lf - lf_hi.astype(F32)).astype(BF16)
        args = (jnp.dot(sums_ref[...], lf_hi, preferred_element_type=F32)
                + jnp.dot(sums_ref[...], lf_lo, preferred_element_type=F32))
        decay = jnp.exp(args)
        qq = hq_ref[rows, :].astype(F32)
        for h in range(N_HEADS):
            hs = slice(h * HEAD_LANES, (h + 1) * HEAD_LANES)
            q_h = qq[:, hs]
            k_h = kk[:, hs]
            v_h = hi_ref[rows, hs]
            a = jnp.where(masks_ref[N_LEVELS] > 0.5,
                          lax.dot_general(q_h.astype(BF16), k_h.astype(BF16), nt,
                                          preferred_element_type=F32), 0.0)
            for lev in range(N_LEVELS):
                x = (jnp.where(upper_ref[lev] > 0.5, q_h, k_h)
                     * decay[lev * CHUNK:(lev + 1) * CHUNK, hs]).astype(BF16)
                a = a + jnp.where(masks_ref[lev] > 0.5,
                                  lax.dot_general(x, x, nt, preferred_element_type=F32), 0.0)
            pre = decay[N_LEVELS * CHUNK:(N_LEVELS + 1) * CHUNK, hs]
            suf = decay[(N_LEVELS + 1) * CHUNK:(N_LEVELS + 2) * CHUNK, hs]
            st = state_ref[h]
            o = (jnp.dot(a.astype(BF16), v_h, preferred_element_type=F32)
                 + lax.dot_general((q_h * pre).astype(BF16), st.astype(BF16), nt,
                                   preferred_element_type=F32))
            state_ref[h] = (st * pre[CHUNK - 1:CHUNK, :]
                            + lax.dot_general(v_h, (k_h * suf).astype(BF16), tn,
                                              preferred_element_type=F32))
            o = o * _rsqrt_mean_sq(o) * g_ref[...] * hg_ref[rows, hs].astype(F32)
            o_ref[rows, hs] = o.astype(BF16)
        return carry

    lax.fori_loop(0, n_chunks, chunk_body, 0)


def _hgrn(hq, hf, hi, hg, lower_bounds, norm_g, tables, layer, tc):
    b, s, _ = hq.shape
    sums, masks, upper = tables
    tile = pl.BlockSpec((None, tc, SEG), lambda bi, ci: (bi, ci, 0))
    const2 = lambda bi, ci: (0, 0)
    const3 = lambda bi, ci: (0, 0, 0)
    return pl.pallas_call(
        functools.partial(_hgrn_kernel, layer=layer, n_chunks=tc // CHUNK),
        grid=(b, s // tc),
        in_specs=[tile, tile, tile, tile,
                  _resident(lower_bounds.shape, const2),
                  _resident((1, HEAD_LANES), const2),
                  _resident(sums.shape, const2),
                  _resident(masks.shape, const3),
                  _resident(upper.shape, const3)],
        out_specs=tile,
        out_shape=jax.ShapeDtypeStruct((b, s, SEG), BF16),
        scratch_shapes=[pltpu.VMEM((N_HEADS, HEAD_LANES, HEAD_LANES), F32)],
        compiler_params=pltpu.CompilerParams(
            dimension_semantics=("arbitrary", "arbitrary"),
            vmem_limit_bytes=VMEM_LIMIT_BYTES),
        name="hgrn2",
    )(hq, hf, hi, hg, lower_bounds, norm_g, sums, masks, upper)


def _out_ffn_kernel(x_ref, od_ref, or_ref, wo_ref, g_ref, wg_ref, wu_ref, wd_ref,
                    o_ref, ff_ref, *, d_ff):
    xn = (x_ref[...]
          + jnp.dot(od_ref[...], wo_ref[:SEG, :], preferred_element_type=F32)
          + jnp.dot(or_ref[...], wo_ref[SEG:, :], preferred_element_type=F32))
    h = (xn * _rsqrt_mean_sq(xn) * g_ref[...]).astype(BF16)
    for j in range(d_ff // FFN_TILE):
        cols = slice(j * FFN_TILE, (j + 1) * FFN_TILE)
        gate = jnp.dot(h, wg_ref[:, cols], preferred_element_type=F32)
        up = jnp.dot(h, wu_ref[:, cols], preferred_element_type=F32)
        ff_ref[:, cols] = (gate * _sigmoid(gate) * up).astype(BF16)
    o_ref[...] = xn + jnp.dot(ff_ref[...], wd_ref[...], preferred_element_type=F32)


def _out_ffn(x, od, orr, w_out, norm_g, w_gate, w_up, w_down, layer, tm):
    n, d = x.shape
    d_ff = w_gate.shape[-1]
    row = lambda i: (i, 0)
    lay = lambda i: (layer, 0, 0)
    return pl.pallas_call(
        functools.partial(_out_ffn_kernel, d_ff=d_ff),
        grid=(n // tm,),
        in_specs=[
            pl.BlockSpec((tm, d), row),
            pl.BlockSpec((tm, SEG), row),
            pl.BlockSpec((tm, SEG), row),
            _resident((None, d, d), lay),
            _resident((1, d), lambda i: (0, 0)),
            _resident((None, d, d_ff), lay),
            _resident((None, d, d_ff), lay),
            _resident((None, d_ff, d), lay),
        ],
        out_specs=pl.BlockSpec((tm, d), row),
        out_shape=jax.ShapeDtypeStruct((n, d), F32),
        scratch_shapes=[pltpu.VMEM((tm, d_ff), BF16)],
        compiler_params=pltpu.CompilerParams(
            dimension_semantics=("arbitrary",), vmem_limit_bytes=VMEM_LIMIT_BYTES),
        name="out_ffn",
    )(x, od, orr, w_out, norm_g, w_gate, w_up, w_down)


def _tile(n, want):
    return min(n, want)


def kernel(x, attn_norm_g, w_in, q_norm_g, k_norm_g, lambda_q1, lambda_k1, lambda_q2, lambda_k2, subln_g, lower_bounds, hgrn_norm_g, w_out, ffn_norm_g, w_gate, w_up, w_down):
    b, s, d = x.shape
    depth = w_in.shape[0]
    n = b * s
    assert d == 2 * SEG and w_in.shape[-1] == N_SEG * SEG and s % CHUNK == 0

    w_in_b = w_in.astype(BF16)
    w_out_b = w_out.astype(BF16)
    w_gate_b = w_gate.astype(BF16)
    w_up_b = w_up.astype(BF16)
    w_down_b = w_down.astype(BF16)

    group = np.arange(HEAD_LANES) // DIFF_HEAD_DIM
    gsum = jnp.asarray(group[:, None] == group[None, :], BF16)
    sums, masks, upper = _hgrn_tables()
    tables = (jnp.asarray(sums, BF16), jnp.asarray(masks), jnp.asarray(upper))
    lower_bounds = lower_bounds.astype(F32)

    tm = _tile(n, 512)
    tq = _tile(s, 256)
    tc = _tile(s, 256)

    xf = x.reshape(n, d)
    for l in range(depth):
        lam_init = 0.8 - 0.6 * math.exp(-0.3 * l)
        qg = jnp.tile(q_norm_g[l], 2)[None] * (DIFF_HEAD_DIM ** -0.5 * math.log2(math.e))
        kg = jnp.tile(k_norm_g[l], 2)[None]
        score_bound = (1.02 * math.sqrt(DIFF_HEAD_DIM)
                       * jnp.max(jnp.abs(q_norm_g[l])) * jnp.max(jnp.abs(k_norm_g[l])))
        q, k, v, hq, hf, hi, hg = _in_proj(xf, attn_norm_g[l][None], w_in_b, l, qg, kg, gsum, tm)
        to3 = lambda t: t.reshape(b, s, SEG)
        od = _diff_attn(to3(q), to3(k), to3(v), lambda_q1[l][None], lambda_k1[l][None],
                        lambda_q2[l][None], lambda_k2[l][None], subln_g[l][None], lam_init, tq,
                        score_bound)
        orr = _hgrn(to3(hq), to3(hf), to3(hi), to3(hg), lower_bounds, hgrn_norm_g[l][None],
                    tables, l, tc)
        xf = _out_ffn(xf, od.reshape(n, SEG), orr.reshape(n, SEG), w_out_b, ffn_norm_g[l][None],
                      w_gate_b, w_up_b, w_down_b, l, tm)
    return xf.reshape(b, s, d)
```

```python
import functools
import math

import numpy as np
import jax
import jax.numpy as jnp
from jax import lax
from jax.experimental import pallas as pl
from jax.experimental.pallas import tpu as pltpu

F32 = jnp.float32
BF16 = jnp.bfloat16

NORM_EPS = 1e-6
CHUNK = 64
CHUNK_SHIFT = 6
HEAD_LANES = 128
N_HEADS = 4
SEG = N_HEADS * HEAD_LANES
DIFF_HEAD_DIM = 64
N_SEG = 7
N_LEVELS = CHUNK_SHIFT

VMEM_LIMIT_BYTES = 52 * 1024 * 1024
FFN_TILE = 256
KEY_BLOCK = 1024
ATTN_SUB = 256
ONES_ROWS = 16
SCORE_BOUND_MAX = 40.0
_NT = (((1,), (1,)), ((), ()))


def _rsqrt_mean_sq(y):
    return lax.rsqrt(jnp.mean(y * y, axis=-1, keepdims=True) + NORM_EPS)


def _sigmoid(z):
    return 1.0 / (1.0 + jnp.exp(-z))


def _resident(block_shape, index_map):
    return pl.BlockSpec(block_shape, index_map, pipeline_mode=pl.Buffered(1))


def _in_proj_kernel(x_ref, g_ref, w_ref, qg_ref, kg_ref, gsum_ref,
                    qt_ref, k_ref, vt_ref, hq_ref, hf_ref, hi_ref, hg_ref):
    x = x_ref[...]
    h = (x * _rsqrt_mean_sq(x) * g_ref[...]).astype(BF16)

    def seg(s):
        return jnp.dot(h, w_ref[:, s * SEG:(s + 1) * SEG], preferred_element_type=F32)

    def head_norm(y, gain_ref, c):
        yc = y[:, c * HEAD_LANES:(c + 1) * HEAD_LANES]
        ss = jnp.dot((yc * yc).astype(BF16), gsum_ref[...], preferred_element_type=F32)
        return yc * lax.rsqrt(ss * (1.0 / DIFF_HEAD_DIM) + NORM_EPS) * gain_ref[...]

    yq = seg(0)
    yk = seg(1)
    for c in range(N_HEADS):
        lanes = slice(c * HEAD_LANES, (c + 1) * HEAD_LANES)
        qt_ref[lanes, :] = head_norm(yq, qg_ref, c).T.astype(BF16)
        k_ref[:, lanes] = head_norm(yk, kg_ref, c).astype(BF16)
    vt_ref[...] = seg(2).T.astype(BF16)
    qr = seg(3)
    hq_ref[...] = (qr * _sigmoid(qr)).astype(BF16)
    hf_ref[...] = seg(4)
    hi_ref[...] = seg(5).astype(BF16)
    gr = seg(6)
    hg_ref[...] = (gr * _sigmoid(gr)).astype(BF16)


def _in_proj(x, norm_g, w_in, layer, qg, kg, gsum, tm):
    n, d = x.shape
    row = lambda i: (i, 0)
    const2 = lambda i: (0, 0)
    out_bf = jax.ShapeDtypeStruct((n, SEG), BF16)
    out_f32 = jax.ShapeDtypeStruct((n, SEG), F32)
    seg_spec = pl.BlockSpec((tm, SEG), row)
    per_blk = KEY_BLOCK // tm
    return pl.pallas_call(
        _in_proj_kernel,
        grid=(n // tm,),
        in_specs=[
            pl.BlockSpec((tm, d), row),
            _resident((1, d), const2),
            _resident((None, d, N_SEG * SEG), lambda i: (layer, 0, 0)),
            _resident((1, HEAD_LANES), const2),
            _resident((1, HEAD_LANES), const2),
            _resident((HEAD_LANES, HEAD_LANES), const2),
        ],
        out_specs=[pl.BlockSpec((SEG, tm), lambda i: (0, i)), seg_spec,
                   pl.BlockSpec((None, SEG, tm), lambda i: (i // per_blk, 0, i % per_blk)),
                   seg_spec, seg_spec, seg_spec, seg_spec],
        out_shape=[jax.ShapeDtypeStruct((SEG, n), BF16), out_bf,
                   jax.ShapeDtypeStruct((n // KEY_BLOCK, SEG, KEY_BLOCK), BF16),
                   out_bf, out_f32, out_bf, out_bf],
        compiler_params=pltpu.CompilerParams(
            dimension_semantics=("arbitrary",), vmem_limit_bytes=VMEM_LIMIT_BYTES),
        name="in_proj",
    )(x, norm_g, w_in, qg, kg, gsum)


def _lambda_full(lq1_ref, lk1_ref, lq2_ref, lk2_ref, lam_init):
    return (jnp.exp(jnp.sum(lq1_ref[...] * lk1_ref[...], axis=-1, keepdims=True))
            - jnp.exp(jnp.sum(lq2_ref[...] * lk2_ref[...], axis=-1, keepdims=True))
            + lam_init)


def _split_maps_t(qt):
    row = lax.broadcasted_iota(jnp.int32, qt.shape, 0)
    zero = jnp.zeros_like(qt)
    return jnp.concatenate([jnp.where(row < DIFF_HEAD_DIM, qt, zero),
                            jnp.where(row >= DIFF_HEAD_DIM, qt, zero)], axis=1)


def _attn_bounded_kernel(qt_ref, k_ref, vt_ref, lq1_ref, lk1_ref, lq2_ref, lk2_ref, sgc_ref,
                         o_ref, acc_ref, *, lam_init):
    blk = pl.program_id(2)
    n_sub = KEY_BLOCK // ATTN_SUB
    ones = jnp.ones((ONES_ROWS, KEY_BLOCK), BF16)

    def q2t(s):
        return _split_maps_t(qt_ref[:, s * ATTN_SUB:(s + 1) * ATTN_SUB])

    def attend(s, k_rows, vt, diagonal):
        n_keys = k_rows.shape[0]
        st = jnp.dot(k_rows, q2t(s), preferred_element_type=F32)
        p = jnp.exp2(st)
        if diagonal:
            tail = p[n_keys - ATTN_SUB:]
            krow = lax.broadcasted_iota(jnp.int32, tail.shape, 0)
            qcol = lax.broadcasted_iota(jnp.int32, tail.shape, 1)
            visible = (krow >> CHUNK_SHIFT) <= ((qcol & (ATTN_SUB - 1)) >> CHUNK_SHIFT)
            tail = jnp.where(visible, tail, 0.0)
            p = tail if n_keys == ATTN_SUB else jnp.concatenate([p[:n_keys - ATTN_SUB], tail], axis=0)
        vte = jnp.concatenate([vt, ones[:, :n_keys]], axis=0)
        return jnp.dot(vte, p.astype(BF16), preferred_element_type=F32)

    start = pl.multiple_of(blk * KEY_BLOCK, KEY_BLOCK)
    for s in range(n_sub):
        n_keys = (s + 1) * ATTN_SUB
        acc_ref[s] = attend(s, k_ref[pl.ds(start, n_keys), :], vt_ref[blk, :, :n_keys], True)

    def body(j, carry):
        k_rows = k_ref[pl.ds(pl.multiple_of(j * KEY_BLOCK, KEY_BLOCK), KEY_BLOCK), :]
        vt = vt_ref[j]
        for s in range(n_sub):
            acc_ref[s] += attend(s, k_rows, vt, False)
        return carry

    lax.fori_loop(0, blk, body, 0)

    lam = _lambda_full(lq1_ref, lk1_ref, lq2_ref, lk2_ref, lam_init)
    for s in range(n_sub):
        a = acc_ref[s]
        o = a[:HEAD_LANES] / a[HEAD_LANES:HEAD_LANES + 1]
        od = o[:, :ATTN_SUB] - lam * o[:, ATTN_SUB:]
        ms = jnp.mean(od * od, axis=0, keepdims=True)
        od = od * lax.rsqrt(ms + NORM_EPS) * sgc_ref[...] * (1.0 - lam_init)
        o_ref[s * ATTN_SUB:(s + 1) * ATTN_SUB, :] = od.T.astype(BF16)


def _attn_online_kernel(qt_ref, k_ref, vt_ref, lq1_ref, lk1_ref, lq2_ref, lk2_ref, sgc_ref,
                        o_ref, v_ref, *, tq, lam_init):
    i = pl.program_id(2)

    @pl.when(i == 0)
    def _():
        for j in range(vt_ref.shape[0]):
            v_ref[j * KEY_BLOCK:(j + 1) * KEY_BLOCK, :] = vt_ref[j].astype(F32).T.astype(BF16)

    q = qt_ref[...].astype(F32).T.astype(BF16)
    lane = lax.broadcasted_iota(jnp.int32, q.shape, 1)
    zero = jnp.zeros_like(q)
    q2 = jnp.concatenate([jnp.where(lane < DIFF_HEAD_DIM, q, zero),
                          jnp.where(lane >= DIFF_HEAD_DIM, q, zero)], axis=0)

    def scores(kj):
        return lax.dot_general(q2, kj, _NT, preferred_element_type=F32)

    start = pl.multiple_of(i * tq, tq)
    s = scores(k_ref[pl.ds(start, tq), :])
    row = lax.broadcasted_iota(jnp.int32, s.shape, 0)
    col = lax.broadcasted_iota(jnp.int32, s.shape, 1)
    visible = (col >> CHUNK_SHIFT) <= ((row & (tq - 1)) >> CHUNK_SHIFT)
    s = jnp.where(visible, s, -jnp.inf)
    m = jnp.max(s, axis=-1, keepdims=True)
    p = jnp.exp2(s - m)
    l = jnp.sum(p, axis=-1, keepdims=True)
    acc = jnp.dot(p.astype(BF16), v_ref[pl.ds(start, tq), :], preferred_element_type=F32)

    def body(j, carry):
        m, l, acc = carry
        off = pl.multiple_of(j * tq, tq)
        s = scores(k_ref[pl.ds(off, tq), :])
        m_new = jnp.maximum(m, jnp.max(s, axis=-1, keepdims=True))
        alpha = jnp.exp2(m - m_new)
        p = jnp.exp2(s - m_new)
        l = alpha * l + jnp.sum(p, axis=-1, keepdims=True)
        acc = alpha * acc + jnp.dot(p.astype(BF16), v_ref[pl.ds(off, tq), :],
                                    preferred_element_type=F32)
        return m_new, l, acc

    m, l, acc = lax.fori_loop(0, i, body, (m, l, acc))
    o = acc / l
    lam = _lambda_full(lq1_ref, lk1_ref, lq2_ref, lk2_ref, lam_init)
    od = o[:tq] - lam * o[tq:]
    od = od * _rsqrt_mean_sq(od) * sgc_ref[...].T * (1.0 - lam_init)
    o_ref[...] = od.astype(BF16)


def _diff_attn(qt, k, vt, lq1, lk1, lq2, lk2, subln_g_col, lam_init, batch, score_bound):
    n = k.shape[0]
    s = n // batch
    blocks = s // KEY_BLOCK
    lam_spec = _resident((1, DIFF_HEAD_DIM), lambda bi, hi, qi: (0, 0))
    k_spec = pl.BlockSpec((s, HEAD_LANES), lambda bi, hi, qi: (bi, hi))
    vt_spec = pl.BlockSpec((blocks, HEAD_LANES, KEY_BLOCK), lambda bi, hi, qi: (bi, hi, 0))
    gain_spec = _resident((HEAD_LANES, 1), lambda bi, hi, qi: (0, 0))

    def call(body, tq, scratch, name):
        per_batch = s // tq
        return pl.pallas_call(
            body,
            grid=(batch, N_HEADS, per_batch),
            in_specs=[pl.BlockSpec((HEAD_LANES, tq), lambda bi, hi, qi: (hi, bi * per_batch + qi)),
                      k_spec, vt_spec, lam_spec, lam_spec, lam_spec, lam_spec, gain_spec],
            out_specs=pl.BlockSpec((tq, HEAD_LANES), lambda bi, hi, qi: (bi * per_batch + qi, hi)),
            out_shape=jax.ShapeDtypeStruct((n, SEG), BF16),
            scratch_shapes=scratch,
            compiler_params=pltpu.CompilerParams(
                dimension_semantics=("arbitrary", "arbitrary", "arbitrary"),
                vmem_limit_bytes=VMEM_LIMIT_BYTES),
            name=name,
        )

    bounded = call(
        functools.partial(_attn_bounded_kernel, lam_init=lam_init), KEY_BLOCK,
        [pltpu.VMEM((KEY_BLOCK // ATTN_SUB, HEAD_LANES + ONES_ROWS, 2 * ATTN_SUB), F32)],
        "diff_attn_bounded")
    online = call(
        functools.partial(_attn_online_kernel, tq=ATTN_SUB, lam_init=lam_init), ATTN_SUB,
        [pltpu.VMEM((s, HEAD_LANES), BF16)], "diff_attn_online")
    return lax.cond(score_bound <= SCORE_BOUND_MAX, bounded, online,
                    qt, k, vt, lq1, lk1, lq2, lk2, subln_g_col)


def _hgrn_tables():
    c = CHUNK
    sums = np.zeros((N_LEVELS + 2, c, c), np.float32)
    masks = np.zeros((N_LEVELS + 1, c, c), np.float32)
    upper = np.zeros((N_LEVELS, c, 1), np.float32)
    for lev in range(N_LEVELS):
        size = 2 << lev
        half = size // 2
        for r in range(c):
            mid = r - r % size + half
            if r % size >= half:
                upper[lev, r, 0] = 1.0
                sums[lev, r, mid:r + 1] = 1.0
                masks[lev, r, mid - half:mid] = 1.0
            else:
                sums[lev, r, r + 1:mid] = 1.0
    for r in range(c):
        sums[N_LEVELS, r, :r + 1] = 1.0
        sums[N_LEVELS + 1, r, r + 1:] = 1.0
        masks[N_LEVELS, r, r] = 1.0
    return sums.reshape((N_LEVELS + 2) * c, c), masks, upper


def _hgrn_kernel(hq_ref, hf_ref, hi_ref, hg_ref, lb_ref, g_ref, sums_ref, masks_ref,
                 upper_ref, o_ref, state_ref, *, layer, n_chunks):
    @pl.when(pl.program_id(1) == 0)
    def _():
        state_ref[...] = jnp.zeros_like(state_ref)

    if layer > 0:
        lbp = lb_ref[...]
        ex = jnp.exp(lbp - jnp.max(lbp, axis=0, keepdims=True))
        gamma = ex / jnp.sum(ex, axis=0, keepdims=True)
        lb = jnp.sum(gamma[1:layer + 1], axis=0, keepdims=True)

    nt = (((1,), (1,)), ((), ()))
    tn = (((0,), (0,)), ((), ()))

    def chunk_body(c, carry):
        rows = pl.ds(pl.multiple_of(c * CHUNK, CHUNK), CHUNK)
        z = hf_ref[rows, :]
        e = jnp.exp(-jnp.abs(z))
        r = 1.0 / (1.0 + e)
        sig_pos = jnp.where(z >= 0, r, e * r)
        sig_neg = jnp.where(z >= 0, e * r, r)
        if layer == 0:
            lf = jnp.minimum(z, 0.0) - jnp.log1p(e)
            kk = sig_neg
        else:
            lf = jnp.log(lb + (1.0 - lb) * sig_pos)
            kk = (1.0 - lb) * sig_neg
        lf_hi = lf.astype(BF16)
        lf_lo = (lf - lf_hi.astype(F32)).astype(BF16)
        args = (jnp.dot(sums_ref[...], lf_hi, preferred_element_type=F32)
                + jnp.dot(sums_ref[...], lf_lo, preferred_element_type=F32))
        decay = jnp.exp(args)
        qq = hq_ref[rows, :].astype(F32)
        for h in range(N_HEADS):
            hs = slice(h * HEAD_LANES, (h + 1) * HEAD_LANES)
            q_h = qq[:, hs]
            k_h = kk[:, hs]
            v_h = hi_ref[rows, hs]
            a = jnp.where(masks_ref[N_LEVELS] > 0.5,
                          lax.dot_general(q_h.astype(BF16), k_h.astype(BF16), nt,
                                          preferred_element_type=F32), 0.0)
            for lev in range(N_LEVELS):
                x = (jnp.where(upper_ref[lev] > 0.5, q_h, k_h)
                     * decay[lev * CHUNK:(lev + 1) * CHUNK, hs]).astype(BF16)
                a = a + jnp.where(masks_ref[lev] > 0.5,
                                  lax.dot_general(x, x, nt, preferred_element_type=F32), 0.0)
            pre = decay[N_LEVELS * CHUNK:(N_LEVELS + 1) * CHUNK, hs]
            suf = decay[(N_LEVELS + 1) * CHUNK:(N_LEVELS + 2) * CHUNK, hs]
            st = state_ref[h]
            o = (jnp.dot(a.astype(BF16), v_h, preferred_element_type=F32)
                 + lax.dot_general((q_h * pre).astype(BF16), st.astype(BF16), nt,
                                   preferred_element_type=F32))
            state_ref[h] = (st * pre[CHUNK - 1:CHUNK, :]
                            + lax.dot_general(v_h, (k_h * suf).astype(BF16), tn,
                                              preferred_element_type=F32))
            o = o * _rsqrt_mean_sq(o) * g_ref[...] * hg_ref[rows, hs].astype(F32)
            o_ref[rows, hs] = o.astype(BF16)
        return carry

    lax.fori_loop(0, n_chunks, chunk_body, 0)


def _hgrn(hq, hf, hi, hg, lower_bounds, norm_g, tables, layer, tc):
    b, s, _ = hq.shape
    sums, masks, upper = tables
    tile = pl.BlockSpec((None, tc, SEG), lambda bi, ci: (bi, ci, 0))
    const2 = lambda bi, ci: (0, 0)
    const3 = lambda bi, ci: (0, 0, 0)
    return pl.pallas_call(
        functools.partial(_hgrn_kernel, layer=layer, n_chunks=tc // CHUNK),
        grid=(b, s // tc),
        in_specs=[tile, tile, tile, tile,
                  _resident(lower_bounds.shape, const2),
                  _resident((1, HEAD_LANES), const2),
                  _resident(sums.shape, const2),
                  _resident(masks.shape, const3),
                  _resident(upper.shape, const3)],
        out_specs=tile,
        out_shape=jax.ShapeDtypeStruct((b, s, SEG), BF16),
        scratch_shapes=[pltpu.VMEM((N_HEADS, HEAD_LANES, HEAD_LANES), F32)],
        compiler_params=pltpu.CompilerParams(
            dimension_semantics=("arbitrary", "arbitrary"),
            vmem_limit_bytes=VMEM_LIMIT_BYTES),
        name="hgrn2",
    )(hq, hf, hi, hg, lower_bounds, norm_g, sums, masks, upper)


def _out_ffn_kernel(x_ref, od_ref, or_ref, wo_ref, g_ref, wg_ref, wu_ref, wd_ref,
                    o_ref, ff_ref, *, d_ff):
    xn = (x_ref[...]
          + jnp.dot(od_ref[...], wo_ref[:SEG, :], preferred_element_type=F32)
          + jnp.dot(or_ref[...], wo_ref[SEG:, :], preferred_element_type=F32))
    h = (xn * _rsqrt_mean_sq(xn) * g_ref[...]).astype(BF16)
    for j in range(d_ff // FFN_TILE):
        cols = slice(j * FFN_TILE, (j + 1) * FFN_TILE)
        gate = jnp.dot(h, wg_ref[:, cols], preferred_element_type=F32)
        up = jnp.dot(h, wu_ref[:, cols], preferred_element_type=F32)
        ff_ref[:, cols] = (gate * _sigmoid(gate) * up).astype(BF16)
    o_ref[...] = xn + jnp.dot(ff_ref[...], wd_ref[...], preferred_element_type=F32)


def _out_ffn(x, od, orr, w_out, norm_g, w_gate, w_up, w_down, layer, tm):
    n, d = x.shape
    d_ff = w_gate.shape[-1]
    row = lambda i: (i, 0)
    lay = lambda i: (layer, 0, 0)
    return pl.pallas_call(
        functools.partial(_out_ffn_kernel, d_ff=d_ff),
        grid=(n // tm,),
        in_specs=[
            pl.BlockSpec((tm, d), row),
            pl.BlockSpec((tm, SEG), row),
            pl.BlockSpec((tm, SEG), row),
            _resident((None, d, d), lay),
            _resident((1, d), lambda i: (0, 0)),
            _resident((None, d, d_ff), lay),
            _resident((None, d, d_ff), lay),
            _resident((None, d_ff, d), lay),
        ],
        out_specs=pl.BlockSpec((tm, d), row),
        out_shape=jax.ShapeDtypeStruct((n, d), F32),
        scratch_shapes=[pltpu.VMEM((tm, d_ff), BF16)],
        compiler_params=pltpu.CompilerParams(
            dimension_semantics=("arbitrary",), vmem_limit_bytes=VMEM_LIMIT_BYTES),
        name="out_ffn",
    )(x, od, orr, w_out, norm_g, w_gate, w_up, w_down)


def _tile(n, want):
    return min(n, want)


def kernel(x, attn_norm_g, w_in, q_norm_g, k_norm_g, lambda_q1, lambda_k1, lambda_q2, lambda_k2, subln_g, lower_bounds, hgrn_norm_g, w_out, ffn_norm_g, w_gate, w_up, w_down):
    b, s, d = x.shape
    depth = w_in.shape[0]
    n = b * s
    assert d == 2 * SEG and w_in.shape[-1] == N_SEG * SEG and s % KEY_BLOCK == 0

    w_in_b = w_in.astype(BF16)
    w_out_b = w_out.astype(BF16)
    w_gate_b = w_gate.astype(BF16)
    w_up_b = w_up.astype(BF16)
    w_down_b = w_down.astype(BF16)

    group = np.arange(HEAD_LANES) // DIFF_HEAD_DIM
    gsum = jnp.asarray(group[:, None] == group[None, :], BF16)
    sums, masks, upper = _hgrn_tables()
    tables = (jnp.asarray(sums, BF16), jnp.asarray(masks), jnp.asarray(upper))
    lower_bounds = lower_bounds.astype(F32)

    tm = _tile(n, 512)
    tc = _tile(s, 256)

    xf = x.reshape(n, d)
    for l in range(depth):
        lam_init = 0.8 - 0.6 * math.exp(-0.3 * l)
        qg = jnp.tile(q_norm_g[l], 2)[None] * (DIFF_HEAD_DIM ** -0.5 * math.log2(math.e))
        kg = jnp.tile(k_norm_g[l], 2)[None]
        score_bound = (1.02 * math.sqrt(DIFF_HEAD_DIM)
                       * jnp.max(jnp.abs(q_norm_g[l])) * jnp.max(jnp.abs(k_norm_g[l])))
        qt, k, vt, hq, hf, hi, hg = _in_proj(xf, attn_norm_g[l][None], w_in_b, l, qg, kg, gsum, tm)
        to3 = lambda t: t.reshape(b, s, SEG)
        od = _diff_attn(qt, k, vt, lambda_q1[l][None], lambda_k1[l][None],
                        lambda_q2[l][None], lambda_k2[l][None], subln_g[l][:, None], lam_init, b,
                        score_bound)
        orr = _hgrn(to3(hq), to3(hf), to3(hi), to3(hg), lower_bounds, hgrn_norm_g[l][None],
                    tables, l, tc)
        xf = _out_ffn(xf, od, orr.reshape(n, SEG), w_out_b, ffn_norm_g[l][None],
                      w_gate_b, w_up_b, w_down_b, l, tm)
    return xf.reshape(b, s, d)
```

```python
import functools
import math

import numpy as np
import jax
import jax.numpy as jnp
from jax import lax
from jax.experimental import pallas as pl
from jax.experimental.pallas import tpu as pltpu

F32 = jnp.float32
BF16 = jnp.bfloat16

NORM_EPS = 1e-6
CHUNK = 64
CHUNK_SHIFT = 6
HEAD_LANES = 128
N_HEADS = 4
SEG = N_HEADS * HEAD_LANES
DIFF_HEAD_DIM = 64
N_SEG = 7
N_LEVELS = CHUNK_SHIFT

VMEM_LIMIT_BYTES = 52 * 1024 * 1024
FFN_TILE = 256
KEY_BLOCK = 1024
ATTN_SUB = 256
ONES_ROWS = 16
SCORE_BOUND_MAX = 40.0
_NT = (((1,), (1,)), ((), ()))
SUBLANES = 8
LOG2E = math.log2(math.e)


def _rsqrt_mean_sq(y):
    return lax.rsqrt(jnp.mean(y * y, axis=-1, keepdims=True) + NORM_EPS)


def _sigmoid(z):
    return 1.0 / (1.0 + jnp.exp(-z))


def _resident(block_shape, index_map):
    return pl.BlockSpec(block_shape, index_map, pipeline_mode=pl.Buffered(1))


def _in_proj_kernel(x_ref, g_ref, w_ref, qg_ref, kg_ref, gsum_ref,
                    qt_ref, k_ref, vt_ref, hq_ref, hf_ref, hi_ref, hg_ref):
    x = x_ref[...]
    h = (x * _rsqrt_mean_sq(x) * g_ref[...]).astype(BF16)

    def seg(s):
        return jnp.dot(h, w_ref[:, s * SEG:(s + 1) * SEG], preferred_element_type=F32)

    def head_norm(y, gain_ref, c):
        yc = y[:, c * HEAD_LANES:(c + 1) * HEAD_LANES]
        ss = jnp.dot((yc * yc).astype(BF16), gsum_ref[...], preferred_element_type=F32)
        return yc * lax.rsqrt(ss * (1.0 / DIFF_HEAD_DIM) + NORM_EPS) * gain_ref[...]

    yq = seg(0)
    yk = seg(1)
    for c in range(N_HEADS):
        lanes = slice(c * HEAD_LANES, (c + 1) * HEAD_LANES)
        qt_ref[lanes, :] = head_norm(yq, qg_ref, c).T.astype(BF16)
        k_ref[:, lanes] = head_norm(yk, kg_ref, c).astype(BF16)
    vt_ref[...] = seg(2).T.astype(BF16)
    qr = seg(3)
    hq_ref[...] = (qr * _sigmoid(qr)).astype(BF16)
    hf_ref[...] = seg(4)
    hi_ref[...] = seg(5).astype(BF16)
    gr = seg(6)
    hg_ref[...] = (gr * _sigmoid(gr)).astype(BF16)


def _in_proj(x, norm_g, w_in, layer, qg, kg, gsum, tm):
    n, d = x.shape
    row = lambda i: (i, 0)
    const2 = lambda i: (0, 0)
    out_bf = jax.ShapeDtypeStruct((n, SEG), BF16)
    out_f32 = jax.ShapeDtypeStruct((n, SEG), F32)
    seg_spec = pl.BlockSpec((tm, SEG), row)
    per_blk = KEY_BLOCK // tm
    return pl.pallas_call(
        _in_proj_kernel,
        grid=(n // tm,),
        in_specs=[
            pl.BlockSpec((tm, d), row),
            _resident((1, d), const2),
            _resident((None, d, N_SEG * SEG), lambda i: (layer, 0, 0)),
            _resident((1, HEAD_LANES), const2),
            _resident((1, HEAD_LANES), const2),
            _resident((HEAD_LANES, HEAD_LANES), const2),
        ],
        out_specs=[pl.BlockSpec((SEG, tm), lambda i: (0, i)), seg_spec,
                   pl.BlockSpec((None, SEG, tm), lambda i: (i // per_blk, 0, i % per_blk)),
                   seg_spec, seg_spec, seg_spec, seg_spec],
        out_shape=[jax.ShapeDtypeStruct((SEG, n), BF16), out_bf,
                   jax.ShapeDtypeStruct((n // KEY_BLOCK, SEG, KEY_BLOCK), BF16),
                   out_bf, out_f32, out_bf, out_bf],
        compiler_params=pltpu.CompilerParams(
            dimension_semantics=("arbitrary",), vmem_limit_bytes=VMEM_LIMIT_BYTES),
        name="in_proj",
    )(x, norm_g, w_in, qg, kg, gsum)


def _lambda_full(lq1_ref, lk1_ref, lq2_ref, lk2_ref, lam_init):
    return (jnp.exp(jnp.sum(lq1_ref[...] * lk1_ref[...], axis=-1, keepdims=True))
            - jnp.exp(jnp.sum(lq2_ref[...] * lk2_ref[...], axis=-1, keepdims=True))
            + lam_init)


def _split_maps_t(qt):
    row = lax.broadcasted_iota(jnp.int32, qt.shape, 0)
    zero = jnp.zeros_like(qt)
    return jnp.concatenate([jnp.where(row < DIFF_HEAD_DIM, qt, zero),
                            jnp.where(row >= DIFF_HEAD_DIM, qt, zero)], axis=1)


def _attn_bounded_kernel(qt_ref, k_ref, vt_ref, lq1_ref, lk1_ref, lq2_ref, lk2_ref, sgc_ref,
                         o_ref, acc_ref, *, lam_init):
    blk = pl.program_id(2)
    n_sub = KEY_BLOCK // ATTN_SUB
    ones = jnp.ones((ONES_ROWS, KEY_BLOCK), BF16)

    def q2t(s):
        return _split_maps_t(qt_ref[:, s * ATTN_SUB:(s + 1) * ATTN_SUB])

    def attend(s, k_rows, vt, diagonal):
        n_keys = k_rows.shape[0]
        st = jnp.dot(k_rows, q2t(s), preferred_element_type=F32)
        p = jnp.exp2(st)
        if diagonal:
            tail = p[n_keys - ATTN_SUB:]
            krow = lax.broadcasted_iota(jnp.int32, tail.shape, 0)
            qcol = lax.broadcasted_iota(jnp.int32, tail.shape, 1)
            visible = (krow >> CHUNK_SHIFT) <= ((qcol & (ATTN_SUB - 1)) >> CHUNK_SHIFT)
            tail = jnp.where(visible, tail, 0.0)
            p = tail if n_keys == ATTN_SUB else jnp.concatenate([p[:n_keys - ATTN_SUB], tail], axis=0)
        vte = jnp.concatenate([vt, ones[:, :n_keys]], axis=0)
        return jnp.dot(vte, p.astype(BF16), preferred_element_type=F32)

    start = pl.multiple_of(blk * KEY_BLOCK, KEY_BLOCK)
    for s in range(n_sub):
        n_keys = (s + 1) * ATTN_SUB
        acc_ref[s] = attend(s, k_ref[pl.ds(start, n_keys), :], vt_ref[blk, :, :n_keys], True)

    def body(j, carry):
        k_rows = k_ref[pl.ds(pl.multiple_of(j * KEY_BLOCK, KEY_BLOCK), KEY_BLOCK), :]
        vt = vt_ref[j]
        for s in range(n_sub):
            acc_ref[s] += attend(s, k_rows, vt, False)
        return carry

    lax.fori_loop(0, blk, body, 0)

    lam = _lambda_full(lq1_ref, lk1_ref, lq2_ref, lk2_ref, lam_init)
    for s in range(n_sub):
        a = acc_ref[s]
        o = a[:HEAD_LANES] / a[HEAD_LANES:HEAD_LANES + 1]
        od = o[:, :ATTN_SUB] - lam * o[:, ATTN_SUB:]
        ms = jnp.mean(od * od, axis=0, keepdims=True)
        od = od * lax.rsqrt(ms + NORM_EPS) * sgc_ref[...] * (1.0 - lam_init)
        o_ref[s * ATTN_SUB:(s + 1) * ATTN_SUB, :] = od.T.astype(BF16)


def _attn_online_kernel(qt_ref, k_ref, vt_ref, lq1_ref, lk1_ref, lq2_ref, lk2_ref, sgc_ref,
                        o_ref, v_ref, *, tq, lam_init):
    i = pl.program_id(2)

    @pl.when(i == 0)
    def _():
        for j in range(vt_ref.shape[0]):
            v_ref[j * KEY_BLOCK:(j + 1) * KEY_BLOCK, :] = vt_ref[j].astype(F32).T.astype(BF16)

    q = qt_ref[...].astype(F32).T.astype(BF16)
    lane = lax.broadcasted_iota(jnp.int32, q.shape, 1)
    zero = jnp.zeros_like(q)
    q2 = jnp.concatenate([jnp.where(lane < DIFF_HEAD_DIM, q, zero),
                          jnp.where(lane >= DIFF_HEAD_DIM, q, zero)], axis=0)

    def scores(kj):
        return lax.dot_general(q2, kj, _NT, preferred_element_type=F32)

    start = pl.multiple_of(i * tq, tq)
    s = scores(k_ref[pl.ds(start, tq), :])
    row = lax.broadcasted_iota(jnp.int32, s.shape, 0)
    col = lax.broadcasted_iota(jnp.int32, s.shape, 1)
    visible = (col >> CHUNK_SHIFT) <= ((row & (tq - 1)) >> CHUNK_SHIFT)
    s = jnp.where(visible, s, -jnp.inf)
    m = jnp.max(s, axis=-1, keepdims=True)
    p = jnp.exp2(s - m)
    l = jnp.sum(p, axis=-1, keepdims=True)
    acc = jnp.dot(p.astype(BF16), v_ref[pl.ds(start, tq), :], preferred_element_type=F32)

    def body(j, carry):
        m, l, acc = carry
        off = pl.multiple_of(j * tq, tq)
        s = scores(k_ref[pl.ds(off, tq), :])
        m_new = jnp.maximum(m, jnp.max(s, axis=-1, keepdims=True))
        alpha = jnp.exp2(m - m_new)
        p = jnp.exp2(s - m_new)
        l = alpha * l + jnp.sum(p, axis=-1, keepdims=True)
        acc = alpha * acc + jnp.dot(p.astype(BF16), v_ref[pl.ds(off, tq), :],
                                    preferred_element_type=F32)
        return m_new, l, acc

    m, l, acc = lax.fori_loop(0, i, body, (m, l, acc))
    o = acc / l
    lam = _lambda_full(lq1_ref, lk1_ref, lq2_ref, lk2_ref, lam_init)
    od = o[:tq] - lam * o[tq:]
    od = od * _rsqrt_mean_sq(od) * sgc_ref[...].T * (1.0 - lam_init)
    o_ref[...] = od.astype(BF16)


def _diff_attn(qt, k, vt, lq1, lk1, lq2, lk2, subln_g_col, lam_init, batch, score_bound):
    n = k.shape[0]
    s = n // batch
    blocks = s // KEY_BLOCK
    lam_spec = _resident((1, DIFF_HEAD_DIM), lambda bi, hi, qi: (0, 0))
    k_spec = pl.BlockSpec((s, HEAD_LANES), lambda bi, hi, qi: (bi, hi))
    vt_spec = pl.BlockSpec((blocks, HEAD_LANES, KEY_BLOCK), lambda bi, hi, qi: (bi, hi, 0))
    gain_spec = _resident((HEAD_LANES, 1), lambda bi, hi, qi: (0, 0))

    def call(body, tq, scratch, name):
        per_batch = s // tq
        return pl.pallas_call(
            body,
            grid=(batch, N_HEADS, per_batch),
            in_specs=[pl.BlockSpec((HEAD_LANES, tq), lambda bi, hi, qi: (hi, bi * per_batch + qi)),
                      k_spec, vt_spec, lam_spec, lam_spec, lam_spec, lam_spec, gain_spec],
            out_specs=pl.BlockSpec((tq, HEAD_LANES), lambda bi, hi, qi: (bi * per_batch + qi, hi)),
            out_shape=jax.ShapeDtypeStruct((n, SEG), BF16),
            scratch_shapes=scratch,
            compiler_params=pltpu.CompilerParams(
                dimension_semantics=("arbitrary", "arbitrary", "arbitrary"),
                vmem_limit_bytes=VMEM_LIMIT_BYTES),
            name=name,
        )

    bounded = call(
        functools.partial(_attn_bounded_kernel, lam_init=lam_init), KEY_BLOCK,
        [pltpu.VMEM((KEY_BLOCK // ATTN_SUB, HEAD_LANES + ONES_ROWS, 2 * ATTN_SUB), F32)],
        "diff_attn_bounded")
    online = call(
        functools.partial(_attn_online_kernel, tq=ATTN_SUB, lam_init=lam_init), ATTN_SUB,
        [pltpu.VMEM((s, HEAD_LANES), BF16)], "diff_attn_online")
    return lax.cond(score_bound <= SCORE_BOUND_MAX, bounded, online,
                    qt, k, vt, lq1, lk1, lq2, lk2, subln_g_col)


def _hgrn_tables():
    c = CHUNK
    masks = np.zeros((N_LEVELS + 1, c, c), np.float32)
    masks[0] = np.eye(c, dtype=np.float32)
    for lev in range(1, N_LEVELS + 1):
        size = 1 << lev
        half = size // 2
        for r in range(c):
            if r % size >= half:
                mid = r - r % size + half
                masks[lev, r, mid - half:mid] = 1.0
    padded = np.zeros((N_LEVELS + 1, 2, c, 2 * c), np.float32)
    padded[:, 0, :, :c] = masks
    padded[:, 1, :, c:] = masks
    return np.tril(np.ones((c, c), np.float32)), padded


def _level_ref_row(lev, row):
    size = 1 << lev
    return row - row % size + size // 2 - 1


def _hgrn_kernel(hq_ref, hf_ref, hi_ref, hg_ref, lb_ref, g_ref, tril_ref, masks_ref,
                 o_ref, state_ref, b_ref, *, layer, n_chunks):
    @pl.when(pl.program_id(1) == 0)
    def _():
        state_ref[...] = jnp.zeros_like(state_ref)

    if layer > 0:
        lbp = lb_ref[...]
        ex = jnp.exp(lbp - jnp.max(lbp, axis=0, keepdims=True))
        gamma = ex / jnp.sum(ex, axis=0, keepdims=True)
        lb = jnp.sum(gamma[1:layer + 1], axis=0, keepdims=True)

    tn = (((0,), (0,)), ((), ()))
    groups = CHUNK // SUBLANES
    sub = lax.broadcasted_iota(jnp.int32, (SUBLANES, SEG), 0)
    up_small = {lev: (sub & (1 << (lev - 1))) != 0 for lev in (1, 2, 3)}
    sgn_small = {lev: jnp.where(up_small[lev], 1.0, -1.0) for lev in (2, 3)}

    def chunk_body(c, carry):
        rows = pl.ds(pl.multiple_of(c * CHUNK, CHUNK), CHUNK)
        z = hf_ref[rows, :]
        e = jnp.exp(-jnp.abs(z))
        r = 1.0 / (1.0 + e)
        sig_neg = jnp.where(z >= 0, e * r, r)
        if layer == 0:
            lf2 = jnp.minimum(z * LOG2E, 0.0) - jnp.log2(1.0 + e)
            kk = sig_neg
        else:
            sig_pos = jnp.where(z >= 0, r, e * r)
            lf2 = jnp.log2(lb + (1.0 - lb) * sig_pos)
            kk = (1.0 - lb) * sig_neg
        hi = lf2.astype(BF16)
        lo = (lf2 - hi.astype(F32)).astype(BF16)
        b2 = (jnp.dot(tril_ref[...], hi, preferred_element_type=F32)
              + jnp.dot(tril_ref[...], lo, preferred_element_type=F32))
        b_ref[...] = b2
        qq = hq_ref[rows, :].astype(F32)

        def grp(a, g):
            return a[g * SUBLANES:(g + 1) * SUBLANES, :]

        def ref_rows(row):
            return jnp.broadcast_to(b_ref[row:row + 1, :], (SUBLANES, SEG))

        x = {lev: [] for lev in range(1, N_LEVELS + 1)}
        q_pre, k_suf = [], []
        for g in range(groups):
            b_g, q_g, k_g = grp(b2, g), grp(qq, g), grp(kk, g)
            row0 = g * SUBLANES
            x[1].append(jnp.where(up_small[1], q_g * jnp.exp2(grp(lf2, g)), k_g))
            ref2 = jnp.where(sub < 4, ref_rows(_level_ref_row(2, row0)),
                             ref_rows(_level_ref_row(2, row0 + 4)))
            x[2].append(jnp.where(up_small[2], q_g, k_g) * jnp.exp2((b_g - ref2) * sgn_small[2]))
            ref3 = ref_rows(_level_ref_row(3, row0))
            x[3].append(jnp.where(up_small[3], q_g, k_g) * jnp.exp2((b_g - ref3) * sgn_small[3]))
            for lev in range(4, N_LEVELS + 1):
                ref = ref_rows(_level_ref_row(lev, row0))
                if row0 % (1 << lev) >= (1 << lev) // 2:
                    x[lev].append(q_g * jnp.exp2(b_g - ref))
                else:
                    x[lev].append(k_g * jnp.exp2(ref - b_g))
            q_pre.append(q_g * jnp.exp2(b_g))
            k_suf.append(k_g * jnp.exp2(ref_rows(CHUNK - 1) - b_g))

        def stack(parts):
            return jnp.concatenate(parts, axis=0).astype(BF16)

        def heads_to_rows(a):
            return jnp.concatenate(
                [a[:, h * HEAD_LANES:(h + 1) * HEAD_LANES] for h in range(N_HEADS)], axis=0)

        upper_groups = {lev: [g for g in range(groups)
                              if (g * SUBLANES) % (1 << lev) >= (1 << lev) // 2]
                        for lev in range(4, N_LEVELS + 1)}
        s_lev = {}
        for lev in range(1, N_LEVELS + 1):
            keys = heads_to_rows(stack(x[lev]))
            if lev >= 4:
                lhs = heads_to_rows(stack([x[lev][g] for g in upper_groups[lev]]))
            else:
                lhs = keys
            s_lev[lev] = lax.dot_general(lhs, keys, _NT, preferred_element_type=F32)

        pair_blocks = []
        for pair in range(N_HEADS // 2):
            lanes = slice(pair * HEAD_LANES, (pair + 1) * HEAD_LANES)
            a_rows = []
            for h in (2 * pair, 2 * pair + 1):
                for g in range(groups):
                    gs = slice(g * SUBLANES, (g + 1) * SUBLANES)
                    hs = slice(h * HEAD_LANES, (h + 1) * HEAD_LANES)
                    a_g = (jnp.sum(qq[gs, hs] * kk[gs, hs], axis=-1, keepdims=True)
                           * masks_ref[0, h % 2, gs, :])
                    for lev in range(1, N_LEVELS + 1):
                        if lev >= 4:
                            if g not in upper_groups[lev]:
                                continue
                            n_up = len(upper_groups[lev])
                            r0 = (h * n_up + upper_groups[lev].index(g)) * SUBLANES
                        else:
                            r0 = h * CHUNK + g * SUBLANES
                        a_g = a_g + s_lev[lev][r0:r0 + SUBLANES, lanes] * masks_ref[lev, h % 2, gs, :]
                    a_rows.append(a_g)
            pair_blocks.append(jnp.concatenate(a_rows, axis=0))
        zero = jnp.zeros_like(pair_blocks[0])
        a_full = jnp.concatenate(
            [jnp.concatenate([pair_blocks[0], zero], axis=1),
             jnp.concatenate([zero, pair_blocks[1]], axis=1)], axis=0).astype(BF16)
        o_intra = jnp.dot(a_full, heads_to_rows(hi_ref[rows, :]), preferred_element_type=F32)

        q_pre = stack(q_pre)
        k_suf = stack(k_suf)
        decay_last = jnp.exp2(b2[CHUNK - 1:CHUNK, :])
        for h in range(N_HEADS):
            hs = slice(h * HEAD_LANES, (h + 1) * HEAD_LANES)
            st = state_ref[h]
            o = (o_intra[h * CHUNK:(h + 1) * CHUNK, :]
                 + lax.dot_general(q_pre[:, hs], st.astype(BF16), _NT,
                                   preferred_element_type=F32))
            state_ref[h] = (st * decay_last[:, hs]
                            + lax.dot_general(hi_ref[rows, hs], k_suf[:, hs], tn,
                                              preferred_element_type=F32))
            o = o * _rsqrt_mean_sq(o) * g_ref[...] * hg_ref[rows, hs].astype(F32)
            o_ref[rows, hs] = o.astype(BF16)
        return carry

    lax.fori_loop(0, n_chunks, chunk_body, 0, unroll=True)


def _hgrn(hq, hf, hi, hg, lower_bounds, norm_g, tables, layer, tc):
    b, s, _ = hq.shape
    tril, masks = tables
    tile = pl.BlockSpec((None, tc, SEG), lambda bi, ci: (bi, ci, 0))
    const2 = lambda bi, ci: (0, 0)
    const4 = lambda bi, ci: (0, 0, 0, 0)
    return pl.pallas_call(
        functools.partial(_hgrn_kernel, layer=layer, n_chunks=tc // CHUNK),
        grid=(b, s // tc),
        in_specs=[tile, tile, tile, tile,
                  _resident(lower_bounds.shape, const2),
                  _resident((1, HEAD_LANES), const2),
                  _resident(tril.shape, const2),
                  _resident(masks.shape, const4)],
        out_specs=tile,
        out_shape=jax.ShapeDtypeStruct((b, s, SEG), BF16),
        scratch_shapes=[pltpu.VMEM((N_HEADS, HEAD_LANES, HEAD_LANES), F32),
                        pltpu.VMEM((CHUNK, SEG), F32)],
        compiler_params=pltpu.CompilerParams(
            dimension_semantics=("arbitrary", "arbitrary"),
            vmem_limit_bytes=VMEM_LIMIT_BYTES),
        name="hgrn2",
    )(hq, hf, hi, hg, lower_bounds, norm_g, tril, masks)


def _out_ffn_kernel(x_ref, od_ref, or_ref, wo_ref, g_ref, wg_ref, wu_ref, wd_ref,
                    o_ref, ff_ref, *, d_ff):
    xn = (x_ref[...]
          + jnp.dot(od_ref[...], wo_ref[:SEG, :], preferred_element_type=F32)
          + jnp.dot(or_ref[...], wo_ref[SEG:, :], preferred_element_type=F32))
    h = (xn * _rsqrt_mean_sq(xn) * g_ref[...]).astype(BF16)
    for j in range(d_ff // FFN_TILE):
        cols = slice(j * FFN_TILE, (j + 1) * FFN_TILE)
        gate = jnp.dot(h, wg_ref[:, cols], preferred_element_type=F32)
        up = jnp.dot(h, wu_ref[:, cols], preferred_element_type=F32)
        ff_ref[:, cols] = (gate * _sigmoid(gate) * up).astype(BF16)
    o_ref[...] = xn + jnp.dot(ff_ref[...], wd_ref[...], preferred_element_type=F32)


def _out_ffn(x, od, orr, w_out, norm_g, w_gate, w_up, w_down, layer, tm):
    n, d = x.shape
    d_ff = w_gate.shape[-1]
    row = lambda i: (i, 0)
    lay = lambda i: (layer, 0, 0)
    return pl.pallas_call(
        functools.partial(_out_ffn_kernel, d_ff=d_ff),
        grid=(n // tm,),
        in_specs=[
            pl.BlockSpec((tm, d), row),
            pl.BlockSpec((tm, SEG), row),
            pl.BlockSpec((tm, SEG), row),
            _resident((None, d, d), lay),
            _resident((1, d), lambda i: (0, 0)),
            _resident((None, d, d_ff), lay),
            _resident((None, d, d_ff), lay),
            _resident((None, d_ff, d), lay),
        ],
        out_specs=pl.BlockSpec((tm, d), row),
        out_shape=jax.ShapeDtypeStruct((n, d), F32),
        scratch_shapes=[pltpu.VMEM((tm, d_ff), BF16)],
        compiler_params=pltpu.CompilerParams(
            dimension_semantics=("arbitrary",), vmem_limit_bytes=VMEM_LIMIT_BYTES),
        name="out_ffn",
    )(x, od, orr, w_out, norm_g, w_gate, w_up, w_down)


def _tile(n, want):
    return min(n, want)


def kernel(x, attn_norm_g, w_in, q_norm_g, k_norm_g, lambda_q1, lambda_k1, lambda_q2, lambda_k2, subln_g, lower_bounds, hgrn_norm_g, w_out, ffn_norm_g, w_gate, w_up, w_down):
    b, s, d = x.shape
    depth = w_in.shape[0]
    n = b * s
    assert d == 2 * SEG and w_in.shape[-1] == N_SEG * SEG and s % KEY_BLOCK == 0

    w_in_b = w_in.astype(BF16)
    w_out_b = w_out.astype(BF16)
    w_gate_b = w_gate.astype(BF16)
    w_up_b = w_up.astype(BF16)
    w_down_b = w_down.astype(BF16)

    group = np.arange(HEAD_LANES) // DIFF_HEAD_DIM
    gsum = jnp.asarray(group[:, None] == group[None, :], BF16)
    tril, masks = _hgrn_tables()
    tables = (jnp.asarray(tril, BF16), jnp.asarray(masks))
    lower_bounds = lower_bounds.astype(F32)

    tm = _tile(n, 512)
    tc = _tile(s, 512)

    xf = x.reshape(n, d)
    for l in range(depth):
        lam_init = 0.8 - 0.6 * math.exp(-0.3 * l)
        qg = jnp.tile(q_norm_g[l], 2)[None] * (DIFF_HEAD_DIM ** -0.5 * LOG2E)
        kg = jnp.tile(k_norm_g[l], 2)[None]
        score_bound = (1.02 * math.sqrt(DIFF_HEAD_DIM)
                       * jnp.max(jnp.abs(q_norm_g[l])) * jnp.max(jnp.abs(k_norm_g[l])))
        qt, k, vt, hq, hf, hi, hg = _in_proj(xf, attn_norm_g[l][None], w_in_b, l, qg, kg, gsum, tm)
        to3 = lambda t: t.reshape(b, s, SEG)
        od = _diff_attn(qt, k, vt, lambda_q1[l][None], lambda_k1[l][None],
                        lambda_q2[l][None], lambda_k2[l][None], subln_g[l][:, None], lam_init, b,
                        score_bound)
        orr = _hgrn(to3(hq), to3(hf), to3(hi), to3(hg), lower_bounds, hgrn_norm_g[l][None],
                    tables, l, tc)
        xf = _out_ffn(xf, od, orr.reshape(n, SEG), w_out_b, ffn_norm_g[l][None],
                      w_gate_b, w_up_b, w_down_b, l, tm)
    return xf.reshape(b, s, d)
```

```python
import functools
import math

import numpy as np
import jax
import jax.numpy as jnp
from jax import lax
from jax.experimental import pallas as pl
from jax.experimental.pallas import tpu as pltpu

F32 = jnp.float32
BF16 = jnp.bfloat16

NORM_EPS = 1e-6
CHUNK = 64
CHUNK_SHIFT = 6
HEAD_LANES = 128
N_HEADS = 4
SEG = N_HEADS * HEAD_LANES
DIFF_HEAD_DIM = 64
N_SEG = 7
N_LEVELS = CHUNK_SHIFT

VMEM_LIMIT_BYTES = 52 * 1024 * 1024
FFN_TILE = 256
KEY_BLOCK = 1024
ATTN_SUB = 256
ONES_ROWS = 16
SCORE_BOUND_MAX = 40.0
_NT = (((1,), (1,)), ((), ()))
SUBLANES = 8
LOG2E = math.log2(math.e)
BLOCK16 = 16
MASK_BLOCK16 = N_LEVELS + 1
BLOCK16_DECAY_MAX = 60.0


def _rsqrt_mean_sq(y):
    return lax.rsqrt(jnp.mean(y * y, axis=-1, keepdims=True) + NORM_EPS)


def _sigmoid(z):
    return 1.0 / (1.0 + jnp.exp(-z))


def _resident(block_shape, index_map):
    return pl.BlockSpec(block_shape, index_map, pipeline_mode=pl.Buffered(1))


def _in_proj_stages(x_ref, g_ref, w_ref, qg_ref, kg_ref, gsum_ref,
                    qt_ref, k_ref, vt_ref, hq_ref, hf_ref, hi_ref, hg_ref):
    x = x_ref[...]
    h = (x * _rsqrt_mean_sq(x) * g_ref[...]).astype(BF16)

    def seg(s):
        return jnp.dot(h, w_ref[:, s * SEG:(s + 1) * SEG], preferred_element_type=F32)

    def head_norm(y, gain_ref, c):
        yc = y[:, c * HEAD_LANES:(c + 1) * HEAD_LANES]
        ss = jnp.dot((yc * yc).astype(BF16), gsum_ref[...], preferred_element_type=F32)
        return yc * lax.rsqrt(ss * (1.0 / DIFF_HEAD_DIM) + NORM_EPS) * gain_ref[...]

    def q_stage():
        yq = seg(0)
        for c in range(N_HEADS):
            lanes = slice(c * HEAD_LANES, (c + 1) * HEAD_LANES)
            qt_ref[lanes, :] = head_norm(yq, qg_ref, c).T.astype(BF16)

    def k_stage():
        yk = seg(1)
        for c in range(N_HEADS):
            lanes = slice(c * HEAD_LANES, (c + 1) * HEAD_LANES)
            k_ref[:, lanes] = head_norm(yk, kg_ref, c).astype(BF16)

    def v_stage():
        vt_ref[...] = seg(2).T.astype(BF16)

    def hq_stage():
        qr = seg(3)
        hq_ref[...] = (qr * _sigmoid(qr)).astype(BF16)

    def hf_stage():
        hf_ref[...] = seg(4)

    def hi_stage():
        hi_ref[...] = seg(5).astype(BF16)

    def hg_stage():
        gr = seg(6)
        hg_ref[...] = (gr * _sigmoid(gr)).astype(BF16)

    return [q_stage, k_stage, v_stage, hq_stage, hf_stage, hi_stage, hg_stage]


def _in_proj_kernel(*refs):
    for stage in _in_proj_stages(*refs):
        stage()


def _in_proj(x, norm_g, w_in, layer, qg, kg, gsum, tm):
    n, d = x.shape
    row = lambda i: (i, 0)
    const2 = lambda i: (0, 0)
    out_bf = jax.ShapeDtypeStruct((n, SEG), BF16)
    out_f32 = jax.ShapeDtypeStruct((n, SEG), F32)
    seg_spec = pl.BlockSpec((tm, SEG), row)
    per_blk = KEY_BLOCK // tm
    return pl.pallas_call(
        _in_proj_kernel,
        grid=(n // tm,),
        in_specs=[
            pl.BlockSpec((tm, d), row),
            _resident((1, d), const2),
            _resident((None, d, N_SEG * SEG), lambda i: (layer, 0, 0)),
            _resident((1, HEAD_LANES), const2),
            _resident((1, HEAD_LANES), const2),
            _resident((HEAD_LANES, HEAD_LANES), const2),
        ],
        out_specs=[pl.BlockSpec((SEG, tm), lambda i: (0, i)), seg_spec,
                   pl.BlockSpec((None, SEG, tm), lambda i: (i // per_blk, 0, i % per_blk)),
                   seg_spec, seg_spec, seg_spec, seg_spec],
        out_shape=[jax.ShapeDtypeStruct((SEG, n), BF16), out_bf,
                   jax.ShapeDtypeStruct((n // KEY_BLOCK, SEG, KEY_BLOCK), BF16),
                   out_bf, out_f32, out_bf, out_bf],
        compiler_params=pltpu.CompilerParams(
            dimension_semantics=("arbitrary",), vmem_limit_bytes=VMEM_LIMIT_BYTES),
        name="in_proj",
    )(x, norm_g, w_in, qg, kg, gsum)


def _lambda_full(lq1_ref, lk1_ref, lq2_ref, lk2_ref, lam_init):
    return (jnp.exp(jnp.sum(lq1_ref[...] * lk1_ref[...], axis=-1, keepdims=True))
            - jnp.exp(jnp.sum(lq2_ref[...] * lk2_ref[...], axis=-1, keepdims=True))
            + lam_init)


def _split_maps_t(qt):
    row = lax.broadcasted_iota(jnp.int32, qt.shape, 0)
    zero = jnp.zeros_like(qt)
    return jnp.concatenate([jnp.where(row < DIFF_HEAD_DIM, qt, zero),
                            jnp.where(row >= DIFF_HEAD_DIM, qt, zero)], axis=1)


def _attn_bounded_kernel(qt_ref, k_ref, vt_ref, lq1_ref, lk1_ref, lq2_ref, lk2_ref, sgc_ref,
                         o_ref, acc_ref, *, lam_init):
    blk = pl.program_id(2)
    n_sub = KEY_BLOCK // ATTN_SUB
    ones = jnp.ones((ONES_ROWS, KEY_BLOCK), BF16)

    def q2t(s):
        return _split_maps_t(qt_ref[:, s * ATTN_SUB:(s + 1) * ATTN_SUB])

    def attend(s, k_rows, vt, diagonal):
        n_keys = k_rows.shape[0]
        st = jnp.dot(k_rows, q2t(s), preferred_element_type=F32)
        p = jnp.exp2(st)
        if diagonal:
            tail = p[n_keys - ATTN_SUB:]
            krow = lax.broadcasted_iota(jnp.int32, tail.shape, 0)
            qcol = lax.broadcasted_iota(jnp.int32, tail.shape, 1)
            visible = (krow >> CHUNK_SHIFT) <= ((qcol & (ATTN_SUB - 1)) >> CHUNK_SHIFT)
            tail = jnp.where(visible, tail, 0.0)
            p = tail if n_keys == ATTN_SUB else jnp.concatenate([p[:n_keys - ATTN_SUB], tail], axis=0)
        vte = jnp.concatenate([vt, ones[:, :n_keys]], axis=0)
        return jnp.dot(vte, p.astype(BF16), preferred_element_type=F32)

    start = pl.multiple_of(blk * KEY_BLOCK, KEY_BLOCK)
    for s in range(n_sub):
        n_keys = (s + 1) * ATTN_SUB
        acc_ref[s] = attend(s, k_ref[pl.ds(start, n_keys), :], vt_ref[blk, :, :n_keys], True)

    def body(j, carry):
        k_rows = k_ref[pl.ds(pl.multiple_of(j * KEY_BLOCK, KEY_BLOCK), KEY_BLOCK), :]
        vt = vt_ref[j]
        for s in range(n_sub):
            acc_ref[s] += attend(s, k_rows, vt, False)
        return carry

    lax.fori_loop(0, blk, body, 0)

    lam = _lambda_full(lq1_ref, lk1_ref, lq2_ref, lk2_ref, lam_init)
    for s in range(n_sub):
        a = acc_ref[s]
        o = a[:HEAD_LANES] * (1.0 / a[HEAD_LANES:HEAD_LANES + 1])
        od = o[:, :ATTN_SUB] - lam * o[:, ATTN_SUB:]
        ms = jnp.mean(od * od, axis=0, keepdims=True)
        od = od * lax.rsqrt(ms + NORM_EPS) * sgc_ref[...] * (1.0 - lam_init)
        o_ref[s * ATTN_SUB:(s + 1) * ATTN_SUB, :] = od.T.astype(BF16)


def _attn_online_kernel(qt_ref, k_ref, vt_ref, lq1_ref, lk1_ref, lq2_ref, lk2_ref, sgc_ref,
                        o_ref, v_ref, *, tq, lam_init):
    i = pl.program_id(2)

    @pl.when(i == 0)
    def _():
        for j in range(vt_ref.shape[0]):
            v_ref[j * KEY_BLOCK:(j + 1) * KEY_BLOCK, :] = vt_ref[j].astype(F32).T.astype(BF16)

    q = qt_ref[...].astype(F32).T.astype(BF16)
    lane = lax.broadcasted_iota(jnp.int32, q.shape, 1)
    zero = jnp.zeros_like(q)
    q2 = jnp.concatenate([jnp.where(lane < DIFF_HEAD_DIM, q, zero),
                          jnp.where(lane >= DIFF_HEAD_DIM, q, zero)], axis=0)

    def scores(kj):
        return lax.dot_general(q2, kj, _NT, preferred_element_type=F32)

    start = pl.multiple_of(i * tq, tq)
    s = scores(k_ref[pl.ds(start, tq), :])
    row = lax.broadcasted_iota(jnp.int32, s.shape, 0)
    col = lax.broadcasted_iota(jnp.int32, s.shape, 1)
    visible = (col >> CHUNK_SHIFT) <= ((row & (tq - 1)) >> CHUNK_SHIFT)
    s = jnp.where(visible, s, -jnp.inf)
    m = jnp.max(s, axis=-1, keepdims=True)
    p = jnp.exp2(s - m)
    l = jnp.sum(p, axis=-1, keepdims=True)
    acc = jnp.dot(p.astype(BF16), v_ref[pl.ds(start, tq), :], preferred_element_type=F32)

    def body(j, carry):
        m, l, acc = carry
        off = pl.multiple_of(j * tq, tq)
        s = scores(k_ref[pl.ds(off, tq), :])
        m_new = jnp.maximum(m, jnp.max(s, axis=-1, keepdims=True))
        alpha = jnp.exp2(m - m_new)
        p = jnp.exp2(s - m_new)
        l = alpha * l + jnp.sum(p, axis=-1, keepdims=True)
        acc = alpha * acc + jnp.dot(p.astype(BF16), v_ref[pl.ds(off, tq), :],
                                    preferred_element_type=F32)
        return m_new, l, acc

    m, l, acc = lax.fori_loop(0, i, body, (m, l, acc))
    o = acc / l
    lam = _lambda_full(lq1_ref, lk1_ref, lq2_ref, lk2_ref, lam_init)
    od = o[:tq] - lam * o[tq:]
    od = od * _rsqrt_mean_sq(od) * sgc_ref[...].T * (1.0 - lam_init)
    o_ref[...] = od.astype(BF16)


def _diff_attn(qt, k, vt, lq1, lk1, lq2, lk2, subln_g_col, lam_init, batch, score_bound):
    n = k.shape[0]
    s = n // batch
    blocks = s // KEY_BLOCK
    lam_spec = _resident((1, DIFF_HEAD_DIM), lambda bi, hi, qi: (0, 0))
    k_spec = pl.BlockSpec((s, HEAD_LANES), lambda bi, hi, qi: (bi, hi))
    vt_spec = pl.BlockSpec((blocks, HEAD_LANES, KEY_BLOCK), lambda bi, hi, qi: (bi, hi, 0))
    gain_spec = _resident((HEAD_LANES, 1), lambda bi, hi, qi: (0, 0))

    def call(body, tq, scratch, name):
        per_batch = s // tq
        return pl.pallas_call(
            body,
            grid=(batch, N_HEADS, per_batch),
            in_specs=[pl.BlockSpec((HEAD_LANES, tq), lambda bi, hi, qi: (hi, bi * per_batch + qi)),
                      k_spec, vt_spec, lam_spec, lam_spec, lam_spec, lam_spec, gain_spec],
            out_specs=pl.BlockSpec((tq, HEAD_LANES), lambda bi, hi, qi: (bi * per_batch + qi, hi)),
            out_shape=jax.ShapeDtypeStruct((n, SEG), BF16),
            scratch_shapes=scratch,
            compiler_params=pltpu.CompilerParams(
                dimension_semantics=("arbitrary", "arbitrary", "arbitrary"),
                vmem_limit_bytes=VMEM_LIMIT_BYTES),
            name=name,
        )

    bounded = call(
        functools.partial(_attn_bounded_kernel, lam_init=lam_init), KEY_BLOCK,
        [pltpu.VMEM((KEY_BLOCK // ATTN_SUB, HEAD_LANES + ONES_ROWS, 2 * ATTN_SUB), F32)],
        "diff_attn_bounded")
    online = call(
        functools.partial(_attn_online_kernel, tq=ATTN_SUB, lam_init=lam_init), ATTN_SUB,
        [pltpu.VMEM((s, HEAD_LANES), BF16)], "diff_attn_online")
    return lax.cond(score_bound <= SCORE_BOUND_MAX, bounded, online,
                    qt, k, vt, lq1, lk1, lq2, lk2, subln_g_col)


def _hgrn_tables():
    c = CHUNK
    masks = np.zeros((N_LEVELS + 1, c, c), np.float32)
    masks[0] = np.eye(c, dtype=np.float32)
    for lev in range(1, N_LEVELS + 1):
        size = 1 << lev
        half = size // 2
        for r in range(c):
            if r % size >= half:
                mid = r - r % size + half
                masks[lev, r, mid - half:mid] = 1.0
    t_idx, s_idx = np.arange(c)[:, None], np.arange(c)[None, :]
    block16 = ((t_idx // BLOCK16 == s_idx // BLOCK16) & (s_idx <= t_idx)).astype(np.float32)
    masks = np.concatenate([masks, block16[None]], axis=0)
    padded = np.zeros((masks.shape[0], 2, c, 2 * c), np.float32)
    padded[:, 0, :, :c] = masks
    padded[:, 1, :, c:] = masks
    return np.tril(np.ones((c, c), np.float32)), padded


def _level_ref_row(lev, row):
    size = 1 << lev
    return row - row % size + size // 2 - 1


def _hgrn_kernel(hq_ref, hf_ref, hi_ref, hg_ref, lb_ref, g_ref, tril_ref, masks_ref,
                 o_ref, state_ref, b_ref, k_ref, lf_ref, *, layer, n_chunks):
    @pl.when(pl.program_id(1) == 0)
    def _():
        state_ref[...] = jnp.zeros_like(state_ref)

    if layer > 0:
        lbp = lb_ref[...]
        ex = jnp.exp(lbp - jnp.max(lbp, axis=0, keepdims=True))
        gamma = ex / jnp.sum(ex, axis=0, keepdims=True)
        lb = jnp.sum(gamma[1:layer + 1], axis=0, keepdims=True)

    tn = (((0,), (0,)), ((), ()))
    groups = CHUNK // SUBLANES
    sub = lax.broadcasted_iota(jnp.int32, (SUBLANES, SEG), 0)
    up_small = {lev: (sub & (1 << (lev - 1))) != 0 for lev in (1, 2, 3)}
    sgn_small = {lev: jnp.where(up_small[lev], 1.0, -1.0) for lev in (2, 3)}
    upper_groups = {lev: [g for g in range(groups)
                          if (g * SUBLANES) % (1 << lev) >= (1 << lev) // 2]
                    for lev in range(4, N_LEVELS + 1)}

    def grp(a, g):
        return a[g * SUBLANES:(g + 1) * SUBLANES, :]

    def stack(parts):
        return jnp.concatenate(parts, axis=0).astype(BF16)

    def heads_to_rows(a):
        return jnp.concatenate(
            [a[:, h * HEAD_LANES:(h + 1) * HEAD_LANES] for h in range(N_HEADS)], axis=0)

    worst = None
    for c in range(n_chunks):
        rows = slice(c * CHUNK, (c + 1) * CHUNK)
        z = hf_ref[rows, :]
        e = jnp.exp(-jnp.abs(z))
        r = 1.0 / (1.0 + e)
        sig_neg = jnp.where(z >= 0, e * r, r)
        if layer == 0:
            lf2 = jnp.minimum(z * LOG2E, 0.0) - jnp.log2(1.0 + e)
            kk = sig_neg
        else:
            sig_pos = jnp.where(z >= 0, r, e * r)
            lf2 = jnp.log2(lb + (1.0 - lb) * sig_pos)
            kk = (1.0 - lb) * sig_neg
        hi = lf2.astype(BF16)
        lo = (lf2 - hi.astype(F32)).astype(BF16)
        b2 = (jnp.dot(tril_ref[...], hi, preferred_element_type=F32)
              + jnp.dot(tril_ref[...], lo, preferred_element_type=F32))
        b_ref[c] = b2
        k_ref[c] = kk
        lf_ref[c] = lf2
        for blk in range(CHUNK // BLOCK16):
            last = (blk + 1) * BLOCK16 - 1
            span = -b2[last:last + 1, :]
            if blk:
                span = span + b2[blk * BLOCK16 - 1:blk * BLOCK16, :]
            worst = span if worst is None else jnp.maximum(worst, span)
    moderate = jnp.max(worst) <= BLOCK16_DECAY_MAX

    def chunk(c, single_reference):
        rows = slice(c * CHUNK, (c + 1) * CHUNK)
        b2 = b_ref[c]
        kk = k_ref[c]
        qq = hq_ref[rows, :].astype(F32)

        def ref_rows(row):
            return jnp.broadcast_to(b_ref[c, row:row + 1, :], (SUBLANES, SEG))

        x = {lev: [] for lev in range(1, N_LEVELS + 1)}
        xq16, xk16, q_pre, k_suf = [], [], [], []
        for g in range(groups):
            b_g, q_g, k_g = grp(b2, g), grp(qq, g), grp(kk, g)
            row0 = g * SUBLANES
            if single_reference:
                start = row0 - row0 % BLOCK16
                d = b_g - ref_rows(start - 1) if start else b_g
                xq16.append(q_g * jnp.exp2(d))
                xk16.append(k_g * jnp.exp2(-d))
                first_level = 5
            else:
                x[1].append(jnp.where(up_small[1], q_g * jnp.exp2(grp(lf_ref[c], g)), k_g))
                ref2 = jnp.where(sub < 4, ref_rows(_level_ref_row(2, row0)),
                                 ref_rows(_level_ref_row(2, row0 + 4)))
                x[2].append(jnp.where(up_small[2], q_g, k_g)
                            * jnp.exp2((b_g - ref2) * sgn_small[2]))
                ref3 = ref_rows(_level_ref_row(3, row0))
                x[3].append(jnp.where(up_small[3], q_g, k_g)
                            * jnp.exp2((b_g - ref3) * sgn_small[3]))
                first_level = 4
            for lev in range(first_level, N_LEVELS + 1):
                ref = ref_rows(_level_ref_row(lev, row0))
                if row0 % (1 << lev) >= (1 << lev) // 2:
                    x[lev].append(q_g * jnp.exp2(b_g - ref))
                else:
                    x[lev].append(k_g * jnp.exp2(ref - b_g))
            q_pre.append(q_g * jnp.exp2(b_g))
            k_suf.append(k_g * jnp.exp2(ref_rows(CHUNK - 1) - b_g))

        all_groups = list(range(groups))
        if single_reference:
            terms = [(MASK_BLOCK16, all_groups, xq16, xk16)]
        else:
            terms = [(lev, all_groups, x[lev], x[lev]) for lev in (1, 2, 3)]
        for lev in range(first_level, N_LEVELS + 1):
            terms.append((lev, upper_groups[lev], [x[lev][g] for g in upper_groups[lev]], x[lev]))

        scores = [lax.dot_general(heads_to_rows(stack(q_rows)), heads_to_rows(stack(k_rows)), _NT,
                                  preferred_element_type=F32)
                  for _, _, q_rows, k_rows in terms]

        pair_blocks = []
        for pair in range(N_HEADS // 2):
            lanes = slice(pair * HEAD_LANES, (pair + 1) * HEAD_LANES)
            a_rows = []
            for h in (2 * pair, 2 * pair + 1):
                for g in range(groups):
                    gs = slice(g * SUBLANES, (g + 1) * SUBLANES)
                    a_g = None
                    if not single_reference:
                        hs = slice(h * HEAD_LANES, (h + 1) * HEAD_LANES)
                        a_g = (jnp.sum(qq[gs, hs] * kk[gs, hs], axis=-1, keepdims=True)
                               * masks_ref[0, h % 2, gs, :])
                    for (mask_id, q_groups, _, _), sc in zip(terms, scores):
                        if g not in q_groups:
                            continue
                        r0 = (h * len(q_groups) + q_groups.index(g)) * SUBLANES
                        part = sc[r0:r0 + SUBLANES, lanes] * masks_ref[mask_id, h % 2, gs, :]
                        a_g = part if a_g is None else a_g + part
                    a_rows.append(a_g)
            pair_blocks.append(jnp.concatenate(a_rows, axis=0))
        zero = jnp.zeros_like(pair_blocks[0])
        a_full = jnp.concatenate(
            [jnp.concatenate([pair_blocks[0], zero], axis=1),
             jnp.concatenate([zero, pair_blocks[1]], axis=1)], axis=0).astype(BF16)
        o_intra = jnp.dot(a_full, heads_to_rows(hi_ref[rows, :]), preferred_element_type=F32)

        q_pre = stack(q_pre)
        k_suf = stack(k_suf)
        decay_last = jnp.exp2(b2[CHUNK - 1:CHUNK, :])
        for h in range(N_HEADS):
            hs = slice(h * HEAD_LANES, (h + 1) * HEAD_LANES)
            st = state_ref[h]
            o = (o_intra[h * CHUNK:(h + 1) * CHUNK, :]
                 + lax.dot_general(q_pre[:, hs], st.astype(BF16), _NT,
                                   preferred_element_type=F32))
            state_ref[h] = (st * decay_last[:, hs]
                            + lax.dot_general(hi_ref[rows, hs], k_suf[:, hs], tn,
                                              preferred_element_type=F32))
            o = o * _rsqrt_mean_sq(o) * g_ref[...] * hg_ref[rows, hs].astype(F32)
            o_ref[rows, hs] = o.astype(BF16)

    @pl.when(moderate)
    def _():
        for c in range(n_chunks):
            chunk(c, True)

    @pl.when(jnp.logical_not(moderate))
    def _():
        for c in range(n_chunks):
            chunk(c, False)


def _hgrn(hq, hf, hi, hg, lower_bounds, norm_g, tables, layer, tc, batch):
    n = hq.shape[0]
    tril, masks = tables
    per_batch = n // batch // tc
    tile = pl.BlockSpec((tc, SEG), lambda bi, ci: (bi * per_batch + ci, 0))
    const2 = lambda bi, ci: (0, 0)
    return pl.pallas_call(
        functools.partial(_hgrn_kernel, layer=layer, n_chunks=tc // CHUNK),
        grid=(batch, per_batch),
        in_specs=[tile, tile, tile, tile,
                  _resident(lower_bounds.shape, const2),
                  _resident((1, HEAD_LANES), const2),
                  _resident(tril.shape, const2),
                  _resident(masks.shape, lambda bi, ci: (0, 0, 0, 0))],
        out_specs=tile,
        out_shape=jax.ShapeDtypeStruct((n, SEG), BF16),
        scratch_shapes=[pltpu.VMEM((N_HEADS, HEAD_LANES, HEAD_LANES), F32)]
        + [pltpu.VMEM((tc // CHUNK, CHUNK, SEG), F32)] * 3,
        compiler_params=pltpu.CompilerParams(
            dimension_semantics=("arbitrary", "arbitrary"),
            vmem_limit_bytes=VMEM_LIMIT_BYTES),
        name="hgrn2",
    )(hq, hf, hi, hg, lower_bounds, norm_g, tril, masks)


def _out_ffn_kernel(x_ref, od_ref, or_ref, wo_ref, g_ref, wg_ref, wu_ref, wd_ref,
                    o_ref, ff_ref, *, d_ff):
    xn = (x_ref[...]
          + jnp.dot(od_ref[...], wo_ref[:SEG, :], preferred_element_type=F32)
          + jnp.dot(or_ref[...], wo_ref[SEG:, :], preferred_element_type=F32))
    h = (xn * _rsqrt_mean_sq(xn) * g_ref[...]).astype(BF16)
    for j in range(d_ff // FFN_TILE):
        cols = slice(j * FFN_TILE, (j + 1) * FFN_TILE)
        gate = jnp.dot(h, wg_ref[:, cols], preferred_element_type=F32)
        up = jnp.dot(h, wu_ref[:, cols], preferred_element_type=F32)
        ff_ref[:, cols] = (gate * _sigmoid(gate) * up).astype(BF16)
    o_ref[...] = xn + jnp.dot(ff_ref[...], wd_ref[...], preferred_element_type=F32)


def _out_ffn(x, od, orr, w_out, norm_g, w_gate, w_up, w_down, layer, tm):
    n, d = x.shape
    d_ff = w_gate.shape[-1]
    row = lambda i: (i, 0)
    lay = lambda i: (layer, 0, 0)
    return pl.pallas_call(
        functools.partial(_out_ffn_kernel, d_ff=d_ff),
        grid=(n // tm,),
        in_specs=[
            pl.BlockSpec((tm, d), row),
            pl.BlockSpec((tm, SEG), row),
            pl.BlockSpec((tm, SEG), row),
            _resident((None, d, d), lay),
            _resident((1, d), lambda i: (0, 0)),
            _resident((None, d, d_ff), lay),
            _resident((None, d, d_ff), lay),
            _resident((None, d_ff, d), lay),
        ],
        out_specs=pl.BlockSpec((tm, d), row),
        out_shape=jax.ShapeDtypeStruct((n, d), F32),
        scratch_shapes=[pltpu.VMEM((tm, d_ff), BF16)],
        compiler_params=pltpu.CompilerParams(
            dimension_semantics=("arbitrary",), vmem_limit_bytes=VMEM_LIMIT_BYTES),
        name="out_ffn",
    )(x, od, orr, w_out, norm_g, w_gate, w_up, w_down)


def _tile(n, want):
    return min(n, want)


def kernel(x, attn_norm_g, w_in, q_norm_g, k_norm_g, lambda_q1, lambda_k1, lambda_q2, lambda_k2, subln_g, lower_bounds, hgrn_norm_g, w_out, ffn_norm_g, w_gate, w_up, w_down):
    b, s, d = x.shape
    depth = w_in.shape[0]
    n = b * s
    assert d == 2 * SEG and w_in.shape[-1] == N_SEG * SEG and s % KEY_BLOCK == 0

    w_in_b = w_in.astype(BF16)
    w_out_b = w_out.astype(BF16)
    w_gate_b = w_gate.astype(BF16)
    w_up_b = w_up.astype(BF16)
    w_down_b = w_down.astype(BF16)

    group = np.arange(HEAD_LANES) // DIFF_HEAD_DIM
    gsum = jnp.asarray(group[:, None] == group[None, :], BF16)
    tril, masks = _hgrn_tables()
    tables = (jnp.asarray(tril, BF16), jnp.asarray(masks))
    lower_bounds = lower_bounds.astype(F32)

    tm = _tile(s, 512)

    xf = x.reshape(n, d)
    for l in range(depth):
        lam_init = 0.8 - 0.6 * math.exp(-0.3 * l)
        qg = jnp.tile(q_norm_g[l], 2)[None] * (DIFF_HEAD_DIM ** -0.5 * LOG2E)
        kg = jnp.tile(k_norm_g[l], 2)[None]
        score_bound = (1.02 * math.sqrt(DIFF_HEAD_DIM)
                       * jnp.max(jnp.abs(q_norm_g[l])) * jnp.max(jnp.abs(k_norm_g[l])))
        qt, k, vt, hq, hf, hi, hg = _in_proj(xf, attn_norm_g[l][None], w_in_b, l, qg, kg, gsum, tm)
        orr = _hgrn(hq, hf, hi, hg, lower_bounds, hgrn_norm_g[l][None], tables, l, tm, b)
        od = _diff_attn(qt, k, vt, lambda_q1[l][None], lambda_k1[l][None],
                        lambda_q2[l][None], lambda_k2[l][None], subln_g[l][:, None], lam_init, b,
                        score_bound)
        xf = _out_ffn(xf, od, orr, w_out_b, ffn_norm_g[l][None],
                      w_gate_b, w_up_b, w_down_b, l, tm)
    return xf.reshape(b, s, d)
```

```python
import functools
import math

import numpy as np
import jax
import jax.numpy as jnp
from jax import lax
from jax.experimental import pallas as pl
from jax.experimental.pallas import tpu as pltpu

F32 = jnp.float32
BF16 = jnp.bfloat16

NORM_EPS = 1e-6
CHUNK = 64
CHUNK_SHIFT = 6
HEAD_LANES = 128
N_HEADS = 4
SEG = N_HEADS * HEAD_LANES
DIFF_HEAD_DIM = 64
N_SEG = 7
N_LEVELS = CHUNK_SHIFT

VMEM_LIMIT_BYTES = 52 * 1024 * 1024
FFN_TILE = 256
KEY_BLOCK = 1024
ATTN_SUB = 256
ONES_ROWS = 16
SCORE_BOUND_MAX = 40.0
_NT = (((1,), (1,)), ((), ()))
SUBLANES = 8
LOG2E = math.log2(math.e)
BLOCK16 = 16
MASK_BLOCK16 = N_LEVELS + 1
BLOCK16_DECAY_MAX = 60.0


def _rsqrt_mean_sq(y):
    return lax.rsqrt(jnp.mean(y * y, axis=-1, keepdims=True) + NORM_EPS)


def _sigmoid(z):
    return 1.0 / (1.0 + jnp.exp(-z))


def _resident(block_shape, index_map):
    return pl.BlockSpec(block_shape, index_map, pipeline_mode=pl.Buffered(1))


def _in_proj_stages(x_ref, g_ref, w_ref, qg_ref, kg_ref,
                    qt_ref, k_ref, vt_ref, hq_ref, hf_ref, hi_ref, hg_ref):
    x = x_ref[...]
    h = (x * _rsqrt_mean_sq(x) * g_ref[...]).astype(BF16)

    def seg(s):
        return jnp.dot(h, w_ref[:, s * SEG:(s + 1) * SEG], preferred_element_type=F32)

    def head_norm_t(y, gain_ref, c):
        t = y[:, c * HEAD_LANES:(c + 1) * HEAD_LANES].T
        halves = []
        for m in range(HEAD_LANES // DIFF_HEAD_DIM):
            tm_ = t[m * DIFF_HEAD_DIM:(m + 1) * DIFF_HEAD_DIM]
            ms = jnp.mean(tm_ * tm_, axis=0, keepdims=True)
            halves.append(tm_ * lax.rsqrt(ms + NORM_EPS))
        return jnp.concatenate(halves, axis=0) * gain_ref[...]

    def q_stage():
        yq = seg(0)
        for c in range(N_HEADS):
            lanes = slice(c * HEAD_LANES, (c + 1) * HEAD_LANES)
            qt_ref[lanes, :] = head_norm_t(yq, qg_ref, c).astype(BF16)

    def k_stage():
        yk = seg(1)
        for c in range(N_HEADS):
            lanes = slice(c * HEAD_LANES, (c + 1) * HEAD_LANES)
            k_ref[:, lanes] = head_norm_t(yk, kg_ref, c).T.astype(BF16)

    def v_stage():
        vt_ref[...] = seg(2).T.astype(BF16)

    def hq_stage():
        qr = seg(3)
        hq_ref[...] = (qr * _sigmoid(qr)).astype(BF16)

    def hf_stage():
        hf_ref[...] = seg(4)

    def hi_stage():
        hi_ref[...] = seg(5).astype(BF16)

    def hg_stage():
        gr = seg(6)
        hg_ref[...] = (gr * _sigmoid(gr)).astype(BF16)

    return [q_stage, k_stage, v_stage, hq_stage, hf_stage, hi_stage, hg_stage]


N_PROJ_IN = 5
N_PROJ_OUT = 7


def _in_proj_kernel(*refs, n_cast):
    cast_in = refs[N_PROJ_IN:N_PROJ_IN + n_cast]
    outs = refs[N_PROJ_IN + n_cast:N_PROJ_IN + n_cast + N_PROJ_OUT]
    cast_out = refs[N_PROJ_IN + n_cast + N_PROJ_OUT:]
    for stage in _in_proj_stages(*refs[:N_PROJ_IN], *outs):
        stage()
    for src, dst in zip(cast_in, cast_out):
        dst[...] = src[...].astype(BF16)


def _in_proj(x, norm_g, w_in, qg, kg, tm, next_weights, next_layer):
    n, d = x.shape
    steps = n // tm
    row = lambda i: (i, 0)
    const2 = lambda i: (0, 0)
    out_bf = jax.ShapeDtypeStruct((n, SEG), BF16)
    out_f32 = jax.ShapeDtypeStruct((n, SEG), F32)
    seg_spec = pl.BlockSpec((tm, SEG), row)
    per_blk = KEY_BLOCK // tm
    cast_in_specs, cast_out_specs, cast_out_shapes = [], [], []
    for w in next_weights:
        _, rows, cols = w.shape
        cast_in_specs.append(pl.BlockSpec((None, rows // steps, cols),
                                          lambda i: (next_layer, i, 0)))
        cast_out_specs.append(pl.BlockSpec((rows // steps, cols), row))
        cast_out_shapes.append(jax.ShapeDtypeStruct((rows, cols), BF16))
    outs = pl.pallas_call(
        functools.partial(_in_proj_kernel, n_cast=len(next_weights)),
        grid=(steps,),
        in_specs=[
            pl.BlockSpec((tm, d), row),
            _resident((1, d), const2),
            _resident((d, N_SEG * SEG), const2),
            _resident((HEAD_LANES, 1), const2),
            _resident((HEAD_LANES, 1), const2),
        ] + cast_in_specs,
        out_specs=[pl.BlockSpec((SEG, tm), lambda i: (0, i)), seg_spec,
                   pl.BlockSpec((None, SEG, tm), lambda i: (i // per_blk, 0, i % per_blk)),
                   seg_spec, seg_spec, seg_spec, seg_spec] + cast_out_specs,
        out_shape=[jax.ShapeDtypeStruct((SEG, n), BF16), out_bf,
                   jax.ShapeDtypeStruct((n // KEY_BLOCK, SEG, KEY_BLOCK), BF16),
                   out_bf, out_f32, out_bf, out_bf] + cast_out_shapes,
        compiler_params=pltpu.CompilerParams(
            dimension_semantics=("arbitrary",), vmem_limit_bytes=VMEM_LIMIT_BYTES),
        name="in_proj",
    )(x, norm_g, w_in, qg, kg, *next_weights)
    return outs[:N_PROJ_OUT], outs[N_PROJ_OUT:]


def _lambda_full(lq1_ref, lk1_ref, lq2_ref, lk2_ref, lam_init):
    return (jnp.exp(jnp.sum(lq1_ref[...] * lk1_ref[...], axis=-1, keepdims=True))
            - jnp.exp(jnp.sum(lq2_ref[...] * lk2_ref[...], axis=-1, keepdims=True))
            + lam_init)


def _split_maps_t(qt):
    row = lax.broadcasted_iota(jnp.int32, qt.shape, 0)
    zero = jnp.zeros_like(qt)
    return jnp.concatenate([jnp.where(row < DIFF_HEAD_DIM, qt, zero),
                            jnp.where(row >= DIFF_HEAD_DIM, qt, zero)], axis=1)


def _attn_bounded_kernel(qt_ref, k_ref, vt_ref, lq1_ref, lk1_ref, lq2_ref, lk2_ref, sgc_ref,
                         o_ref, acc_ref, *, lam_init):
    blk = pl.program_id(2)
    n_sub = KEY_BLOCK // ATTN_SUB
    ones = jnp.ones((ONES_ROWS, KEY_BLOCK), BF16)

    def q2t(s):
        return _split_maps_t(qt_ref[:, s * ATTN_SUB:(s + 1) * ATTN_SUB])

    def attend(s, k_rows, vt, diagonal):
        n_keys = k_rows.shape[0]
        st = jnp.dot(k_rows, q2t(s), preferred_element_type=F32)
        p = jnp.exp2(st)
        if diagonal:
            tail = p[n_keys - ATTN_SUB:]
            krow = lax.broadcasted_iota(jnp.int32, tail.shape, 0)
            qcol = lax.broadcasted_iota(jnp.int32, tail.shape, 1)
            visible = (krow >> CHUNK_SHIFT) <= ((qcol & (ATTN_SUB - 1)) >> CHUNK_SHIFT)
            tail = jnp.where(visible, tail, 0.0)
            p = tail if n_keys == ATTN_SUB else jnp.concatenate([p[:n_keys - ATTN_SUB], tail], axis=0)
        vte = jnp.concatenate([vt, ones[:, :n_keys]], axis=0)
        return jnp.dot(vte, p.astype(BF16), preferred_element_type=F32)

    start = pl.multiple_of(blk * KEY_BLOCK, KEY_BLOCK)
    for s in range(n_sub):
        n_keys = (s + 1) * ATTN_SUB
        acc_ref[s] = attend(s, k_ref[pl.ds(start, n_keys), :], vt_ref[blk, :, :n_keys], True)

    def body(j, carry):
        k_rows = k_ref[pl.ds(pl.multiple_of(j * KEY_BLOCK, KEY_BLOCK), KEY_BLOCK), :]
        vt = vt_ref[j]
        for s in range(n_sub):
            acc_ref[s] += attend(s, k_rows, vt, False)
        return carry

    lax.fori_loop(0, blk, body, 0)

    lam = _lambda_full(lq1_ref, lk1_ref, lq2_ref, lk2_ref, lam_init)
    for s in range(n_sub):
        a = acc_ref[s]
        o = a[:HEAD_LANES] * (1.0 / a[HEAD_LANES:HEAD_LANES + 1])
        od = o[:, :ATTN_SUB] - lam * o[:, ATTN_SUB:]
        ms = jnp.mean(od * od, axis=0, keepdims=True)
        od = od * lax.rsqrt(ms + NORM_EPS) * sgc_ref[...] * (1.0 - lam_init)
        o_ref[s * ATTN_SUB:(s + 1) * ATTN_SUB, :] = od.T.astype(BF16)


def _attn_online_kernel(qt_ref, k_ref, vt_ref, lq1_ref, lk1_ref, lq2_ref, lk2_ref, sgc_ref,
                        o_ref, v_ref, *, tq, lam_init):
    i = pl.program_id(2)

    @pl.when(i == 0)
    def _():
        for j in range(vt_ref.shape[0]):
            v_ref[j * KEY_BLOCK:(j + 1) * KEY_BLOCK, :] = vt_ref[j].astype(F32).T.astype(BF16)

    q = qt_ref[...].astype(F32).T.astype(BF16)
    lane = lax.broadcasted_iota(jnp.int32, q.shape, 1)
    zero = jnp.zeros_like(q)
    q2 = jnp.concatenate([jnp.where(lane < DIFF_HEAD_DIM, q, zero),
                          jnp.where(lane >= DIFF_HEAD_DIM, q, zero)], axis=0)

    def scores(kj):
        return lax.dot_general(q2, kj, _NT, preferred_element_type=F32)

    start = pl.multiple_of(i * tq, tq)
    s = scores(k_ref[pl.ds(start, tq), :])
    row = lax.broadcasted_iota(jnp.int32, s.shape, 0)
    col = lax.broadcasted_iota(jnp.int32, s.shape, 1)
    visible = (col >> CHUNK_SHIFT) <= ((row & (tq - 1)) >> CHUNK_SHIFT)
    s = jnp.where(visible, s, -jnp.inf)
    m = jnp.max(s, axis=-1, keepdims=True)
    p = jnp.exp2(s - m)
    l = jnp.sum(p, axis=-1, keepdims=True)
    acc = jnp.dot(p.astype(BF16), v_ref[pl.ds(start, tq), :], preferred_element_type=F32)

    def body(j, carry):
        m, l, acc = carry
        off = pl.multiple_of(j * tq, tq)
        s = scores(k_ref[pl.ds(off, tq), :])
        m_new = jnp.maximum(m, jnp.max(s, axis=-1, keepdims=True))
        alpha = jnp.exp2(m - m_new)
        p = jnp.exp2(s - m_new)
        l = alpha * l + jnp.sum(p, axis=-1, keepdims=True)
        acc = alpha * acc + jnp.dot(p.astype(BF16), v_ref[pl.ds(off, tq), :],
                                    preferred_element_type=F32)
        return m_new, l, acc

    m, l, acc = lax.fori_loop(0, i, body, (m, l, acc))
    o = acc / l
    lam = _lambda_full(lq1_ref, lk1_ref, lq2_ref, lk2_ref, lam_init)
    od = o[:tq] - lam * o[tq:]
    od = od * _rsqrt_mean_sq(od) * sgc_ref[...].T * (1.0 - lam_init)
    o_ref[...] = od.astype(BF16)


def _diff_attn(qt, k, vt, lq1, lk1, lq2, lk2, subln_g_col, lam_init, batch, score_bound):
    n = k.shape[0]
    s = n // batch
    blocks = s // KEY_BLOCK
    lam_spec = _resident((1, DIFF_HEAD_DIM), lambda bi, hi, qi: (0, 0))
    k_spec = pl.BlockSpec((s, HEAD_LANES), lambda bi, hi, qi: (bi, hi))
    vt_spec = pl.BlockSpec((blocks, HEAD_LANES, KEY_BLOCK), lambda bi, hi, qi: (bi, hi, 0))
    gain_spec = _resident((HEAD_LANES, 1), lambda bi, hi, qi: (0, 0))

    def call(body, tq, scratch, name):
        per_batch = s // tq
        return pl.pallas_call(
            body,
            grid=(batch, N_HEADS, per_batch),
            in_specs=[pl.BlockSpec((HEAD_LANES, tq), lambda bi, hi, qi: (hi, bi * per_batch + qi)),
                      k_spec, vt_spec, lam_spec, lam_spec, lam_spec, lam_spec, gain_spec],
            out_specs=pl.BlockSpec((tq, HEAD_LANES), lambda bi, hi, qi: (bi * per_batch + qi, hi)),
            out_shape=jax.ShapeDtypeStruct((n, SEG), BF16),
            scratch_shapes=scratch,
            compiler_params=pltpu.CompilerParams(
                dimension_semantics=("arbitrary", "arbitrary", "arbitrary"),
                vmem_limit_bytes=VMEM_LIMIT_BYTES),
            name=name,
        )

    bounded = call(
        functools.partial(_attn_bounded_kernel, lam_init=lam_init), KEY_BLOCK,
        [pltpu.VMEM((KEY_BLOCK // ATTN_SUB, HEAD_LANES + ONES_ROWS, 2 * ATTN_SUB), F32)],
        "diff_attn_bounded")
    online = call(
        functools.partial(_attn_online_kernel, tq=ATTN_SUB, lam_init=lam_init), ATTN_SUB,
        [pltpu.VMEM((s, HEAD_LANES), BF16)], "diff_attn_online")
    return lax.cond(score_bound <= SCORE_BOUND_MAX, bounded, online,
                    qt, k, vt, lq1, lk1, lq2, lk2, subln_g_col)


def _hgrn_tables():
    c = CHUNK
    masks = np.zeros((N_LEVELS + 1, c, c), np.float32)
    masks[0] = np.eye(c, dtype=np.float32)
    for lev in range(1, N_LEVELS + 1):
        size = 1 << lev
        half = size // 2
        for r in range(c):
            if r % size >= half:
                mid = r - r % size + half
                masks[lev, r, mid - half:mid] = 1.0
    t_idx, s_idx = np.arange(c)[:, None], np.arange(c)[None, :]
    block16 = ((t_idx // BLOCK16 == s_idx // BLOCK16) & (s_idx <= t_idx)).astype(np.float32)
    masks = np.concatenate([masks, block16[None]], axis=0)
    padded = np.zeros((masks.shape[0], 2, c, 2 * c), np.float32)
    padded[:, 0, :, :c] = masks
    padded[:, 1, :, c:] = masks
    return np.tril(np.ones((c, c), np.float32)), padded


def _level_ref_row(lev, row):
    size = 1 << lev
    return row - row % size + size // 2 - 1


def _hgrn_kernel(hq_ref, hf_ref, hi_ref, hg_ref, lb_ref, g_ref, tril_ref, masks_ref,
                 o_ref, state_ref, b_ref, k_ref, lf_ref, *, layer, n_chunks):
    @pl.when(pl.program_id(1) == 0)
    def _():
        state_ref[...] = jnp.zeros_like(state_ref)

    if layer > 0:
        lbp = lb_ref[...]
        ex = jnp.exp(lbp - jnp.max(lbp, axis=0, keepdims=True))
        gamma = ex / jnp.sum(ex, axis=0, keepdims=True)
        lb = jnp.sum(gamma[1:layer + 1], axis=0, keepdims=True)

    tn = (((0,), (0,)), ((), ()))
    groups = CHUNK // SUBLANES
    sub = lax.broadcasted_iota(jnp.int32, (SUBLANES, SEG), 0)
    up_small = {lev: (sub & (1 << (lev - 1))) != 0 for lev in (1, 2, 3)}
    sgn_small = {lev: jnp.where(up_small[lev], 1.0, -1.0) for lev in (2, 3)}
    upper_groups = {lev: [g for g in range(groups)
                          if (g * SUBLANES) % (1 << lev) >= (1 << lev) // 2]
                    for lev in range(4, N_LEVELS + 1)}

    def grp(a, g):
        return a[g * SUBLANES:(g + 1) * SUBLANES, :]

    def stack(parts):
        return jnp.concatenate(parts, axis=0).astype(BF16)

    def heads_to_rows(a):
        return jnp.concatenate(
            [a[:, h * HEAD_LANES:(h + 1) * HEAD_LANES] for h in range(N_HEADS)], axis=0)

    worst = None
    for c in range(n_chunks):
        rows = slice(c * CHUNK, (c + 1) * CHUNK)
        z = hf_ref[rows, :]
        e = jnp.exp(-jnp.abs(z))
        r = 1.0 / (1.0 + e)
        sig_neg = jnp.where(z >= 0, e * r, r)
        if layer == 0:
            lf2 = jnp.minimum(z * LOG2E, 0.0) - jnp.log2(1.0 + e)
            kk = sig_neg
        else:
            sig_pos = jnp.where(z >= 0, r, e * r)
            lf2 = jnp.log2(lb + (1.0 - lb) * sig_pos)
            kk = (1.0 - lb) * sig_neg
        hi = lf2.astype(BF16)
        lo = (lf2 - hi.astype(F32)).astype(BF16)
        b2 = (jnp.dot(tril_ref[...], hi, preferred_element_type=F32)
              + jnp.dot(tril_ref[...], lo, preferred_element_type=F32))
        b_ref[c] = b2
        k_ref[c] = kk
        lf_ref[c] = lf2
        for blk in range(CHUNK // BLOCK16):
            last = (blk + 1) * BLOCK16 - 1
            span = -b2[last:last + 1, :]
            if blk:
                span = span + b2[blk * BLOCK16 - 1:blk * BLOCK16, :]
            worst = span if worst is None else jnp.maximum(worst, span)
    moderate = jnp.max(worst) <= BLOCK16_DECAY_MAX

    def chunk(c, single_reference):
        rows = slice(c * CHUNK, (c + 1) * CHUNK)
        b2 = b_ref[c]
        kk = k_ref[c]
        qq = hq_ref[rows, :].astype(F32)

        def ref_rows(row):
            return jnp.broadcast_to(b_ref[c, row:row + 1, :], (SUBLANES, SEG))

        x = {lev: [] for lev in range(1, N_LEVELS + 1)}
        xq16, xk16, q_pre, k_suf = [], [], [], []
        for g in range(groups):
            b_g, q_g, k_g = grp(b2, g), grp(qq, g), grp(kk, g)
            row0 = g * SUBLANES
            if single_reference:
                start = row0 - row0 % BLOCK16
                d = b_g - ref_rows(start - 1) if start else b_g
                xq16.append(q_g * jnp.exp2(d))
                xk16.append(k_g * jnp.exp2(-d))
                first_level = 5
            else:
                x[1].append(jnp.where(up_small[1], q_g * jnp.exp2(grp(lf_ref[c], g)), k_g))
                ref2 = jnp.where(sub < 4, ref_rows(_level_ref_row(2, row0)),
                                 ref_rows(_level_ref_row(2, row0 + 4)))
                x[2].append(jnp.where(up_small[2], q_g, k_g)
                            * jnp.exp2((b_g - ref2) * sgn_small[2]))
                ref3 = ref_rows(_level_ref_row(3, row0))
                x[3].append(jnp.where(up_small[3], q_g, k_g)
                            * jnp.exp2((b_g - ref3) * sgn_small[3]))
                first_level = 4
            for lev in range(first_level, N_LEVELS + 1):
                ref = ref_rows(_level_ref_row(lev, row0))
                if row0 % (1 << lev) >= (1 << lev) // 2:
                    x[lev].append(q_g * jnp.exp2(b_g - ref))
                else:
                    x[lev].append(k_g * jnp.exp2(ref - b_g))
            q_pre.append(q_g * jnp.exp2(b_g))
            k_suf.append(k_g * jnp.exp2(ref_rows(CHUNK - 1) - b_g))

        all_groups = list(range(groups))
        if single_reference:
            terms = [(MASK_BLOCK16, all_groups, xq16, xk16)]
        else:
            terms = [(lev, all_groups, x[lev], x[lev]) for lev in (1, 2, 3)]
        for lev in range(first_level, N_LEVELS + 1):
            terms.append((lev, upper_groups[lev], [x[lev][g] for g in upper_groups[lev]], x[lev]))

        scores = [lax.dot_general(heads_to_rows(stack(q_rows)), heads_to_rows(stack(k_rows)), _NT,
                                  preferred_element_type=F32)
                  for _, _, q_rows, k_rows in terms]

        pair_blocks = []
        for pair in range(N_HEADS // 2):
            lanes = slice(pair * HEAD_LANES, (pair + 1) * HEAD_LANES)
            a_rows = []
            for h in (2 * pair, 2 * pair + 1):
                for g in range(groups):
                    gs = slice(g * SUBLANES, (g + 1) * SUBLANES)
                    a_g = None
                    if not single_reference:
                        hs = slice(h * HEAD_LANES, (h + 1) * HEAD_LANES)
                        a_g = (jnp.sum(qq[gs, hs] * kk[gs, hs], axis=-1, keepdims=True)
                               * masks_ref[0, h % 2, gs, :])
                    for (mask_id, q_groups, _, _), sc in zip(terms, scores):
                        if g not in q_groups:
                            continue
                        r0 = (h * len(q_groups) + q_groups.index(g)) * SUBLANES
                        part = sc[r0:r0 + SUBLANES, lanes] * masks_ref[mask_id, h % 2, gs, :]
                        a_g = part if a_g is None else a_g + part
                    a_rows.append(a_g)
            pair_blocks.append(jnp.concatenate(a_rows, axis=0))
        zero = jnp.zeros_like(pair_blocks[0])
        a_full = jnp.concatenate(
            [jnp.concatenate([pair_blocks[0], zero], axis=1),
             jnp.concatenate([zero, pair_blocks[1]], axis=1)], axis=0).astype(BF16)
        o_intra = jnp.dot(a_full, heads_to_rows(hi_ref[rows, :]), preferred_element_type=F32)

        q_pre = stack(q_pre)
        k_suf = stack(k_suf)
        decay_last = jnp.exp2(b2[CHUNK - 1:CHUNK, :])
        for h in range(N_HEADS):
            hs = slice(h * HEAD_LANES, (h + 1) * HEAD_LANES)
            st = state_ref[h]
            o = (o_intra[h * CHUNK:(h + 1) * CHUNK, :]
                 + lax.dot_general(q_pre[:, hs], st.astype(BF16), _NT,
                                   preferred_element_type=F32))
            state_ref[h] = (st * decay_last[:, hs]
                            + lax.dot_general(hi_ref[rows, hs], k_suf[:, hs], tn,
                                              preferred_element_type=F32))
            o = o * _rsqrt_mean_sq(o) * g_ref[...] * hg_ref[rows, hs].astype(F32)
            o_ref[rows, hs] = o.astype(BF16)

    @pl.when(moderate)
    def _():
        for c in range(n_chunks):
            chunk(c, True)

    @pl.when(jnp.logical_not(moderate))
    def _():
        for c in range(n_chunks):
            chunk(c, False)


def _hgrn(hq, hf, hi, hg, lower_bounds, norm_g, tables, layer, tc, batch):
    n = hq.shape[0]
    tril, masks = tables
    per_batch = n // batch // tc
    tile = pl.BlockSpec((tc, SEG), lambda bi, ci: (bi * per_batch + ci, 0))
    const2 = lambda bi, ci: (0, 0)
    return pl.pallas_call(
        functools.partial(_hgrn_kernel, layer=layer, n_chunks=tc // CHUNK),
        grid=(batch, per_batch),
        in_specs=[tile, tile, tile, tile,
                  _resident(lower_bounds.shape, const2),
                  _resident((1, HEAD_LANES), const2),
                  _resident(tril.shape, const2),
                  _resident(masks.shape, lambda bi, ci: (0, 0, 0, 0))],
        out_specs=tile,
        out_shape=jax.ShapeDtypeStruct((n, SEG), BF16),
        scratch_shapes=[pltpu.VMEM((N_HEADS, HEAD_LANES, HEAD_LANES), F32)]
        + [pltpu.VMEM((tc // CHUNK, CHUNK, SEG), F32)] * 3,
        compiler_params=pltpu.CompilerParams(
            dimension_semantics=("arbitrary", "arbitrary"),
            vmem_limit_bytes=VMEM_LIMIT_BYTES),
        name="hgrn2",
    )(hq, hf, hi, hg, lower_bounds, norm_g, tril, masks)


def _out_ffn_kernel(x_ref, od_ref, or_ref, wo_ref, g_ref, wg_ref, wu_ref, wd_ref,
                    o_ref, ff_ref, *, d_ff):
    xn = (x_ref[...]
          + jnp.dot(od_ref[...], wo_ref[:SEG, :], preferred_element_type=F32)
          + jnp.dot(or_ref[...], wo_ref[SEG:, :], preferred_element_type=F32))
    h = (xn * _rsqrt_mean_sq(xn) * g_ref[...]).astype(BF16)
    for j in range(d_ff // FFN_TILE):
        cols = slice(j * FFN_TILE, (j + 1) * FFN_TILE)
        gate = jnp.dot(h, wg_ref[:, cols], preferred_element_type=F32)
        up = jnp.dot(h, wu_ref[:, cols], preferred_element_type=F32)
        ff_ref[:, cols] = (gate * _sigmoid(gate) * up).astype(BF16)
    o_ref[...] = xn + jnp.dot(ff_ref[...], wd_ref[...], preferred_element_type=F32)


def _out_ffn(x, od, orr, w_out, norm_g, w_gate, w_up, w_down, tm):
    n, d = x.shape
    d_ff = w_gate.shape[-1]
    row = lambda i: (i, 0)
    const2 = lambda i: (0, 0)
    return pl.pallas_call(
        functools.partial(_out_ffn_kernel, d_ff=d_ff),
        grid=(n // tm,),
        in_specs=[
            pl.BlockSpec((tm, d), row),
            pl.BlockSpec((tm, SEG), row),
            pl.BlockSpec((tm, SEG), row),
            _resident((d, d), const2),
            _resident((1, d), const2),
            _resident((d, d_ff), const2),
            _resident((d, d_ff), const2),
            _resident((d_ff, d), const2),
        ],
        out_specs=pl.BlockSpec((tm, d), row),
        out_shape=jax.ShapeDtypeStruct((n, d), F32),
        scratch_shapes=[pltpu.VMEM((tm, d_ff), BF16)],
        compiler_params=pltpu.CompilerParams(
            dimension_semantics=("arbitrary",), vmem_limit_bytes=VMEM_LIMIT_BYTES),
        name="out_ffn",
    )(x, od, orr, w_out, norm_g, w_gate, w_up, w_down)


def _tile(n, want):
    return min(n, want)


def kernel(x, attn_norm_g, w_in, q_norm_g, k_norm_g, lambda_q1, lambda_k1, lambda_q2, lambda_k2, subln_g, lower_bounds, hgrn_norm_g, w_out, ffn_norm_g, w_gate, w_up, w_down):
    b, s, d = x.shape
    depth = w_in.shape[0]
    n = b * s
    assert d == 2 * SEG and w_in.shape[-1] == N_SEG * SEG and s % KEY_BLOCK == 0

    d_ff = w_gate.shape[-1]
    weights_f32 = (w_in, w_out, w_gate, w_up, w_down.reshape(depth, d, d_ff))
    weights_b = [w[0].astype(BF16) for w in weights_f32]

    tril, masks = _hgrn_tables()
    tables = (jnp.asarray(tril, BF16), jnp.asarray(masks))
    lower_bounds = lower_bounds.astype(F32)

    tm = _tile(s, 512)

    xf = x.reshape(n, d)
    for l in range(depth):
        lam_init = 0.8 - 0.6 * math.exp(-0.3 * l)
        qg = jnp.tile(q_norm_g[l], 2)[:, None] * (DIFF_HEAD_DIM ** -0.5 * LOG2E)
        kg = jnp.tile(k_norm_g[l], 2)[:, None]
        score_bound = (1.02 * math.sqrt(DIFF_HEAD_DIM)
                       * jnp.max(jnp.abs(q_norm_g[l])) * jnp.max(jnp.abs(k_norm_g[l])))
        w_in_b, w_out_b, w_gate_b, w_up_b, w_down_b = weights_b
        (qt, k, vt, hq, hf, hi, hg), weights_b = _in_proj(
            xf, attn_norm_g[l][None], w_in_b, qg, kg, tm,
            weights_f32 if l + 1 < depth else (), l + 1)
        orr = _hgrn(hq, hf, hi, hg, lower_bounds, hgrn_norm_g[l][None], tables, l, tm, b)
        od = _diff_attn(qt, k, vt, lambda_q1[l][None], lambda_k1[l][None],
                        lambda_q2[l][None], lambda_k2[l][None], subln_g[l][:, None], lam_init, b,
                        score_bound)
        xf = _out_ffn(xf, od, orr, w_out_b, ffn_norm_g[l][None],
                      w_gate_b, w_up_b, w_down_b.reshape(d_ff, d), tm)
    return xf.reshape(b, s, d)
```

```python
import functools
import math

import numpy as np
import jax
import jax.numpy as jnp
from jax import lax
from jax.experimental import pallas as pl
from jax.experimental.pallas import tpu as pltpu

F32 = jnp.float32
BF16 = jnp.bfloat16

NORM_EPS = 1e-6
CHUNK = 64
CHUNK_SHIFT = 6
HEAD_LANES = 128
N_HEADS = 4
SEG = N_HEADS * HEAD_LANES
DIFF_HEAD_DIM = 64
N_SEG = 7
N_LEVELS = CHUNK_SHIFT

VMEM_LIMIT_BYTES = 52 * 1024 * 1024
FFN_TILE = 256
KEY_BLOCK = 1024
ATTN_SUB = 256
ONES_ROWS = 16
SCORE_BOUND_MAX = 40.0
_NT = (((1,), (1,)), ((), ()))
SUBLANES = 8
BF16_SUBLANES = 16
LOG2E = math.log2(math.e)
BLOCK16 = 16
MASK_BLOCK16 = N_LEVELS + 1
BLOCK16_DECAY_MAX = 60.0


def _rsqrt_mean_sq(y):
    return lax.rsqrt(jnp.mean(y * y, axis=-1, keepdims=True) + NORM_EPS)


def _sigmoid(z):
    return 1.0 / (1.0 + jnp.exp(-z))


def _resident(block_shape, index_map):
    return pl.BlockSpec(block_shape, index_map, pipeline_mode=pl.Buffered(1))


def _in_proj_stages(x_ref, g_ref, w_ref, qg_ref, kg_ref,
                    qt_ref, k_ref, vt_ref, hq_ref, hf_ref, hi_ref, hg_ref):
    x = x_ref[...]
    h = (x * _rsqrt_mean_sq(x) * g_ref[...]).astype(BF16)

    def seg(s):
        return jnp.dot(h, w_ref[:, s * SEG:(s + 1) * SEG], preferred_element_type=F32)

    def head_norm_t(y, gain_ref, c):
        t = y[:, c * HEAD_LANES:(c + 1) * HEAD_LANES].T
        halves = []
        for m in range(HEAD_LANES // DIFF_HEAD_DIM):
            tm_ = t[m * DIFF_HEAD_DIM:(m + 1) * DIFF_HEAD_DIM]
            ms = jnp.mean(tm_ * tm_, axis=0, keepdims=True)
            halves.append(tm_ * lax.rsqrt(ms + NORM_EPS))
        return jnp.concatenate(halves, axis=0) * gain_ref[...]

    def q_stage():
        yq = seg(0)
        for c in range(N_HEADS):
            lanes = slice(c * HEAD_LANES, (c + 1) * HEAD_LANES)
            qt_ref[lanes, :] = head_norm_t(yq, qg_ref, c).astype(BF16)

    def k_stage():
        yk = seg(1)
        for c in range(N_HEADS):
            lanes = slice(c * HEAD_LANES, (c + 1) * HEAD_LANES)
            k_ref[:, lanes] = head_norm_t(yk, kg_ref, c).T.astype(BF16)

    def v_stage():
        vt_ref[...] = seg(2).T.astype(BF16)

    def hq_stage():
        qr = seg(3)
        hq_ref[...] = (qr * _sigmoid(qr)).astype(BF16)

    def hf_stage():
        hf_ref[...] = seg(4)

    def hi_stage():
        hi_ref[...] = seg(5).astype(BF16)

    def hg_stage():
        gr = seg(6)
        hg_ref[...] = (gr * _sigmoid(gr)).astype(BF16)

    return [q_stage, k_stage, v_stage, hq_stage, hf_stage, hi_stage, hg_stage]


N_PROJ_IN = 5
N_PROJ_OUT = 7


def _in_proj_kernel(*refs, n_cast):
    cast_in = refs[N_PROJ_IN:N_PROJ_IN + n_cast]
    outs = refs[N_PROJ_IN + n_cast:N_PROJ_IN + n_cast + N_PROJ_OUT]
    cast_out = refs[N_PROJ_IN + n_cast + N_PROJ_OUT:]
    for stage in _in_proj_stages(*refs[:N_PROJ_IN], *outs):
        stage()
    for src, dst in zip(cast_in, cast_out):
        dst[...] = src[...].astype(BF16)


def _in_proj(x, norm_g, w_in, qg, kg, tm, next_weights, next_layer):
    n, d = x.shape
    steps = n // tm
    row = lambda i: (i, 0)
    const2 = lambda i: (0, 0)
    out_bf = jax.ShapeDtypeStruct((n, SEG), BF16)
    out_f32 = jax.ShapeDtypeStruct((n, SEG), F32)
    seg_spec = pl.BlockSpec((tm, SEG), row)
    per_blk = KEY_BLOCK // tm
    cast_in_specs, cast_out_specs, cast_out_shapes = [], [], []
    for w in next_weights:
        _, rows, cols = w.shape
        slabs = max(k for k in range(1, steps + 1)
                    if rows % k == 0 and (rows // k) % BF16_SUBLANES == 0)
        slab = lambda i, last=slabs - 1: jnp.minimum(i, last)
        cast_in_specs.append(pl.BlockSpec((None, rows // slabs, cols),
                                          lambda i, slab=slab: (next_layer, slab(i), 0)))
        cast_out_specs.append(pl.BlockSpec((rows // slabs, cols),
                                           lambda i, slab=slab: (slab(i), 0)))
        cast_out_shapes.append(jax.ShapeDtypeStruct((rows, cols), BF16))
    outs = pl.pallas_call(
        functools.partial(_in_proj_kernel, n_cast=len(next_weights)),
        grid=(steps,),
        in_specs=[
            pl.BlockSpec((tm, d), row),
            _resident((1, d), const2),
            _resident((d, N_SEG * SEG), const2),
            _resident((HEAD_LANES, 1), const2),
            _resident((HEAD_LANES, 1), const2),
        ] + cast_in_specs,
        out_specs=[pl.BlockSpec((SEG, tm), lambda i: (0, i)), seg_spec,
                   pl.BlockSpec((None, SEG, tm), lambda i: (i // per_blk, 0, i % per_blk)),
                   seg_spec, seg_spec, seg_spec, seg_spec] + cast_out_specs,
        out_shape=[jax.ShapeDtypeStruct((SEG, n), BF16), out_bf,
                   jax.ShapeDtypeStruct((n // KEY_BLOCK, SEG, KEY_BLOCK), BF16),
                   out_bf, out_f32, out_bf, out_bf] + cast_out_shapes,
        compiler_params=pltpu.CompilerParams(
            dimension_semantics=("arbitrary",), vmem_limit_bytes=VMEM_LIMIT_BYTES),
        name="in_proj",
    )(x, norm_g, w_in, qg, kg, *next_weights)
    return outs[:N_PROJ_OUT], outs[N_PROJ_OUT:]


def _lambda_full(lq1_ref, lk1_ref, lq2_ref, lk2_ref, lam_init):
    return (jnp.exp(jnp.sum(lq1_ref[...] * lk1_ref[...], axis=-1, keepdims=True))
            - jnp.exp(jnp.sum(lq2_ref[...] * lk2_ref[...], axis=-1, keepdims=True))
            + lam_init)


def _split_maps_t(qt):
    row = lax.broadcasted_iota(jnp.int32, qt.shape, 0)
    zero = jnp.zeros_like(qt)
    return jnp.concatenate([jnp.where(row < DIFF_HEAD_DIM, qt, zero),
                            jnp.where(row >= DIFF_HEAD_DIM, qt, zero)], axis=1)


def _attn_bounded_kernel(qt_ref, k_ref, vt_ref, lq1_ref, lk1_ref, lq2_ref, lk2_ref, sgc_ref,
                         o_ref, acc_ref, *, lam_init):
    blk = pl.program_id(2)
    n_sub = KEY_BLOCK // ATTN_SUB
    ones = jnp.ones((ONES_ROWS, KEY_BLOCK), BF16)

    def q2t(s):
        return _split_maps_t(qt_ref[:, s * ATTN_SUB:(s + 1) * ATTN_SUB])

    def attend(s, k_rows, vt, diagonal):
        n_keys = k_rows.shape[0]
        st = jnp.dot(k_rows, q2t(s), preferred_element_type=F32)
        p = jnp.exp2(st)
        if diagonal:
            tail = p[n_keys - ATTN_SUB:]
            krow = lax.broadcasted_iota(jnp.int32, tail.shape, 0)
            qcol = lax.broadcasted_iota(jnp.int32, tail.shape, 1)
            visible = (krow >> CHUNK_SHIFT) <= ((qcol & (ATTN_SUB - 1)) >> CHUNK_SHIFT)
            tail = jnp.where(visible, tail, 0.0)
            p = tail if n_keys == ATTN_SUB else jnp.concatenate([p[:n_keys - ATTN_SUB], tail], axis=0)
        vte = jnp.concatenate([vt, ones[:, :n_keys]], axis=0)
        return jnp.dot(vte, p.astype(BF16), preferred_element_type=F32)

    start = pl.multiple_of(blk * KEY_BLOCK, KEY_BLOCK)
    for s in range(n_sub):
        n_keys = (s + 1) * ATTN_SUB
        acc_ref[s] = attend(s, k_ref[pl.ds(start, n_keys), :], vt_ref[blk, :, :n_keys], True)

    def body(j, carry):
        k_rows = k_ref[pl.ds(pl.multiple_of(j * KEY_BLOCK, KEY_BLOCK), KEY_BLOCK), :]
        vt = vt_ref[j]
        for s in range(n_sub):
            acc_ref[s] += attend(s, k_rows, vt, False)
        return carry

    lax.fori_loop(0, blk, body, 0)

    lam = _lambda_full(lq1_ref, lk1_ref, lq2_ref, lk2_ref, lam_init)
    for s in range(n_sub):
        a = acc_ref[s]
        o = a[:HEAD_LANES] * (1.0 / a[HEAD_LANES:HEAD_LANES + 1])
        od = o[:, :ATTN_SUB] - lam * o[:, ATTN_SUB:]
        ms = jnp.mean(od * od, axis=0, keepdims=True)
        od = od * lax.rsqrt(ms + NORM_EPS) * sgc_ref[...] * (1.0 - lam_init)
        o_ref[s * ATTN_SUB:(s + 1) * ATTN_SUB, :] = od.T.astype(BF16)


def _attn_online_kernel(qt_ref, k_ref, vt_ref, lq1_ref, lk1_ref, lq2_ref, lk2_ref, sgc_ref,
                        o_ref, v_ref, *, tq, lam_init):
    i = pl.program_id(2)

    @pl.when(i == 0)
    def _():
        for j in range(vt_ref.shape[0]):
            v_ref[j * KEY_BLOCK:(j + 1) * KEY_BLOCK, :] = vt_ref[j].astype(F32).T.astype(BF16)

    q = qt_ref[...].astype(F32).T.astype(BF16)
    lane = lax.broadcasted_iota(jnp.int32, q.shape, 1)
    zero = jnp.zeros_like(q)
    q2 = jnp.concatenate([jnp.where(lane < DIFF_HEAD_DIM, q, zero),
                          jnp.where(lane >= DIFF_HEAD_DIM, q, zero)], axis=0)

    def scores(kj):
        return lax.dot_general(q2, kj, _NT, preferred_element_type=F32)

    start = pl.multiple_of(i * tq, tq)
    s = scores(k_ref[pl.ds(start, tq), :])
    row = lax.broadcasted_iota(jnp.int32, s.shape, 0)
    col = lax.broadcasted_iota(jnp.int32, s.shape, 1)
    visible = (col >> CHUNK_SHIFT) <= ((row & (tq - 1)) >> CHUNK_SHIFT)
    s = jnp.where(visible, s, -jnp.inf)
    m = jnp.max(s, axis=-1, keepdims=True)
    p = jnp.exp2(s - m)
    l = jnp.sum(p, axis=-1, keepdims=True)
    acc = jnp.dot(p.astype(BF16), v_ref[pl.ds(start, tq), :], preferred_element_type=F32)

    def body(j, carry):
        m, l, acc = carry
        off = pl.multiple_of(j * tq, tq)
        s = scores(k_ref[pl.ds(off, tq), :])
        m_new = jnp.maximum(m, jnp.max(s, axis=-1, keepdims=True))
        alpha = jnp.exp2(m - m_new)
        p = jnp.exp2(s - m_new)
        l = alpha * l + jnp.sum(p, axis=-1, keepdims=True)
        acc = alpha * acc + jnp.dot(p.astype(BF16), v_ref[pl.ds(off, tq), :],
                                    preferred_element_type=F32)
        return m_new, l, acc

    m, l, acc = lax.fori_loop(0, i, body, (m, l, acc))
    o = acc / l
    lam = _lambda_full(lq1_ref, lk1_ref, lq2_ref, lk2_ref, lam_init)
    od = o[:tq] - lam * o[tq:]
    od = od * _rsqrt_mean_sq(od) * sgc_ref[...].T * (1.0 - lam_init)
    o_ref[...] = od.astype(BF16)


def _diff_attn(qt, k, vt, lq1, lk1, lq2, lk2, subln_g_col, lam_init, batch, score_bound):
    n = k.shape[0]
    s = n // batch
    blocks = s // KEY_BLOCK
    lam_spec = _resident((1, DIFF_HEAD_DIM), lambda bi, hi, qi: (0, 0))
    k_spec = pl.BlockSpec((s, HEAD_LANES), lambda bi, hi, qi: (bi, hi))
    vt_spec = pl.BlockSpec((blocks, HEAD_LANES, KEY_BLOCK), lambda bi, hi, qi: (bi, hi, 0))
    gain_spec = _resident((HEAD_LANES, 1), lambda bi, hi, qi: (0, 0))

    def call(body, tq, scratch, name):
        per_batch = s // tq
        return pl.pallas_call(
            body,
            grid=(batch, N_HEADS, per_batch),
            in_specs=[pl.BlockSpec((HEAD_LANES, tq), lambda bi, hi, qi: (hi, bi * per_batch + qi)),
                      k_spec, vt_spec, lam_spec, lam_spec, lam_spec, lam_spec, gain_spec],
            out_specs=pl.BlockSpec((tq, HEAD_LANES), lambda bi, hi, qi: (bi * per_batch + qi, hi)),
            out_shape=jax.ShapeDtypeStruct((n, SEG), BF16),
            scratch_shapes=scratch,
            compiler_params=pltpu.CompilerParams(
                dimension_semantics=("arbitrary", "arbitrary", "arbitrary"),
                vmem_limit_bytes=VMEM_LIMIT_BYTES),
            name=name,
        )

    bounded = call(
        functools.partial(_attn_bounded_kernel, lam_init=lam_init), KEY_BLOCK,
        [pltpu.VMEM((KEY_BLOCK // ATTN_SUB, HEAD_LANES + ONES_ROWS, 2 * ATTN_SUB), F32)],
        "diff_attn_bounded")
    online = call(
        functools.partial(_attn_online_kernel, tq=ATTN_SUB, lam_init=lam_init), ATTN_SUB,
        [pltpu.VMEM((s, HEAD_LANES), BF16)], "diff_attn_online")
    return lax.cond(score_bound <= SCORE_BOUND_MAX, bounded, online,
                    qt, k, vt, lq1, lk1, lq2, lk2, subln_g_col)


def _hgrn_tables():
    c = CHUNK
    masks = np.zeros((N_LEVELS + 1, c, c), np.float32)
    masks[0] = np.eye(c, dtype=np.float32)
    for lev in range(1, N_LEVELS + 1):
        size = 1 << lev
        half = size // 2
        for r in range(c):
            if r % size >= half:
                mid = r - r % size + half
                masks[lev, r, mid - half:mid] = 1.0
    t_idx, s_idx = np.arange(c)[:, None], np.arange(c)[None, :]
    block16 = ((t_idx // BLOCK16 == s_idx // BLOCK16) & (s_idx <= t_idx)).astype(np.float32)
    masks = np.concatenate([masks, block16[None]], axis=0)
    padded = np.zeros((masks.shape[0], 2, c, 2 * c), np.float32)
    padded[:, 0, :, :c] = masks
    padded[:, 1, :, c:] = masks
    return np.tril(np.ones((c, c), np.float32)), padded


def _level_ref_row(lev, row):
    size = 1 << lev
    return row - row % size + size // 2 - 1


def _hgrn_kernel(hq_ref, hf_ref, hi_ref, hg_ref, lb_ref, g_ref, tril_ref, masks_ref,
                 o_ref, state_ref, b_ref, k_ref, lf_ref, *, layer, n_chunks):
    @pl.when(pl.program_id(1) == 0)
    def _():
        state_ref[...] = jnp.zeros_like(state_ref)

    if layer > 0:
        lbp = lb_ref[...]
        ex = jnp.exp(lbp - jnp.max(lbp, axis=0, keepdims=True))
        gamma = ex / jnp.sum(ex, axis=0, keepdims=True)
        lb = jnp.sum(gamma[1:layer + 1], axis=0, keepdims=True)

    tn = (((0,), (0,)), ((), ()))
    groups = CHUNK // SUBLANES
    sub = lax.broadcasted_iota(jnp.int32, (SUBLANES, SEG), 0)
    up_small = {lev: (sub & (1 << (lev - 1))) != 0 for lev in (1, 2, 3)}
    sgn_small = {lev: jnp.where(up_small[lev], 1.0, -1.0) for lev in (2, 3)}
    upper_groups = {lev: [g for g in range(groups)
                          if (g * SUBLANES) % (1 << lev) >= (1 << lev) // 2]
                    for lev in range(4, N_LEVELS + 1)}

    def grp(a, g):
        return a[g * SUBLANES:(g + 1) * SUBLANES, :]

    def stack(parts):
        return jnp.concatenate(parts, axis=0).astype(BF16)

    def heads_to_rows(a):
        return jnp.concatenate(
            [a[:, h * HEAD_LANES:(h + 1) * HEAD_LANES] for h in range(N_HEADS)], axis=0)

    worst = None
    for c in range(n_chunks):
        rows = slice(c * CHUNK, (c + 1) * CHUNK)
        z = hf_ref[rows, :]
        e = jnp.exp(-jnp.abs(z))
        r = 1.0 / (1.0 + e)
        sig_neg = jnp.where(z >= 0, e * r, r)
        if layer == 0:
            lf2 = jnp.minimum(z * LOG2E, 0.0) - jnp.log2(1.0 + e)
            kk = sig_neg
        else:
            sig_pos = jnp.where(z >= 0, r, e * r)
            lf2 = jnp.log2(lb + (1.0 - lb) * sig_pos)
            kk = (1.0 - lb) * sig_neg
        hi = lf2.astype(BF16)
        lo = (lf2 - hi.astype(F32)).astype(BF16)
        b2 = (jnp.dot(tril_ref[...], hi, preferred_element_type=F32)
              + jnp.dot(tril_ref[...], lo, preferred_element_type=F32))
        b_ref[c] = b2
        k_ref[c] = kk
        lf_ref[c] = lf2
        for blk in range(CHUNK // BLOCK16):
            last = (blk + 1) * BLOCK16 - 1
            span = -b2[last:last + 1, :]
            if blk:
                span = span + b2[blk * BLOCK16 - 1:blk * BLOCK16, :]
            worst = span if worst is None else jnp.maximum(worst, span)
    moderate = jnp.max(worst) <= BLOCK16_DECAY_MAX

    def chunk(c, single_reference):
        rows = slice(c * CHUNK, (c + 1) * CHUNK)
        b2 = b_ref[c]
        kk = k_ref[c]
        qq = hq_ref[rows, :].astype(F32)

        def ref_rows(row):
            return jnp.broadcast_to(b_ref[c, row:row + 1, :], (SUBLANES, SEG))

        x = {lev: [] for lev in range(1, N_LEVELS + 1)}
        xq16, xk16, q_pre, k_suf = [], [], [], []
        for g in range(groups):
            b_g, q_g, k_g = grp(b2, g), grp(qq, g), grp(kk, g)
            row0 = g * SUBLANES
            if single_reference:
                start = row0 - row0 % BLOCK16
                d = b_g - ref_rows(start - 1) if start else b_g
                xq16.append(q_g * jnp.exp2(d))
                xk16.append(k_g * jnp.exp2(-d))
                first_level = 5
            else:
                x[1].append(jnp.where(up_small[1], q_g * jnp.exp2(grp(lf_ref[c], g)), k_g))
                ref2 = jnp.where(sub < 4, ref_rows(_level_ref_row(2, row0)),
                                 ref_rows(_level_ref_row(2, row0 + 4)))
                x[2].append(jnp.where(up_small[2], q_g, k_g)
                            * jnp.exp2((b_g - ref2) * sgn_small[2]))
                ref3 = ref_rows(_level_ref_row(3, row0))
                x[3].append(jnp.where(up_small[3], q_g, k_g)
                            * jnp.exp2((b_g - ref3) * sgn_small[3]))
                first_level = 4
            for lev in range(first_level, N_LEVELS + 1):
                ref = ref_rows(_level_ref_row(lev, row0))
                if row0 % (1 << lev) >= (1 << lev) // 2:
                    x[lev].append(q_g * jnp.exp2(b_g - ref))
                else:
                    x[lev].append(k_g * jnp.exp2(ref - b_g))
            q_pre.append(q_g * jnp.exp2(b_g))
            k_suf.append(k_g * jnp.exp2(ref_rows(CHUNK - 1) - b_g))

        all_groups = list(range(groups))
        if single_reference:
            terms = [(MASK_BLOCK16, all_groups, xq16, xk16)]
        else:
            terms = [(lev, all_groups, x[lev], x[lev]) for lev in (1, 2, 3)]
        for lev in range(first_level, N_LEVELS + 1):
            terms.append((lev, upper_groups[lev], [x[lev][g] for g in upper_groups[lev]], x[lev]))

        scores = [lax.dot_general(heads_to_rows(stack(q_rows)), heads_to_rows(stack(k_rows)), _NT,
                                  preferred_element_type=F32)
                  for _, _, q_rows, k_rows in terms]

        pair_blocks = []
        for pair in range(N_HEADS // 2):
            lanes = slice(pair * HEAD_LANES, (pair + 1) * HEAD_LANES)
            a_rows = []
            for h in (2 * pair, 2 * pair + 1):
                for g in range(groups):
                    gs = slice(g * SUBLANES, (g + 1) * SUBLANES)
                    a_g = None
                    if not single_reference:
                        hs = slice(h * HEAD_LANES, (h + 1) * HEAD_LANES)
                        a_g = (jnp.sum(qq[gs, hs] * kk[gs, hs], axis=-1, keepdims=True)
                               * masks_ref[0, h % 2, gs, :])
                    for (mask_id, q_groups, _, _), sc in zip(terms, scores):
                        if g not in q_groups:
                            continue
                        r0 = (h * len(q_groups) + q_groups.index(g)) * SUBLANES
                        part = sc[r0:r0 + SUBLANES, lanes] * masks_ref[mask_id, h % 2, gs, :]
                        a_g = part if a_g is None else a_g + part
                    a_rows.append(a_g)
            pair_blocks.append(jnp.concatenate(a_rows, axis=0))
        zero = jnp.zeros_like(pair_blocks[0])
        a_full = jnp.concatenate(
            [jnp.concatenate([pair_blocks[0], zero], axis=1),
             jnp.concatenate([zero, pair_blocks[1]], axis=1)], axis=0).astype(BF16)
        o_intra = jnp.dot(a_full, heads_to_rows(hi_ref[rows, :]), preferred_element_type=F32)

        q_pre = stack(q_pre)
        k_suf = stack(k_suf)
        decay_last = jnp.exp2(b2[CHUNK - 1:CHUNK, :])
        for h in range(N_HEADS):
            hs = slice(h * HEAD_LANES, (h + 1) * HEAD_LANES)
            st = state_ref[h]
            o = (o_intra[h * CHUNK:(h + 1) * CHUNK, :]
                 + lax.dot_general(q_pre[:, hs], st.astype(BF16), _NT,
                                   preferred_element_type=F32))
            state_ref[h] = (st * decay_last[:, hs]
                            + lax.dot_general(hi_ref[rows, hs], k_suf[:, hs], tn,
                                              preferred_element_type=F32))
            o = o * _rsqrt_mean_sq(o) * g_ref[...] * hg_ref[rows, hs].astype(F32)
            o_ref[rows, hs] = o.astype(BF16)

    @pl.when(moderate)
    def _():
        for c in range(n_chunks):
            chunk(c, True)

    @pl.when(jnp.logical_not(moderate))
    def _():
        for c in range(n_chunks):
            chunk(c, False)


def _hgrn(hq, hf, hi, hg, lower_bounds, norm_g, tables, layer, tc, batch):
    n = hq.shape[0]
    tril, masks = tables
    per_batch = n // batch // tc
    tile = pl.BlockSpec((tc, SEG), lambda bi, ci: (bi * per_batch + ci, 0))
    const2 = lambda bi, ci: (0, 0)
    return pl.pallas_call(
        functools.partial(_hgrn_kernel, layer=layer, n_chunks=tc // CHUNK),
        grid=(batch, per_batch),
        in_specs=[tile, tile, tile, tile,
                  _resident(lower_bounds.shape, const2),
                  _resident((1, HEAD_LANES), const2),
                  _resident(tril.shape, const2),
                  _resident(masks.shape, lambda bi, ci: (0, 0, 0, 0))],
        out_specs=tile,
        out_shape=jax.ShapeDtypeStruct((n, SEG), BF16),
        scratch_shapes=[pltpu.VMEM((N_HEADS, HEAD_LANES, HEAD_LANES), F32)]
        + [pltpu.VMEM((tc // CHUNK, CHUNK, SEG), F32)] * 3,
        compiler_params=pltpu.CompilerParams(
            dimension_semantics=("arbitrary", "arbitrary"),
            vmem_limit_bytes=VMEM_LIMIT_BYTES),
        name="hgrn2",
    )(hq, hf, hi, hg, lower_bounds, norm_g, tril, masks)


def _out_ffn_kernel(x_ref, od_ref, or_ref, wo_ref, g_ref, wg_ref, wu_ref, wd_ref,
                    o_ref, ff_ref, *, d_ff):
    xn = (x_ref[...]
          + jnp.dot(od_ref[...], wo_ref[:SEG, :], preferred_element_type=F32)
          + jnp.dot(or_ref[...], wo_ref[SEG:, :], preferred_element_type=F32))
    h = (xn * _rsqrt_mean_sq(xn) * g_ref[...]).astype(BF16)
    for j in range(d_ff // FFN_TILE):
        cols = slice(j * FFN_TILE, (j + 1) * FFN_TILE)
        gate = jnp.dot(h, wg_ref[:, cols], preferred_element_type=F32)
        up = jnp.dot(h, wu_ref[:, cols], preferred_element_type=F32)
        ff_ref[:, cols] = (gate * _sigmoid(gate) * up).astype(BF16)
    o_ref[...] = xn + jnp.dot(ff_ref[...], wd_ref[...], preferred_element_type=F32)


def _out_ffn(x, od, orr, w_out, norm_g, w_gate, w_up, w_down, tm):
    n, d = x.shape
    d_ff = w_gate.shape[-1]
    row = lambda i: (i, 0)
    const2 = lambda i: (0, 0)
    return pl.pallas_call(
        functools.partial(_out_ffn_kernel, d_ff=d_ff),
        grid=(n // tm,),
        in_specs=[
            pl.BlockSpec((tm, d), row),
            pl.BlockSpec((tm, SEG), row),
            pl.BlockSpec((tm, SEG), row),
            _resident((d, d), const2),
            _resident((1, d), const2),
            _resident((d, d_ff), const2),
            _resident((d, d_ff), const2),
            _resident((d_ff, d), const2),
        ],
        out_specs=pl.BlockSpec((tm, d), row),
        out_shape=jax.ShapeDtypeStruct((n, d), F32),
        scratch_shapes=[pltpu.VMEM((tm, d_ff), BF16)],
        compiler_params=pltpu.CompilerParams(
            dimension_semantics=("arbitrary",), vmem_limit_bytes=VMEM_LIMIT_BYTES),
        name="out_ffn",
    )(x, od, orr, w_out, norm_g, w_gate, w_up, w_down)


def _tile(n, want):
    return min(n, want)


def kernel(x, attn_norm_g, w_in, q_norm_g, k_norm_g, lambda_q1, lambda_k1, lambda_q2, lambda_k2, subln_g, lower_bounds, hgrn_norm_g, w_out, ffn_norm_g, w_gate, w_up, w_down):
    b, s, d = x.shape
    depth = w_in.shape[0]
    n = b * s
    assert d == 2 * SEG and w_in.shape[-1] == N_SEG * SEG and s % KEY_BLOCK == 0

    weights_f32 = (w_in, w_out, w_gate, w_up, w_down)
    weights_b = [w[0].astype(BF16) for w in weights_f32]

    tril, masks = _hgrn_tables()
    tables = (jnp.asarray(tril, BF16), jnp.asarray(masks))
    lower_bounds = lower_bounds.astype(F32)

    tm = _tile(s, 512)

    xf = x.reshape(n, d)
    for l in range(depth):
        lam_init = 0.8 - 0.6 * math.exp(-0.3 * l)
        qg = jnp.tile(q_norm_g[l], 2)[:, None] * (DIFF_HEAD_DIM ** -0.5 * LOG2E)
        kg = jnp.tile(k_norm_g[l], 2)[:, None]
        score_bound = (1.02 * math.sqrt(DIFF_HEAD_DIM)
                       * jnp.max(jnp.abs(q_norm_g[l])) * jnp.max(jnp.abs(k_norm_g[l])))
        w_in_b, w_out_b, w_gate_b, w_up_b, w_down_b = weights_b
        (qt, k, vt, hq, hf, hi, hg), weights_b = _in_proj(
            xf, attn_norm_g[l][None], w_in_b, qg, kg, tm,
            weights_f32 if l + 1 < depth else (), l + 1)
        orr = _hgrn(hq, hf, hi, hg, lower_bounds, hgrn_norm_g[l][None], tables, l, tm, b)
        od = _diff_attn(qt, k, vt, lambda_q1[l][None], lambda_k1[l][None],
                        lambda_q2[l][None], lambda_k2[l][None], subln_g[l][:, None], lam_init, b,
                        score_bound)
        xf = _out_ffn(xf, od, orr, w_out_b, ffn_norm_g[l][None],
                      w_gate_b, w_up_b, w_down_b, tm)
    return xf.reshape(b, s, d)
```

```python
import functools
import math

import numpy as np
import jax
import jax.numpy as jnp
from jax import lax
from jax.experimental import pallas as pl
from jax.experimental.pallas import tpu as pltpu

F32 = jnp.float32
BF16 = jnp.bfloat16

NORM_EPS = 1e-6
CHUNK = 64
CHUNK_SHIFT = 6
HEAD_LANES = 128
N_HEADS = 4
SEG = N_HEADS * HEAD_LANES
DIFF_HEAD_DIM = 64
N_SEG = 7
N_LEVELS = CHUNK_SHIFT

VMEM_LIMIT_BYTES = 52 * 1024 * 1024
FFN_TILE = 256
PROJ_ROWS = 1024
HGRN_ROWS = 1024
FFN_ROWS = 512
KEY_BLOCK = 1024
ATTN_SUB = 256
ONES_ROWS = 16
SCORE_BOUND_MAX = 40.0
_NT = (((1,), (1,)), ((), ()))
SUBLANES = 8
BF16_SUBLANES = 16
LOG2E = math.log2(math.e)
BLOCK16 = 16
MASK_BLOCK16 = N_LEVELS + 1
BLOCK16_DECAY_MAX = 60.0


def _rsqrt_mean_sq(y):
    return lax.rsqrt(jnp.mean(y * y, axis=-1, keepdims=True) + NORM_EPS)


def _sigmoid(z):
    return 1.0 / (1.0 + jnp.exp(-z))


def _resident(block_shape, index_map):
    return pl.BlockSpec(block_shape, index_map, pipeline_mode=pl.Buffered(1))


def _in_proj_stages(x_ref, g_ref, w_ref, qg_ref, kg_ref,
                    qt_ref, k_ref, vt_ref, hq_ref, hf_ref, hi_ref, hg_ref):
    x = x_ref[...]
    h = (x * _rsqrt_mean_sq(x) * g_ref[...]).astype(BF16)

    def seg(s):
        return jnp.dot(h, w_ref[:, s * SEG:(s + 1) * SEG], preferred_element_type=F32)

    def head_norm_t(y, gain_ref, c):
        t = y[:, c * HEAD_LANES:(c + 1) * HEAD_LANES].T
        halves = []
        for m in range(HEAD_LANES // DIFF_HEAD_DIM):
            tm_ = t[m * DIFF_HEAD_DIM:(m + 1) * DIFF_HEAD_DIM]
            ms = jnp.mean(tm_ * tm_, axis=0, keepdims=True)
            halves.append(tm_ * lax.rsqrt(ms + NORM_EPS))
        return jnp.concatenate(halves, axis=0) * gain_ref[...]

    def q_stage():
        yq = seg(0)
        for c in range(N_HEADS):
            lanes = slice(c * HEAD_LANES, (c + 1) * HEAD_LANES)
            qt_ref[lanes, :] = head_norm_t(yq, qg_ref, c).astype(BF16)

    def k_stage():
        yk = seg(1)
        for c in range(N_HEADS):
            lanes = slice(c * HEAD_LANES, (c + 1) * HEAD_LANES)
            k_ref[:, lanes] = head_norm_t(yk, kg_ref, c).T.astype(BF16)

    def v_stage():
        vt_ref[...] = seg(2).T.astype(BF16)

    def hq_stage():
        qr = seg(3)
        hq_ref[...] = (qr * _sigmoid(qr)).astype(BF16)

    def hf_stage():
        hf_ref[...] = seg(4)

    def hi_stage():
        hi_ref[...] = seg(5).astype(BF16)

    def hg_stage():
        gr = seg(6)
        hg_ref[...] = (gr * _sigmoid(gr)).astype(BF16)

    return [q_stage, k_stage, v_stage, hq_stage, hf_stage, hi_stage, hg_stage]


N_PROJ_IN = 5
N_PROJ_OUT = 7


def _in_proj_kernel(*refs, n_cast):
    cast_in = refs[N_PROJ_IN:N_PROJ_IN + n_cast]
    outs = refs[N_PROJ_IN + n_cast:N_PROJ_IN + n_cast + N_PROJ_OUT]
    cast_out = refs[N_PROJ_IN + n_cast + N_PROJ_OUT:]
    for stage in _in_proj_stages(*refs[:N_PROJ_IN], *outs):
        stage()
    for src, dst in zip(cast_in, cast_out):
        dst[...] = src[...].astype(BF16)


def _in_proj(x, norm_g, w_in, qg, kg, tm, next_weights, next_layer):
    n, d = x.shape
    steps = n // tm
    row = lambda i: (i, 0)
    const2 = lambda i: (0, 0)
    out_bf = jax.ShapeDtypeStruct((n, SEG), BF16)
    out_f32 = jax.ShapeDtypeStruct((n, SEG), F32)
    seg_spec = pl.BlockSpec((tm, SEG), row)
    per_blk = KEY_BLOCK // tm
    cast_in_specs, cast_out_specs, cast_out_shapes = [], [], []
    for w in next_weights:
        _, rows, cols = w.shape
        slabs = max(k for k in range(1, steps + 1)
                    if rows % k == 0 and (rows // k) % BF16_SUBLANES == 0)
        slab = lambda i, last=slabs - 1: jnp.minimum(i, last)
        cast_in_specs.append(pl.BlockSpec((None, rows // slabs, cols),
                                          lambda i, slab=slab: (next_layer, slab(i), 0)))
        cast_out_specs.append(pl.BlockSpec((rows // slabs, cols),
                                           lambda i, slab=slab: (slab(i), 0)))
        cast_out_shapes.append(jax.ShapeDtypeStruct((rows, cols), BF16))
    outs = pl.pallas_call(
        functools.partial(_in_proj_kernel, n_cast=len(next_weights)),
        grid=(steps,),
        in_specs=[
            pl.BlockSpec((tm, d), row),
            _resident((1, d), const2),
            _resident((d, N_SEG * SEG), const2),
            _resident((HEAD_LANES, 1), const2),
            _resident((HEAD_LANES, 1), const2),
        ] + cast_in_specs,
        out_specs=[pl.BlockSpec((SEG, tm), lambda i: (0, i)), seg_spec,
                   pl.BlockSpec((None, SEG, tm), lambda i: (i // per_blk, 0, i % per_blk)),
                   seg_spec, seg_spec, seg_spec, seg_spec] + cast_out_specs,
        out_shape=[jax.ShapeDtypeStruct((SEG, n), BF16), out_bf,
                   jax.ShapeDtypeStruct((n // KEY_BLOCK, SEG, KEY_BLOCK), BF16),
                   out_bf, out_f32, out_bf, out_bf] + cast_out_shapes,
        compiler_params=pltpu.CompilerParams(
            dimension_semantics=("arbitrary",), vmem_limit_bytes=VMEM_LIMIT_BYTES),
        name="in_proj",
    )(x, norm_g, w_in, qg, kg, *next_weights)
    return outs[:N_PROJ_OUT], outs[N_PROJ_OUT:]


def _lambda_full(lq1_ref, lk1_ref, lq2_ref, lk2_ref, lam_init):
    return (jnp.exp(jnp.sum(lq1_ref[...] * lk1_ref[...], axis=-1, keepdims=True))
            - jnp.exp(jnp.sum(lq2_ref[...] * lk2_ref[...], axis=-1, keepdims=True))
            + lam_init)


def _split_maps_t(qt):
    row = lax.broadcasted_iota(jnp.int32, qt.shape, 0)
    zero = jnp.zeros_like(qt)
    return jnp.concatenate([jnp.where(row < DIFF_HEAD_DIM, qt, zero),
                            jnp.where(row >= DIFF_HEAD_DIM, qt, zero)], axis=1)


def _attn_bounded_kernel(qt_ref, k_ref, vt_ref, lq1_ref, lk1_ref, lq2_ref, lk2_ref, sgc_ref,
                         o_ref, acc_ref, *, lam_init):
    blk = pl.program_id(2)
    n_sub = KEY_BLOCK // ATTN_SUB
    ones = jnp.ones((ONES_ROWS, KEY_BLOCK), BF16)

    def q2t(s):
        return _split_maps_t(qt_ref[:, s * ATTN_SUB:(s + 1) * ATTN_SUB])

    def attend(s, k_rows, vt, diagonal):
        n_keys = k_rows.shape[0]
        st = jnp.dot(k_rows, q2t(s), preferred_element_type=F32)
        p = jnp.exp2(st)
        if diagonal:
            tail = p[n_keys - ATTN_SUB:]
            krow = lax.broadcasted_iota(jnp.int32, tail.shape, 0)
            qcol = lax.broadcasted_iota(jnp.int32, tail.shape, 1)
            visible = (krow >> CHUNK_SHIFT) <= ((qcol & (ATTN_SUB - 1)) >> CHUNK_SHIFT)
            tail = jnp.where(visible, tail, 0.0)
            p = tail if n_keys == ATTN_SUB else jnp.concatenate([p[:n_keys - ATTN_SUB], tail], axis=0)
        vte = jnp.concatenate([vt, ones[:, :n_keys]], axis=0)
        return jnp.dot(vte, p.astype(BF16), preferred_element_type=F32)

    start = pl.multiple_of(blk * KEY_BLOCK, KEY_BLOCK)
    for s in range(n_sub):
        n_keys = (s + 1) * ATTN_SUB
        acc_ref[s] = attend(s, k_ref[pl.ds(start, n_keys), :], vt_ref[blk, :, :n_keys], True)

    def body(j, carry):
        k_rows = k_ref[pl.ds(pl.multiple_of(j * KEY_BLOCK, KEY_BLOCK), KEY_BLOCK), :]
        vt = vt_ref[j]
        for s in range(n_sub):
            acc_ref[s] += attend(s, k_rows, vt, False)
        return carry

    lax.fori_loop(0, blk, body, 0)

    lam = _lambda_full(lq1_ref, lk1_ref, lq2_ref, lk2_ref, lam_init)
    for s in range(n_sub):
        a = acc_ref[s]
        o = a[:HEAD_LANES] * (1.0 / a[HEAD_LANES:HEAD_LANES + 1])
        od = o[:, :ATTN_SUB] - lam * o[:, ATTN_SUB:]
        ms = jnp.mean(od * od, axis=0, keepdims=True)
        od = od * lax.rsqrt(ms + NORM_EPS) * sgc_ref[...] * (1.0 - lam_init)
        o_ref[s * ATTN_SUB:(s + 1) * ATTN_SUB, :] = od.T.astype(BF16)


def _attn_online_kernel(qt_ref, k_ref, vt_ref, lq1_ref, lk1_ref, lq2_ref, lk2_ref, sgc_ref,
                        o_ref, v_ref, *, tq, lam_init):
    i = pl.program_id(2)

    @pl.when(i == 0)
    def _():
        for j in range(vt_ref.shape[0]):
            v_ref[j * KEY_BLOCK:(j + 1) * KEY_BLOCK, :] = vt_ref[j].astype(F32).T.astype(BF16)

    q = qt_ref[...].astype(F32).T.astype(BF16)
    lane = lax.broadcasted_iota(jnp.int32, q.shape, 1)
    zero = jnp.zeros_like(q)
    q2 = jnp.concatenate([jnp.where(lane < DIFF_HEAD_DIM, q, zero),
                          jnp.where(lane >= DIFF_HEAD_DIM, q, zero)], axis=0)

    def scores(kj):
        return lax.dot_general(q2, kj, _NT, preferred_element_type=F32)

    start = pl.multiple_of(i * tq, tq)
    s = scores(k_ref[pl.ds(start, tq), :])
    row = lax.broadcasted_iota(jnp.int32, s.shape, 0)
    col = lax.broadcasted_iota(jnp.int32, s.shape, 1)
    visible = (col >> CHUNK_SHIFT) <= ((row & (tq - 1)) >> CHUNK_SHIFT)
    s = jnp.where(visible, s, -jnp.inf)
    m = jnp.max(s, axis=-1, keepdims=True)
    p = jnp.exp2(s - m)
    l = jnp.sum(p, axis=-1, keepdims=True)
    acc = jnp.dot(p.astype(BF16), v_ref[pl.ds(start, tq), :], preferred_element_type=F32)

    def body(j, carry):
        m, l, acc = carry
        off = pl.multiple_of(j * tq, tq)
        s = scores(k_ref[pl.ds(off, tq), :])
        m_new = jnp.maximum(m, jnp.max(s, axis=-1, keepdims=True))
        alpha = jnp.exp2(m - m_new)
        p = jnp.exp2(s - m_new)
        l = alpha * l + jnp.sum(p, axis=-1, keepdims=True)
        acc = alpha * acc + jnp.dot(p.astype(BF16), v_ref[pl.ds(off, tq), :],
                                    preferred_element_type=F32)
        return m_new, l, acc

    m, l, acc = lax.fori_loop(0, i, body, (m, l, acc))
    o = acc / l
    lam = _lambda_full(lq1_ref, lk1_ref, lq2_ref, lk2_ref, lam_init)
    od = o[:tq] - lam * o[tq:]
    od = od * _rsqrt_mean_sq(od) * sgc_ref[...].T * (1.0 - lam_init)
    o_ref[...] = od.astype(BF16)


def _diff_attn(qt, k, vt, lq1, lk1, lq2, lk2, subln_g_col, lam_init, batch, score_bound):
    n = k.shape[0]
    s = n // batch
    blocks = s // KEY_BLOCK
    lam_spec = _resident((1, DIFF_HEAD_DIM), lambda bi, hi, qi: (0, 0))
    k_spec = pl.BlockSpec((s, HEAD_LANES), lambda bi, hi, qi: (bi, hi))
    vt_spec = pl.BlockSpec((blocks, HEAD_LANES, KEY_BLOCK), lambda bi, hi, qi: (bi, hi, 0))
    gain_spec = _resident((HEAD_LANES, 1), lambda bi, hi, qi: (0, 0))

    def call(body, tq, scratch, name):
        per_batch = s // tq
        return pl.pallas_call(
            body,
            grid=(batch, N_HEADS, per_batch),
            in_specs=[pl.BlockSpec((HEAD_LANES, tq), lambda bi, hi, qi: (hi, bi * per_batch + qi)),
                      k_spec, vt_spec, lam_spec, lam_spec, lam_spec, lam_spec, gain_spec],
            out_specs=pl.BlockSpec((tq, HEAD_LANES), lambda bi, hi, qi: (bi * per_batch + qi, hi)),
            out_shape=jax.ShapeDtypeStruct((n, SEG), BF16),
            scratch_shapes=scratch,
            compiler_params=pltpu.CompilerParams(
                dimension_semantics=("arbitrary", "arbitrary", "arbitrary"),
                vmem_limit_bytes=VMEM_LIMIT_BYTES),
            name=name,
        )

    bounded = call(
        functools.partial(_attn_bounded_kernel, lam_init=lam_init), KEY_BLOCK,
        [pltpu.VMEM((KEY_BLOCK // ATTN_SUB, HEAD_LANES + ONES_ROWS, 2 * ATTN_SUB), F32)],
        "diff_attn_bounded")
    online = call(
        functools.partial(_attn_online_kernel, tq=ATTN_SUB, lam_init=lam_init), ATTN_SUB,
        [pltpu.VMEM((s, HEAD_LANES), BF16)], "diff_attn_online")
    return lax.cond(score_bound <= SCORE_BOUND_MAX, bounded, online,
                    qt, k, vt, lq1, lk1, lq2, lk2, subln_g_col)


def _hgrn_tables():
    c = CHUNK
    masks = np.zeros((N_LEVELS + 1, c, c), np.float32)
    masks[0] = np.eye(c, dtype=np.float32)
    for lev in range(1, N_LEVELS + 1):
        size = 1 << lev
        half = size // 2
        for r in range(c):
            if r % size >= half:
                mid = r - r % size + half
                masks[lev, r, mid - half:mid] = 1.0
    t_idx, s_idx = np.arange(c)[:, None], np.arange(c)[None, :]
    block16 = ((t_idx // BLOCK16 == s_idx // BLOCK16) & (s_idx <= t_idx)).astype(np.float32)
    masks = np.concatenate([masks, block16[None]], axis=0)
    padded = np.zeros((masks.shape[0], 2, c, 2 * c), np.float32)
    padded[:, 0, :, :c] = masks
    padded[:, 1, :, c:] = masks
    return np.tril(np.ones((c, c), np.float32)), padded


def _level_ref_row(lev, row):
    size = 1 << lev
    return row - row % size + size // 2 - 1


def _hgrn_kernel(hq_ref, hf_ref, hi_ref, hg_ref, lb_ref, g_ref, tril_ref, masks_ref,
                 o_ref, state_ref, b_ref, k_ref, lf_ref, *, layer, n_chunks):
    @pl.when(pl.program_id(1) == 0)
    def _():
        state_ref[...] = jnp.zeros_like(state_ref)

    if layer > 0:
        lbp = lb_ref[...]
        ex = jnp.exp(lbp - jnp.max(lbp, axis=0, keepdims=True))
        gamma = ex / jnp.sum(ex, axis=0, keepdims=True)
        lb = jnp.sum(gamma[1:layer + 1], axis=0, keepdims=True)

    tn = (((0,), (0,)), ((), ()))
    groups = CHUNK // SUBLANES
    sub = lax.broadcasted_iota(jnp.int32, (SUBLANES, SEG), 0)
    up_small = {lev: (sub & (1 << (lev - 1))) != 0 for lev in (1, 2, 3)}
    sgn_small = {lev: jnp.where(up_small[lev], 1.0, -1.0) for lev in (2, 3)}
    upper_groups = {lev: [g for g in range(groups)
                          if (g * SUBLANES) % (1 << lev) >= (1 << lev) // 2]
                    for lev in range(4, N_LEVELS + 1)}

    def grp(a, g):
        return a[g * SUBLANES:(g + 1) * SUBLANES, :]

    def stack(parts):
        return jnp.concatenate(parts, axis=0).astype(BF16)

    def heads_to_rows(a):
        return jnp.concatenate(
            [a[:, h * HEAD_LANES:(h + 1) * HEAD_LANES] for h in range(N_HEADS)], axis=0)

    worst = None
    for c in range(n_chunks):
        rows = slice(c * CHUNK, (c + 1) * CHUNK)
        z = hf_ref[rows, :]
        e = jnp.exp(-jnp.abs(z))
        r = 1.0 / (1.0 + e)
        sig_neg = jnp.where(z >= 0, e * r, r)
        if layer == 0:
            lf2 = jnp.minimum(z * LOG2E, 0.0) - jnp.log2(1.0 + e)
            kk = sig_neg
        else:
            sig_pos = jnp.where(z >= 0, r, e * r)
            lf2 = jnp.log2(lb + (1.0 - lb) * sig_pos)
            kk = (1.0 - lb) * sig_neg
        hi = lf2.astype(BF16)
        lo = (lf2 - hi.astype(F32)).astype(BF16)
        b2 = (jnp.dot(tril_ref[...], hi, preferred_element_type=F32)
              + jnp.dot(tril_ref[...], lo, preferred_element_type=F32))
        b_ref[c] = b2
        k_ref[c] = kk
        lf_ref[c] = lf2
        for blk in range(CHUNK // BLOCK16):
            last = (blk + 1) * BLOCK16 - 1
            span = -b2[last:last + 1, :]
            if blk:
                span = span + b2[blk * BLOCK16 - 1:blk * BLOCK16, :]
            worst = span if worst is None else jnp.maximum(worst, span)
    moderate = jnp.max(worst) <= BLOCK16_DECAY_MAX

    def chunk(c, single_reference):
        rows = slice(c * CHUNK, (c + 1) * CHUNK)
        b2 = b_ref[c]
        kk = k_ref[c]
        qq = hq_ref[rows, :].astype(F32)

        def ref_rows(row):
            return jnp.broadcast_to(b_ref[c, row:row + 1, :], (SUBLANES, SEG))

        x = {lev: [] for lev in range(1, N_LEVELS + 1)}
        xq16, xk16, q_pre, k_suf = [], [], [], []
        for g in range(groups):
            b_g, q_g, k_g = grp(b2, g), grp(qq, g), grp(kk, g)
            row0 = g * SUBLANES
            if single_reference:
                start = row0 - row0 % BLOCK16
                d = b_g - ref_rows(start - 1) if start else b_g
                xq16.append(q_g * jnp.exp2(d))
                xk16.append(k_g * jnp.exp2(-d))
                first_level = 5
            else:
                x[1].append(jnp.where(up_small[1], q_g * jnp.exp2(grp(lf_ref[c], g)), k_g))
                ref2 = jnp.where(sub < 4, ref_rows(_level_ref_row(2, row0)),
                                 ref_rows(_level_ref_row(2, row0 + 4)))
                x[2].append(jnp.where(up_small[2], q_g, k_g)
                            * jnp.exp2((b_g - ref2) * sgn_small[2]))
                ref3 = ref_rows(_level_ref_row(3, row0))
                x[3].append(jnp.where(up_small[3], q_g, k_g)
                            * jnp.exp2((b_g - ref3) * sgn_small[3]))
                first_level = 4
            for lev in range(first_level, N_LEVELS + 1):
                ref = ref_rows(_level_ref_row(lev, row0))
                if row0 % (1 << lev) >= (1 << lev) // 2:
                    x[lev].append(q_g * jnp.exp2(b_g - ref))
                else:
                    x[lev].append(k_g * jnp.exp2(ref - b_g))
            q_pre.append(q_g * jnp.exp2(b_g))
            k_suf.append(k_g * jnp.exp2(ref_rows(CHUNK - 1) - b_g))

        all_groups = list(range(groups))
        if single_reference:
            terms = [(MASK_BLOCK16, all_groups, xq16, xk16)]
        else:
            terms = [(lev, all_groups, x[lev], x[lev]) for lev in (1, 2, 3)]
        for lev in range(first_level, N_LEVELS + 1):
            terms.append((lev, upper_groups[lev], [x[lev][g] for g in upper_groups[lev]], x[lev]))

        scores = [lax.dot_general(heads_to_rows(stack(q_rows)), heads_to_rows(stack(k_rows)), _NT,
                                  preferred_element_type=F32)
                  for _, _, q_rows, k_rows in terms]

        pair_blocks = []
        for pair in range(N_HEADS // 2):
            lanes = slice(pair * HEAD_LANES, (pair + 1) * HEAD_LANES)
            a_rows = []
            for h in (2 * pair, 2 * pair + 1):
                for g in range(groups):
                    gs = slice(g * SUBLANES, (g + 1) * SUBLANES)
                    a_g = None
                    if not single_reference:
                        hs = slice(h * HEAD_LANES, (h + 1) * HEAD_LANES)
                        a_g = (jnp.sum(qq[gs, hs] * kk[gs, hs], axis=-1, keepdims=True)
                               * masks_ref[0, h % 2, gs, :])
                    for (mask_id, q_groups, _, _), sc in zip(terms, scores):
                        if g not in q_groups:
                            continue
                        r0 = (h * len(q_groups) + q_groups.index(g)) * SUBLANES
                        part = sc[r0:r0 + SUBLANES, lanes] * masks_ref[mask_id, h % 2, gs, :]
                        a_g = part if a_g is None else a_g + part
                    a_rows.append(a_g)
            pair_blocks.append(jnp.concatenate(a_rows, axis=0))
        zero = jnp.zeros_like(pair_blocks[0])
        a_full = jnp.concatenate(
            [jnp.concatenate([pair_blocks[0], zero], axis=1),
             jnp.concatenate([zero, pair_blocks[1]], axis=1)], axis=0).astype(BF16)
        o_intra = jnp.dot(a_full, heads_to_rows(hi_ref[rows, :]), preferred_element_type=F32)

        q_pre = stack(q_pre)
        k_suf = stack(k_suf)
        decay_last = jnp.exp2(b2[CHUNK - 1:CHUNK, :])
        for h in range(N_HEADS):
            hs = slice(h * HEAD_LANES, (h + 1) * HEAD_LANES)
            st = state_ref[h]
            o = (o_intra[h * CHUNK:(h + 1) * CHUNK, :]
                 + lax.dot_general(q_pre[:, hs], st.astype(BF16), _NT,
                                   preferred_element_type=F32))
            state_ref[h] = (st * decay_last[:, hs]
                            + lax.dot_general(hi_ref[rows, hs], k_suf[:, hs], tn,
                                              preferred_element_type=F32))
            o = o * _rsqrt_mean_sq(o) * g_ref[...] * hg_ref[rows, hs].astype(F32)
            o_ref[rows, hs] = o.astype(BF16)

    @pl.when(moderate)
    def _():
        for c in range(n_chunks):
            chunk(c, True)

    @pl.when(jnp.logical_not(moderate))
    def _():
        for c in range(n_chunks):
            chunk(c, False)


def _hgrn(hq, hf, hi, hg, lower_bounds, norm_g, tables, layer, tc, batch):
    n = hq.shape[0]
    tril, masks = tables
    per_batch = n // batch // tc
    tile = pl.BlockSpec((tc, SEG), lambda bi, ci: (bi * per_batch + ci, 0))
    const2 = lambda bi, ci: (0, 0)
    return pl.pallas_call(
        functools.partial(_hgrn_kernel, layer=layer, n_chunks=tc // CHUNK),
        grid=(batch, per_batch),
        in_specs=[tile, tile, tile, tile,
                  _resident(lower_bounds.shape, const2),
                  _resident((1, HEAD_LANES), const2),
                  _resident(tril.shape, const2),
                  _resident(masks.shape, lambda bi, ci: (0, 0, 0, 0))],
        out_specs=tile,
        out_shape=jax.ShapeDtypeStruct((n, SEG), BF16),
        scratch_shapes=[pltpu.VMEM((N_HEADS, HEAD_LANES, HEAD_LANES), F32)]
        + [pltpu.VMEM((tc // CHUNK, CHUNK, SEG), F32)] * 3,
        compiler_params=pltpu.CompilerParams(
            dimension_semantics=("arbitrary", "arbitrary"),
            vmem_limit_bytes=VMEM_LIMIT_BYTES),
        name="hgrn2",
    )(hq, hf, hi, hg, lower_bounds, norm_g, tril, masks)


def _out_ffn_kernel(x_ref, od_ref, or_ref, wo_ref, g_ref, wg_ref, wu_ref, wd_ref,
                    o_ref, ff_ref, *, d_ff):
    xn = (x_ref[...]
          + jnp.dot(od_ref[...], wo_ref[:SEG, :], preferred_element_type=F32)
          + jnp.dot(or_ref[...], wo_ref[SEG:, :], preferred_element_type=F32))
    h = (xn * _rsqrt_mean_sq(xn) * g_ref[...]).astype(BF16)
    for j in range(d_ff // FFN_TILE):
        cols = slice(j * FFN_TILE, (j + 1) * FFN_TILE)
        gate = jnp.dot(h, wg_ref[:, cols], preferred_element_type=F32)
        up = jnp.dot(h, wu_ref[:, cols], preferred_element_type=F32)
        ff_ref[:, cols] = (gate * _sigmoid(gate) * up).astype(BF16)
    o_ref[...] = xn + jnp.dot(ff_ref[...], wd_ref[...], preferred_element_type=F32)


def _out_ffn(x, od, orr, w_out, norm_g, w_gate, w_up, w_down, tm):
    n, d = x.shape
    d_ff = w_gate.shape[-1]
    row = lambda i: (i, 0)
    const2 = lambda i: (0, 0)
    return pl.pallas_call(
        functools.partial(_out_ffn_kernel, d_ff=d_ff),
        grid=(n // tm,),
        in_specs=[
            pl.BlockSpec((tm, d), row),
            pl.BlockSpec((tm, SEG), row),
            pl.BlockSpec((tm, SEG), row),
            _resident((d, d), const2),
            _resident((1, d), const2),
            _resident((d, d_ff), const2),
            _resident((d, d_ff), const2),
            _resident((d_ff, d), const2),
        ],
        out_specs=pl.BlockSpec((tm, d), row),
        out_shape=jax.ShapeDtypeStruct((n, d), F32),
        scratch_shapes=[pltpu.VMEM((tm, d_ff), BF16)],
        compiler_params=pltpu.CompilerParams(
            dimension_semantics=("arbitrary",), vmem_limit_bytes=VMEM_LIMIT_BYTES),
        name="out_ffn",
    )(x, od, orr, w_out, norm_g, w_gate, w_up, w_down)


def _tile(n, want):
    return min(n, want)


def kernel(x, attn_norm_g, w_in, q_norm_g, k_norm_g, lambda_q1, lambda_k1, lambda_q2, lambda_k2, subln_g, lower_bounds, hgrn_norm_g, w_out, ffn_norm_g, w_gate, w_up, w_down):
    b, s, d = x.shape
    depth = w_in.shape[0]
    n = b * s
    assert d == 2 * SEG and w_in.shape[-1] == N_SEG * SEG and s % KEY_BLOCK == 0

    weights_f32 = (w_in, w_out, w_gate, w_up, w_down)
    weights_b = [w[0].astype(BF16) for w in weights_f32]

    tril, masks = _hgrn_tables()
    tables = (jnp.asarray(tril, BF16), jnp.asarray(masks))
    lower_bounds = lower_bounds.astype(F32)

    tm_proj = _tile(s, PROJ_ROWS)
    tm_hgrn = _tile(s, HGRN_ROWS)
    tm_ffn = _tile(s, FFN_ROWS)

    xf = x.reshape(n, d)
    for l in range(depth):
        lam_init = 0.8 - 0.6 * math.exp(-0.3 * l)
        qg = jnp.tile(q_norm_g[l], 2)[:, None] * (DIFF_HEAD_DIM ** -0.5 * LOG2E)
        kg = jnp.tile(k_norm_g[l], 2)[:, None]
        score_bound = (1.02 * math.sqrt(DIFF_HEAD_DIM)
                       * jnp.max(jnp.abs(q_norm_g[l])) * jnp.max(jnp.abs(k_norm_g[l])))
        w_in_b, w_out_b, w_gate_b, w_up_b, w_down_b = weights_b
        (qt, k, vt, hq, hf, hi, hg), weights_b = _in_proj(
            xf, attn_norm_g[l][None], w_in_b, qg, kg, tm_proj,
            weights_f32 if l + 1 < depth else (), l + 1)
        orr = _hgrn(hq, hf, hi, hg, lower_bounds, hgrn_norm_g[l][None], tables, l, tm_hgrn, b)
        od = _diff_attn(qt, k, vt, lambda_q1[l][None], lambda_k1[l][None],
                        lambda_q2[l][None], lambda_k2[l][None], subln_g[l][:, None], lam_init, b,
                        score_bound)
        xf = _out_ffn(xf, od, orr, w_out_b, ffn_norm_g[l][None],
                      w_gate_b, w_up_b, w_down_b, tm_ffn)
    return xf.reshape(b, s, d)
```

```python
import functools
import math

import numpy as np
import jax
import jax.numpy as jnp
from jax import lax
from jax.experimental import pallas as pl
from jax.experimental.pallas import tpu as pltpu

F32 = jnp.float32
BF16 = jnp.bfloat16

NORM_EPS = 1e-6
CHUNK = 64
CHUNK_SHIFT = 6
HEAD_LANES = 128
N_HEADS = 4
SEG = N_HEADS * HEAD_LANES
DIFF_HEAD_DIM = 64
N_SEG = 7
N_LEVELS = CHUNK_SHIFT

VMEM_LIMIT_BYTES = 52 * 1024 * 1024
FFN_TILE = 256
PROJ_ROWS = 1024
HGRN_ROWS = 1024
FFN_ROWS = 512
KEY_BLOCK = 4096
ATTN_SUB = 256
ONES_ROWS = 16
SCORE_BOUND_MAX = 40.0
_NT = (((1,), (1,)), ((), ()))
SUBLANES = 8
BF16_SUBLANES = 16
LOG2E = math.log2(math.e)
BLOCK16 = 16
MASK_BLOCK16 = N_LEVELS + 1
BLOCK16_DECAY_MAX = 60.0


def _rsqrt_mean_sq(y):
    return lax.rsqrt(jnp.mean(y * y, axis=-1, keepdims=True) + NORM_EPS)


def _sigmoid(z):
    return 1.0 / (1.0 + jnp.exp(-z))


def _resident(block_shape, index_map):
    return pl.BlockSpec(block_shape, index_map, pipeline_mode=pl.Buffered(1))


def _in_proj_stages(x_ref, g_ref, w_ref, qg_ref, kg_ref,
                    qt_ref, k_ref, vt_ref, hq_ref, hf_ref, hi_ref, hg_ref):
    x = x_ref[...]
    h = (x * _rsqrt_mean_sq(x) * g_ref[...]).astype(BF16)

    def seg(s):
        return jnp.dot(h, w_ref[:, s * SEG:(s + 1) * SEG], preferred_element_type=F32)

    def head_norm_t(y, gain_ref, c):
        t = y[:, c * HEAD_LANES:(c + 1) * HEAD_LANES].T
        halves = []
        for m in range(HEAD_LANES // DIFF_HEAD_DIM):
            tm_ = t[m * DIFF_HEAD_DIM:(m + 1) * DIFF_HEAD_DIM]
            ms = jnp.mean(tm_ * tm_, axis=0, keepdims=True)
            halves.append(tm_ * lax.rsqrt(ms + NORM_EPS))
        return jnp.concatenate(halves, axis=0) * gain_ref[...]

    def q_stage():
        yq = seg(0)
        for c in range(N_HEADS):
            lanes = slice(c * HEAD_LANES, (c + 1) * HEAD_LANES)
            qt_ref[lanes, :] = head_norm_t(yq, qg_ref, c).astype(BF16)

    def k_stage():
        yk = seg(1)
        for c in range(N_HEADS):
            lanes = slice(c * HEAD_LANES, (c + 1) * HEAD_LANES)
            k_ref[:, lanes] = head_norm_t(yk, kg_ref, c).T.astype(BF16)

    def v_stage():
        vt_ref[...] = seg(2).T.astype(BF16)

    def hq_stage():
        qr = seg(3)
        hq_ref[...] = (qr * _sigmoid(qr)).astype(BF16)

    def hf_stage():
        hf_ref[...] = seg(4)

    def hi_stage():
        hi_ref[...] = seg(5).astype(BF16)

    def hg_stage():
        gr = seg(6)
        hg_ref[...] = (gr * _sigmoid(gr)).astype(BF16)

    return [q_stage, k_stage, v_stage, hq_stage, hf_stage, hi_stage, hg_stage]


N_PROJ_IN = 5
N_PROJ_OUT = 7


def _in_proj_kernel(*refs, n_cast):
    cast_in = refs[N_PROJ_IN:N_PROJ_IN + n_cast]
    outs = refs[N_PROJ_IN + n_cast:N_PROJ_IN + n_cast + N_PROJ_OUT]
    cast_out = refs[N_PROJ_IN + n_cast + N_PROJ_OUT:]
    for stage in _in_proj_stages(*refs[:N_PROJ_IN], *outs):
        stage()
    for src, dst in zip(cast_in, cast_out):
        dst[...] = src[...].astype(BF16)


def _in_proj(x, norm_g, w_in, qg, kg, tm, next_weights, next_layer):
    n, d = x.shape
    steps = n // tm
    row = lambda i: (i, 0)
    const2 = lambda i: (0, 0)
    out_bf = jax.ShapeDtypeStruct((n, SEG), BF16)
    out_f32 = jax.ShapeDtypeStruct((n, SEG), F32)
    seg_spec = pl.BlockSpec((tm, SEG), row)
    per_blk = KEY_BLOCK // tm
    cast_in_specs, cast_out_specs, cast_out_shapes = [], [], []
    for w in next_weights:
        _, rows, cols = w.shape
        slabs = max(k for k in range(1, steps + 1)
                    if rows % k == 0 and (rows // k) % BF16_SUBLANES == 0)
        slab = lambda i, last=slabs - 1: jnp.minimum(i, last)
        cast_in_specs.append(pl.BlockSpec((None, rows // slabs, cols),
                                          lambda i, slab=slab: (next_layer, slab(i), 0)))
        cast_out_specs.append(pl.BlockSpec((rows // slabs, cols),
                                           lambda i, slab=slab: (slab(i), 0)))
        cast_out_shapes.append(jax.ShapeDtypeStruct((rows, cols), BF16))
    outs = pl.pallas_call(
        functools.partial(_in_proj_kernel, n_cast=len(next_weights)),
        grid=(steps,),
        in_specs=[
            pl.BlockSpec((tm, d), row),
            _resident((1, d), const2),
            _resident((d, N_SEG * SEG), const2),
            _resident((HEAD_LANES, 1), const2),
            _resident((HEAD_LANES, 1), const2),
        ] + cast_in_specs,
        out_specs=[pl.BlockSpec((SEG, tm), lambda i: (0, i)), seg_spec,
                   pl.BlockSpec((None, SEG, tm), lambda i: (i // per_blk, 0, i % per_blk)),
                   seg_spec, seg_spec, seg_spec, seg_spec] + cast_out_specs,
        out_shape=[jax.ShapeDtypeStruct((SEG, n), BF16), out_bf,
                   jax.ShapeDtypeStruct((n // KEY_BLOCK, SEG, KEY_BLOCK), BF16),
                   out_bf, out_f32, out_bf, out_bf] + cast_out_shapes,
        compiler_params=pltpu.CompilerParams(
            dimension_semantics=("arbitrary",), vmem_limit_bytes=VMEM_LIMIT_BYTES),
        name="in_proj",
    )(x, norm_g, w_in, qg, kg, *next_weights)
    return outs[:N_PROJ_OUT], outs[N_PROJ_OUT:]


def _lambda_full(lq1_ref, lk1_ref, lq2_ref, lk2_ref, lam_init):
    return (jnp.exp(jnp.sum(lq1_ref[...] * lk1_ref[...], axis=-1, keepdims=True))
            - jnp.exp(jnp.sum(lq2_ref[...] * lk2_ref[...], axis=-1, keepdims=True))
            + lam_init)


def _split_maps_t(qt):
    row = lax.broadcasted_iota(jnp.int32, qt.shape, 0)
    zero = jnp.zeros_like(qt)
    return jnp.concatenate([jnp.where(row < DIFF_HEAD_DIM, qt, zero),
                            jnp.where(row >= DIFF_HEAD_DIM, qt, zero)], axis=1)


def _attn_bounded_kernel(qt_ref, k_ref, vt_ref, lq1_ref, lk1_ref, lq2_ref, lk2_ref, sgc_ref,
                         o_ref, acc_ref, *, lam_init):
    blk = pl.program_id(2)
    n_sub = KEY_BLOCK // ATTN_SUB
    ones = jnp.ones((ONES_ROWS, KEY_BLOCK), BF16)

    def q2t(s):
        return _split_maps_t(qt_ref[:, s * ATTN_SUB:(s + 1) * ATTN_SUB])

    def attend(s, k_rows, vt, diagonal):
        n_keys = k_rows.shape[0]
        st = jnp.dot(k_rows, q2t(s), preferred_element_type=F32)
        p = jnp.exp2(st)
        if diagonal:
            tail = p[n_keys - ATTN_SUB:]
            krow = lax.broadcasted_iota(jnp.int32, tail.shape, 0)
            qcol = lax.broadcasted_iota(jnp.int32, tail.shape, 1)
            visible = (krow >> CHUNK_SHIFT) <= ((qcol & (ATTN_SUB - 1)) >> CHUNK_SHIFT)
            tail = jnp.where(visible, tail, 0.0)
            p = tail if n_keys == ATTN_SUB else jnp.concatenate([p[:n_keys - ATTN_SUB], tail], axis=0)
        vte = jnp.concatenate([vt, ones[:, :n_keys]], axis=0)
        return jnp.dot(vte, p.astype(BF16), preferred_element_type=F32)

    start = pl.multiple_of(blk * KEY_BLOCK, KEY_BLOCK)
    for s in range(n_sub):
        n_keys = (s + 1) * ATTN_SUB
        acc_ref[s] = attend(s, k_ref[pl.ds(start, n_keys), :], vt_ref[blk, :, :n_keys], True)

    def body(j, carry):
        k_rows = k_ref[pl.ds(pl.multiple_of(j * KEY_BLOCK, KEY_BLOCK), KEY_BLOCK), :]
        vt = vt_ref[j]
        for s in range(n_sub):
            acc_ref[s] += attend(s, k_rows, vt, False)
        return carry

    lax.fori_loop(0, blk, body, 0)

    lam = _lambda_full(lq1_ref, lk1_ref, lq2_ref, lk2_ref, lam_init)
    for s in range(n_sub):
        a = acc_ref[s]
        o = a[:HEAD_LANES] * (1.0 / a[HEAD_LANES:HEAD_LANES + 1])
        od = o[:, :ATTN_SUB] - lam * o[:, ATTN_SUB:]
        ms = jnp.mean(od * od, axis=0, keepdims=True)
        od = od * lax.rsqrt(ms + NORM_EPS) * sgc_ref[...] * (1.0 - lam_init)
        o_ref[s * ATTN_SUB:(s + 1) * ATTN_SUB, :] = od.T.astype(BF16)


def _attn_online_kernel(qt_ref, k_ref, vt_ref, lq1_ref, lk1_ref, lq2_ref, lk2_ref, sgc_ref,
                        o_ref, v_ref, *, tq, lam_init):
    i = pl.program_id(2)

    @pl.when(i == 0)
    def _():
        for j in range(vt_ref.shape[0]):
            v_ref[j * KEY_BLOCK:(j + 1) * KEY_BLOCK, :] = vt_ref[j].astype(F32).T.astype(BF16)

    q = qt_ref[...].astype(F32).T.astype(BF16)
    lane = lax.broadcasted_iota(jnp.int32, q.shape, 1)
    zero = jnp.zeros_like(q)
    q2 = jnp.concatenate([jnp.where(lane < DIFF_HEAD_DIM, q, zero),
                          jnp.where(lane >= DIFF_HEAD_DIM, q, zero)], axis=0)

    def scores(kj):
        return lax.dot_general(q2, kj, _NT, preferred_element_type=F32)

    start = pl.multiple_of(i * tq, tq)
    s = scores(k_ref[pl.ds(start, tq), :])
    row = lax.broadcasted_iota(jnp.int32, s.shape, 0)
    col = lax.broadcasted_iota(jnp.int32, s.shape, 1)
    visible = (col >> CHUNK_SHIFT) <= ((row & (tq - 1)) >> CHUNK_SHIFT)
    s = jnp.where(visible, s, -jnp.inf)
    m = jnp.max(s, axis=-1, keepdims=True)
    p = jnp.exp2(s - m)
    l = jnp.sum(p, axis=-1, keepdims=True)
    acc = jnp.dot(p.astype(BF16), v_ref[pl.ds(start, tq), :], preferred_element_type=F32)

    def body(j, carry):
        m, l, acc = carry
        off = pl.multiple_of(j * tq, tq)
        s = scores(k_ref[pl.ds(off, tq), :])
        m_new = jnp.maximum(m, jnp.max(s, axis=-1, keepdims=True))
        alpha = jnp.exp2(m - m_new)
        p = jnp.exp2(s - m_new)
        l = alpha * l + jnp.sum(p, axis=-1, keepdims=True)
        acc = alpha * acc + jnp.dot(p.astype(BF16), v_ref[pl.ds(off, tq), :],
                                    preferred_element_type=F32)
        return m_new, l, acc

    m, l, acc = lax.fori_loop(0, i, body, (m, l, acc))
    o = acc / l
    lam = _lambda_full(lq1_ref, lk1_ref, lq2_ref, lk2_ref, lam_init)
    od = o[:tq] - lam * o[tq:]
    od = od * _rsqrt_mean_sq(od) * sgc_ref[...].T * (1.0 - lam_init)
    o_ref[...] = od.astype(BF16)


def _diff_attn(qt, k, vt, lq1, lk1, lq2, lk2, subln_g_col, lam_init, batch, score_bound):
    n = k.shape[0]
    s = n // batch
    blocks = s // KEY_BLOCK
    lam_spec = _resident((1, DIFF_HEAD_DIM), lambda bi, hi, qi: (0, 0))
    k_spec = pl.BlockSpec((s, HEAD_LANES), lambda bi, hi, qi: (bi, hi))
    vt_spec = pl.BlockSpec((blocks, HEAD_LANES, KEY_BLOCK), lambda bi, hi, qi: (bi, hi, 0))
    gain_spec = _resident((HEAD_LANES, 1), lambda bi, hi, qi: (0, 0))

    def call(body, tq, scratch, name):
        per_batch = s // tq
        return pl.pallas_call(
            body,
            grid=(batch, N_HEADS, per_batch),
            in_specs=[pl.BlockSpec((HEAD_LANES, tq), lambda bi, hi, qi: (hi, bi * per_batch + qi)),
                      k_spec, vt_spec, lam_spec, lam_spec, lam_spec, lam_spec, gain_spec],
            out_specs=pl.BlockSpec((tq, HEAD_LANES), lambda bi, hi, qi: (bi * per_batch + qi, hi)),
            out_shape=jax.ShapeDtypeStruct((n, SEG), BF16),
            scratch_shapes=scratch,
            compiler_params=pltpu.CompilerParams(
                dimension_semantics=("arbitrary", "arbitrary", "arbitrary"),
                vmem_limit_bytes=VMEM_LIMIT_BYTES),
            name=name,
        )

    bounded = call(
        functools.partial(_attn_bounded_kernel, lam_init=lam_init), KEY_BLOCK,
        [pltpu.VMEM((KEY_BLOCK // ATTN_SUB, HEAD_LANES + ONES_ROWS, 2 * ATTN_SUB), F32)],
        "diff_attn_bounded")
    online = call(
        functools.partial(_attn_online_kernel, tq=ATTN_SUB, lam_init=lam_init), ATTN_SUB,
        [pltpu.VMEM((s, HEAD_LANES), BF16)], "diff_attn_online")
    return lax.cond(score_bound <= SCORE_BOUND_MAX, bounded, online,
                    qt, k, vt, lq1, lk1, lq2, lk2, subln_g_col)


def _hgrn_tables():
    c = CHUNK
    masks = np.zeros((N_LEVELS + 1, c, c), np.float32)
    masks[0] = np.eye(c, dtype=np.float32)
    for lev in range(1, N_LEVELS + 1):
        size = 1 << lev
        half = size // 2
        for r in range(c):
            if r % size >= half:
                mid = r - r % size + half
                masks[lev, r, mid - half:mid] = 1.0
    t_idx, s_idx = np.arange(c)[:, None], np.arange(c)[None, :]
    block16 = ((t_idx // BLOCK16 == s_idx // BLOCK16) & (s_idx <= t_idx)).astype(np.float32)
    masks = np.concatenate([masks, block16[None]], axis=0)
    padded = np.zeros((masks.shape[0], 2, c, 2 * c), np.float32)
    padded[:, 0, :, :c] = masks
    padded[:, 1, :, c:] = masks
    return np.tril(np.ones((c, c), np.float32)), padded


def _level_ref_row(lev, row):
    size = 1 << lev
    return row - row % size + size // 2 - 1


def _hgrn_kernel(hq_ref, hf_ref, hi_ref, hg_ref, lb_ref, g_ref, tril_ref, masks_ref,
                 o_ref, state_ref, b_ref, k_ref, lf_ref, *, layer, n_chunks):
    @pl.when(pl.program_id(1) == 0)
    def _():
        state_ref[...] = jnp.zeros_like(state_ref)

    if layer > 0:
        lbp = lb_ref[...]
        ex = jnp.exp(lbp - jnp.max(lbp, axis=0, keepdims=True))
        gamma = ex / jnp.sum(ex, axis=0, keepdims=True)
        lb = jnp.sum(gamma[1:layer + 1], axis=0, keepdims=True)

    tn = (((0,), (0,)), ((), ()))
    groups = CHUNK // SUBLANES
    sub = lax.broadcasted_iota(jnp.int32, (SUBLANES, SEG), 0)
    up_small = {lev: (sub & (1 << (lev - 1))) != 0 for lev in (1, 2, 3)}
    sgn_small = {lev: jnp.where(up_small[lev], 1.0, -1.0) for lev in (2, 3)}
    upper_groups = {lev: [g for g in range(groups)
                          if (g * SUBLANES) % (1 << lev) >= (1 << lev) // 2]
                    for lev in range(4, N_LEVELS + 1)}

    def grp(a, g):
        return a[g * SUBLANES:(g + 1) * SUBLANES, :]

    def stack(parts):
        return jnp.concatenate(parts, axis=0).astype(BF16)

    def heads_to_rows(a):
        return jnp.concatenate(
            [a[:, h * HEAD_LANES:(h + 1) * HEAD_LANES] for h in range(N_HEADS)], axis=0)

    worst = None
    for c in range(n_chunks):
        rows = slice(c * CHUNK, (c + 1) * CHUNK)
        z = hf_ref[rows, :]
        e = jnp.exp(-jnp.abs(z))
        r = 1.0 / (1.0 + e)
        sig_neg = jnp.where(z >= 0, e * r, r)
        if layer == 0:
            lf2 = jnp.minimum(z * LOG2E, 0.0) - jnp.log2(1.0 + e)
            kk = sig_neg
        else:
            sig_pos = jnp.where(z >= 0, r, e * r)
            lf2 = jnp.log2(lb + (1.0 - lb) * sig_pos)
            kk = (1.0 - lb) * sig_neg
        hi = lf2.astype(BF16)
        lo = (lf2 - hi.astype(F32)).astype(BF16)
        b2 = (jnp.dot(tril_ref[...], hi, preferred_element_type=F32)
              + jnp.dot(tril_ref[...], lo, preferred_element_type=F32))
        b_ref[c] = b2
        k_ref[c] = kk
        lf_ref[c] = lf2
        for blk in range(CHUNK // BLOCK16):
            last = (blk + 1) * BLOCK16 - 1
            span = -b2[last:last + 1, :]
            if blk:
                span = span + b2[blk * BLOCK16 - 1:blk * BLOCK16, :]
            worst = span if worst is None else jnp.maximum(worst, span)
    moderate = jnp.max(worst) <= BLOCK16_DECAY_MAX

    def chunk(c, single_reference):
        rows = slice(c * CHUNK, (c + 1) * CHUNK)
        b2 = b_ref[c]
        kk = k_ref[c]
        qq = hq_ref[rows, :].astype(F32)

        def ref_rows(row):
            return jnp.broadcast_to(b_ref[c, row:row + 1, :], (SUBLANES, SEG))

        x = {lev: [] for lev in range(1, N_LEVELS + 1)}
        xq16, xk16, q_pre, k_suf = [], [], [], []
        for g in range(groups):
            b_g, q_g, k_g = grp(b2, g), grp(qq, g), grp(kk, g)
            row0 = g * SUBLANES
            if single_reference:
                start = row0 - row0 % BLOCK16
                d = b_g - ref_rows(start - 1) if start else b_g
                xq16.append(q_g * jnp.exp2(d))
                xk16.append(k_g * jnp.exp2(-d))
                first_level = 5
            else:
                x[1].append(jnp.where(up_small[1], q_g * jnp.exp2(grp(lf_ref[c], g)), k_g))
                ref2 = jnp.where(sub < 4, ref_rows(_level_ref_row(2, row0)),
                                 ref_rows(_level_ref_row(2, row0 + 4)))
                x[2].append(jnp.where(up_small[2], q_g, k_g)
                            * jnp.exp2((b_g - ref2) * sgn_small[2]))
                ref3 = ref_rows(_level_ref_row(3, row0))
                x[3].append(jnp.where(up_small[3], q_g, k_g)
                            * jnp.exp2((b_g - ref3) * sgn_small[3]))
                first_level = 4
            for lev in range(first_level, N_LEVELS + 1):
                ref = ref_rows(_level_ref_row(lev, row0))
                if row0 % (1 << lev) >= (1 << lev) // 2:
                    x[lev].append(q_g * jnp.exp2(b_g - ref))
                else:
                    x[lev].append(k_g * jnp.exp2(ref - b_g))
            q_pre.append(q_g * jnp.exp2(b_g))
            k_suf.append(k_g * jnp.exp2(ref_rows(CHUNK - 1) - b_g))

        all_groups = list(range(groups))
        if single_reference:
            terms = [(MASK_BLOCK16, all_groups, xq16, xk16)]
        else:
            terms = [(lev, all_groups, x[lev], x[lev]) for lev in (1, 2, 3)]
        for lev in range(first_level, N_LEVELS + 1):
            terms.append((lev, upper_groups[lev], [x[lev][g] for g in upper_groups[lev]], x[lev]))

        scores = [lax.dot_general(heads_to_rows(stack(q_rows)), heads_to_rows(stack(k_rows)), _NT,
                                  preferred_element_type=F32)
                  for _, _, q_rows, k_rows in terms]

        pair_blocks = []
        for pair in range(N_HEADS // 2):
            lanes = slice(pair * HEAD_LANES, (pair + 1) * HEAD_LANES)
            a_rows = []
            for h in (2 * pair, 2 * pair + 1):
                for g in range(groups):
                    gs = slice(g * SUBLANES, (g + 1) * SUBLANES)
                    a_g = None
                    if not single_reference:
                        hs = slice(h * HEAD_LANES, (h + 1) * HEAD_LANES)
                        a_g = (jnp.sum(qq[gs, hs] * kk[gs, hs], axis=-1, keepdims=True)
                               * masks_ref[0, h % 2, gs, :])
                    for (mask_id, q_groups, _, _), sc in zip(terms, scores):
                        if g not in q_groups:
                            continue
                        r0 = (h * len(q_groups) + q_groups.index(g)) * SUBLANES
                        part = sc[r0:r0 + SUBLANES, lanes] * masks_ref[mask_id, h % 2, gs, :]
                        a_g = part if a_g is None else a_g + part
                    a_rows.append(a_g)
            pair_blocks.append(jnp.concatenate(a_rows, axis=0))
        zero = jnp.zeros_like(pair_blocks[0])
        a_full = jnp.concatenate(
            [jnp.concatenate([pair_blocks[0], zero], axis=1),
             jnp.concatenate([zero, pair_blocks[1]], axis=1)], axis=0).astype(BF16)
        o_intra = jnp.dot(a_full, heads_to_rows(hi_ref[rows, :]), preferred_element_type=F32)

        q_pre = stack(q_pre)
        k_suf = stack(k_suf)
        decay_last = jnp.exp2(b2[CHUNK - 1:CHUNK, :])
        for h in range(N_HEADS):
            hs = slice(h * HEAD_LANES, (h + 1) * HEAD_LANES)
            st = state_ref[h]
            o = (o_intra[h * CHUNK:(h + 1) * CHUNK, :]
                 + lax.dot_general(q_pre[:, hs], st.astype(BF16), _NT,
                                   preferred_element_type=F32))
            state_ref[h] = (st * decay_last[:, hs]
                            + lax.dot_general(hi_ref[rows, hs], k_suf[:, hs], tn,
                                              preferred_element_type=F32))
            o = o * _rsqrt_mean_sq(o) * g_ref[...] * hg_ref[rows, hs].astype(F32)
            o_ref[rows, hs] = o.astype(BF16)

    @pl.when(moderate)
    def _():
        for c in range(n_chunks):
            chunk(c, True)

    @pl.when(jnp.logical_not(moderate))
    def _():
        for c in range(n_chunks):
            chunk(c, False)


def _hgrn(hq, hf, hi, hg, lower_bounds, norm_g, tables, layer, tc, batch):
    n = hq.shape[0]
    tril, masks = tables
    per_batch = n // batch // tc
    tile = pl.BlockSpec((tc, SEG), lambda bi, ci: (bi * per_batch + ci, 0))
    const2 = lambda bi, ci: (0, 0)
    return pl.pallas_call(
        functools.partial(_hgrn_kernel, layer=layer, n_chunks=tc // CHUNK),
        grid=(batch, per_batch),
        in_specs=[tile, tile, tile, tile,
                  _resident(lower_bounds.shape, const2),
                  _resident((1, HEAD_LANES), const2),
                  _resident(tril.shape, const2),
                  _resident(masks.shape, lambda bi, ci: (0, 0, 0, 0))],
        out_specs=tile,
        out_shape=jax.ShapeDtypeStruct((n, SEG), BF16),
        scratch_shapes=[pltpu.VMEM((N_HEADS, HEAD_LANES, HEAD_LANES), F32)]
        + [pltpu.VMEM((tc // CHUNK, CHUNK, SEG), F32)] * 3,
        compiler_params=pltpu.CompilerParams(
            dimension_semantics=("arbitrary", "arbitrary"),
            vmem_limit_bytes=VMEM_LIMIT_BYTES),
        name="hgrn2",
    )(hq, hf, hi, hg, lower_bounds, norm_g, tril, masks)


def _out_ffn_kernel(x_ref, od_ref, or_ref, wo_ref, g_ref, wg_ref, wu_ref, wd_ref,
                    o_ref, ff_ref, *, d_ff):
    xn = (x_ref[...]
          + jnp.dot(od_ref[...], wo_ref[:SEG, :], preferred_element_type=F32)
          + jnp.dot(or_ref[...], wo_ref[SEG:, :], preferred_element_type=F32))
    h = (xn * _rsqrt_mean_sq(xn) * g_ref[...]).astype(BF16)
    for j in range(d_ff // FFN_TILE):
        cols = slice(j * FFN_TILE, (j + 1) * FFN_TILE)
        gate = jnp.dot(h, wg_ref[:, cols], preferred_element_type=F32)
        up = jnp.dot(h, wu_ref[:, cols], preferred_element_type=F32)
        ff_ref[:, cols] = (gate * _sigmoid(gate) * up).astype(BF16)
    o_ref[...] = xn + jnp.dot(ff_ref[...], wd_ref[...], preferred_element_type=F32)


def _out_ffn(x, od, orr, w_out, norm_g, w_gate, w_up, w_down, tm):
    n, d = x.shape
    d_ff = w_gate.shape[-1]
    row = lambda i: (i, 0)
    const2 = lambda i: (0, 0)
    return pl.pallas_call(
        functools.partial(_out_ffn_kernel, d_ff=d_ff),
        grid=(n // tm,),
        in_specs=[
            pl.BlockSpec((tm, d), row),
            pl.BlockSpec((tm, SEG), row),
            pl.BlockSpec((tm, SEG), row),
            _resident((d, d), const2),
            _resident((1, d), const2),
            _resident((d, d_ff), const2),
            _resident((d, d_ff), const2),
            _resident((d_ff, d), const2),
        ],
        out_specs=pl.BlockSpec((tm, d), row),
        out_shape=jax.ShapeDtypeStruct((n, d), F32),
        scratch_shapes=[pltpu.VMEM((tm, d_ff), BF16)],
        compiler_params=pltpu.CompilerParams(
            dimension_semantics=("arbitrary",), vmem_limit_bytes=VMEM_LIMIT_BYTES),
        name="out_ffn",
    )(x, od, orr, w_out, norm_g, w_gate, w_up, w_down)


def _tile(n, want):
    return min(n, want)


def kernel(x, attn_norm_g, w_in, q_norm_g, k_norm_g, lambda_q1, lambda_k1, lambda_q2, lambda_k2, subln_g, lower_bounds, hgrn_norm_g, w_out, ffn_norm_g, w_gate, w_up, w_down):
    b, s, d = x.shape
    depth = w_in.shape[0]
    n = b * s
    assert d == 2 * SEG and w_in.shape[-1] == N_SEG * SEG and s % KEY_BLOCK == 0

    weights_f32 = (w_in, w_out, w_gate, w_up, w_down)
    weights_b = [w[0].astype(BF16) for w in weights_f32]

    tril, masks = _hgrn_tables()
    tables = (jnp.asarray(tril, BF16), jnp.asarray(masks))
    lower_bounds = lower_bounds.astype(F32)

    tm_proj = _tile(s, PROJ_ROWS)
    tm_hgrn = _tile(s, HGRN_ROWS)
    tm_ffn = _tile(s, FFN_ROWS)

    xf = x.reshape(n, d)
    for l in range(depth):
        lam_init = 0.8 - 0.6 * math.exp(-0.3 * l)
        qg = jnp.tile(q_norm_g[l], 2)[:, None] * (DIFF_HEAD_DIM ** -0.5 * LOG2E)
        kg = jnp.tile(k_norm_g[l], 2)[:, None]
        score_bound = (1.02 * math.sqrt(DIFF_HEAD_DIM)
                       * jnp.max(jnp.abs(q_norm_g[l])) * jnp.max(jnp.abs(k_norm_g[l])))
        w_in_b, w_out_b, w_gate_b, w_up_b, w_down_b = weights_b
        (qt, k, vt, hq, hf, hi, hg), weights_b = _in_proj(
            xf, attn_norm_g[l][None], w_in_b, qg, kg, tm_proj,
            weights_f32 if l + 1 < depth else (), l + 1)
        orr = _hgrn(hq, hf, hi, hg, lower_bounds, hgrn_norm_g[l][None], tables, l, tm_hgrn, b)
        od = _diff_attn(qt, k, vt, lambda_q1[l][None], lambda_k1[l][None],
                        lambda_q2[l][None], lambda_k2[l][None], subln_g[l][:, None], lam_init, b,
                        score_bound)
        xf = _out_ffn(xf, od, orr, w_out_b, ffn_norm_g[l][None],
                      w_gate_b, w_up_b, w_down_b, tm_ffn)
    return xf.reshape(b, s, d)
```

```python
import functools
import math

import numpy as np
import jax
import jax.numpy as jnp
from jax import lax
from jax.experimental import pallas as pl
from jax.experimental.pallas import tpu as pltpu

F32 = jnp.float32
BF16 = jnp.bfloat16

NORM_EPS = 1e-6
CHUNK = 64
CHUNK_SHIFT = 6
HEAD_LANES = 128
N_HEADS = 4
SEG = N_HEADS * HEAD_LANES
DIFF_HEAD_DIM = 64
N_SEG = 7
N_LEVELS = CHUNK_SHIFT

VMEM_LIMIT_BYTES = 52 * 1024 * 1024
FFN_TILE = 256
PROJ_ROWS = 1024
HGRN_ROWS = 1024
FFN_ROWS = 1024
KEY_BLOCK = 4096
ATTN_SUB = 256
ONES_ROWS = 16
SCORE_BOUND_MAX = 40.0
_NT = (((1,), (1,)), ((), ()))
SUBLANES = 8
BF16_SUBLANES = 16
LOG2E = math.log2(math.e)
BLOCK16 = 16
MASK_BLOCK16 = N_LEVELS + 1
BLOCK16_DECAY_MAX = 60.0


def _rsqrt_mean_sq(y):
    return lax.rsqrt(jnp.mean(y * y, axis=-1, keepdims=True) + NORM_EPS)


def _sigmoid(z):
    return 1.0 / (1.0 + jnp.exp(-z))


def _resident(block_shape, index_map):
    return pl.BlockSpec(block_shape, index_map, pipeline_mode=pl.Buffered(1))


def _in_proj_stages(x_ref, g_ref, w_ref, qg_ref, kg_ref,
                    qt_ref, k_ref, vt_ref, hq_ref, hf_ref, hi_ref, hg_ref):
    x = x_ref[...]
    h = (x * _rsqrt_mean_sq(x) * g_ref[...]).astype(BF16)

    def seg(s):
        return jnp.dot(h, w_ref[:, s * SEG:(s + 1) * SEG], preferred_element_type=F32)

    def head_norm_t(y, gain_ref, c):
        t = y[:, c * HEAD_LANES:(c + 1) * HEAD_LANES].T
        halves = []
        for m in range(HEAD_LANES // DIFF_HEAD_DIM):
            tm_ = t[m * DIFF_HEAD_DIM:(m + 1) * DIFF_HEAD_DIM]
            ms = jnp.mean(tm_ * tm_, axis=0, keepdims=True)
            halves.append(tm_ * lax.rsqrt(ms + NORM_EPS))
        return jnp.concatenate(halves, axis=0) * gain_ref[...]

    def q_stage():
        yq = seg(0)
        for c in range(N_HEADS):
            lanes = slice(c * HEAD_LANES, (c + 1) * HEAD_LANES)
            qt_ref[lanes, :] = head_norm_t(yq, qg_ref, c).astype(BF16)

    def k_stage():
        yk = seg(1)
        for c in range(N_HEADS):
            lanes = slice(c * HEAD_LANES, (c + 1) * HEAD_LANES)
            k_ref[:, lanes] = head_norm_t(yk, kg_ref, c).T.astype(BF16)

    def v_stage():
        vt_ref[...] = seg(2).T.astype(BF16)

    def hq_stage():
        qr = seg(3)
        hq_ref[...] = (qr * _sigmoid(qr)).astype(BF16)

    def hf_stage():
        hf_ref[...] = seg(4)

    def hi_stage():
        hi_ref[...] = seg(5).astype(BF16)

    def hg_stage():
        gr = seg(6)
        hg_ref[...] = (gr * _sigmoid(gr)).astype(BF16)

    return [q_stage, k_stage, v_stage, hq_stage, hf_stage, hi_stage, hg_stage]


N_PROJ_IN = 5
N_PROJ_OUT = 7


def _in_proj_kernel(*refs, n_cast):
    cast_in = refs[N_PROJ_IN:N_PROJ_IN + n_cast]
    outs = refs[N_PROJ_IN + n_cast:N_PROJ_IN + n_cast + N_PROJ_OUT]
    cast_out = refs[N_PROJ_IN + n_cast + N_PROJ_OUT:]
    for stage in _in_proj_stages(*refs[:N_PROJ_IN], *outs):
        stage()
    for src, dst in zip(cast_in, cast_out):
        dst[...] = src[...].astype(BF16)


def _in_proj(x, norm_g, w_in, qg, kg, tm, convert):
    n, d = x.shape
    steps = n // tm
    row = lambda i: (i, 0)
    const2 = lambda i: (0, 0)
    out_bf = jax.ShapeDtypeStruct((n, SEG), BF16)
    out_f32 = jax.ShapeDtypeStruct((n, SEG), F32)
    seg_spec = pl.BlockSpec((tm, SEG), row)
    per_blk = KEY_BLOCK // tm
    cast_in_specs, cast_out_specs, cast_out_shapes = [], [], []
    for w, layer in convert:
        _, rows, cols = w.shape
        slabs = max(k for k in range(1, steps + 1)
                    if rows % k == 0 and (rows // k) % BF16_SUBLANES == 0)
        slab = lambda i, last=slabs - 1: jnp.minimum(i, last)
        cast_in_specs.append(pl.BlockSpec((None, rows // slabs, cols),
                                          lambda i, slab=slab, layer=layer: (layer, slab(i), 0)))
        cast_out_specs.append(pl.BlockSpec((rows // slabs, cols),
                                           lambda i, slab=slab: (slab(i), 0)))
        cast_out_shapes.append(jax.ShapeDtypeStruct((rows, cols), BF16))
    outs = pl.pallas_call(
        functools.partial(_in_proj_kernel, n_cast=len(convert)),
        grid=(steps,),
        in_specs=[
            pl.BlockSpec((tm, d), row),
            _resident((1, d), const2),
            _resident((d, N_SEG * SEG), const2),
            _resident((HEAD_LANES, 1), const2),
            _resident((HEAD_LANES, 1), const2),
        ] + cast_in_specs,
        out_specs=[pl.BlockSpec((SEG, tm), lambda i: (0, i)), seg_spec,
                   pl.BlockSpec((None, SEG, tm), lambda i: (i // per_blk, 0, i % per_blk)),
                   seg_spec, seg_spec, seg_spec, seg_spec] + cast_out_specs,
        out_shape=[jax.ShapeDtypeStruct((SEG, n), BF16), out_bf,
                   jax.ShapeDtypeStruct((n // KEY_BLOCK, SEG, KEY_BLOCK), BF16),
                   out_bf, out_f32, out_bf, out_bf] + cast_out_shapes,
        compiler_params=pltpu.CompilerParams(
            dimension_semantics=("arbitrary",), vmem_limit_bytes=VMEM_LIMIT_BYTES),
        name="in_proj",
    )(x, norm_g, w_in, qg, kg, *[w for w, _ in convert])
    return outs[:N_PROJ_OUT], outs[N_PROJ_OUT:]


def _lambda_full(lq1_ref, lk1_ref, lq2_ref, lk2_ref, lam_init):
    return (jnp.exp(jnp.sum(lq1_ref[...] * lk1_ref[...], axis=-1, keepdims=True))
            - jnp.exp(jnp.sum(lq2_ref[...] * lk2_ref[...], axis=-1, keepdims=True))
            + lam_init)


def _split_maps_t(qt):
    row = lax.broadcasted_iota(jnp.int32, qt.shape, 0)
    zero = jnp.zeros_like(qt)
    return jnp.concatenate([jnp.where(row < DIFF_HEAD_DIM, qt, zero),
                            jnp.where(row >= DIFF_HEAD_DIM, qt, zero)], axis=1)


def _attn_bounded_kernel(qt_ref, k_ref, vt_ref, lq1_ref, lk1_ref, lq2_ref, lk2_ref, sgc_ref,
                         o_ref, acc_ref, *, lam_init):
    blk = pl.program_id(2)
    n_sub = KEY_BLOCK // ATTN_SUB
    ones = jnp.ones((ONES_ROWS, KEY_BLOCK), BF16)

    def q2t(s):
        return _split_maps_t(qt_ref[:, s * ATTN_SUB:(s + 1) * ATTN_SUB])

    def attend(s, k_rows, vt, diagonal):
        n_keys = k_rows.shape[0]
        st = jnp.dot(k_rows, q2t(s), preferred_element_type=F32)
        p = jnp.exp2(st)
        if diagonal:
            tail = p[n_keys - ATTN_SUB:]
            krow = lax.broadcasted_iota(jnp.int32, tail.shape, 0)
            qcol = lax.broadcasted_iota(jnp.int32, tail.shape, 1)
            visible = (krow >> CHUNK_SHIFT) <= ((qcol & (ATTN_SUB - 1)) >> CHUNK_SHIFT)
            tail = jnp.where(visible, tail, 0.0)
            p = tail if n_keys == ATTN_SUB else jnp.concatenate([p[:n_keys - ATTN_SUB], tail], axis=0)
        vte = jnp.concatenate([vt, ones[:, :n_keys]], axis=0)
        return jnp.dot(vte, p.astype(BF16), preferred_element_type=F32)

    start = pl.multiple_of(blk * KEY_BLOCK, KEY_BLOCK)
    for s in range(n_sub):
        n_keys = (s + 1) * ATTN_SUB
        acc_ref[s] = attend(s, k_ref[pl.ds(start, n_keys), :], vt_ref[blk, :, :n_keys], True)

    def body(j, carry):
        k_rows = k_ref[pl.ds(pl.multiple_of(j * KEY_BLOCK, KEY_BLOCK), KEY_BLOCK), :]
        vt = vt_ref[j]
        for s in range(n_sub):
            acc_ref[s] += attend(s, k_rows, vt, False)
        return carry

    lax.fori_loop(0, blk, body, 0)

    lam = _lambda_full(lq1_ref, lk1_ref, lq2_ref, lk2_ref, lam_init)
    for s in range(n_sub):
        a = acc_ref[s]
        o = a[:HEAD_LANES] * (1.0 / a[HEAD_LANES:HEAD_LANES + 1])
        od = o[:, :ATTN_SUB] - lam * o[:, ATTN_SUB:]
        ms = jnp.mean(od * od, axis=0, keepdims=True)
        od = od * lax.rsqrt(ms + NORM_EPS) * sgc_ref[...] * (1.0 - lam_init)
        o_ref[s * ATTN_SUB:(s + 1) * ATTN_SUB, :] = od.T.astype(BF16)


def _attn_online_kernel(qt_ref, k_ref, vt_ref, lq1_ref, lk1_ref, lq2_ref, lk2_ref, sgc_ref,
                        o_ref, v_ref, *, tq, lam_init):
    i = pl.program_id(2)

    @pl.when(i == 0)
    def _():
        for j in range(vt_ref.shape[0]):
            v_ref[j * KEY_BLOCK:(j + 1) * KEY_BLOCK, :] = vt_ref[j].astype(F32).T.astype(BF16)

    q = qt_ref[...].astype(F32).T.astype(BF16)
    lane = lax.broadcasted_iota(jnp.int32, q.shape, 1)
    zero = jnp.zeros_like(q)
    q2 = jnp.concatenate([jnp.where(lane < DIFF_HEAD_DIM, q, zero),
                          jnp.where(lane >= DIFF_HEAD_DIM, q, zero)], axis=0)

    def scores(kj):
        return lax.dot_general(q2, kj, _NT, preferred_element_type=F32)

    start = pl.multiple_of(i * tq, tq)
    s = scores(k_ref[pl.ds(start, tq), :])
    row = lax.broadcasted_iota(jnp.int32, s.shape, 0)
    col = lax.broadcasted_iota(jnp.int32, s.shape, 1)
    visible = (col >> CHUNK_SHIFT) <= ((row & (tq - 1)) >> CHUNK_SHIFT)
    s = jnp.where(visible, s, -jnp.inf)
    m = jnp.max(s, axis=-1, keepdims=True)
    p = jnp.exp2(s - m)
    l = jnp.sum(p, axis=-1, keepdims=True)
    acc = jnp.dot(p.astype(BF16), v_ref[pl.ds(start, tq), :], preferred_element_type=F32)

    def body(j, carry):
        m, l, acc = carry
        off = pl.multiple_of(j * tq, tq)
        s = scores(k_ref[pl.ds(off, tq), :])
        m_new = jnp.maximum(m, jnp.max(s, axis=-1, keepdims=True))
        alpha = jnp.exp2(m - m_new)
        p = jnp.exp2(s - m_new)
        l = alpha * l + jnp.sum(p, axis=-1, keepdims=True)
        acc = alpha * acc + jnp.dot(p.astype(BF16), v_ref[pl.ds(off, tq), :],
                                    preferred_element_type=F32)
        return m_new, l, acc

    m, l, acc = lax.fori_loop(0, i, body, (m, l, acc))
    o = acc / l
    lam = _lambda_full(lq1_ref, lk1_ref, lq2_ref, lk2_ref, lam_init)
    od = o[:tq] - lam * o[tq:]
    od = od * _rsqrt_mean_sq(od) * sgc_ref[...].T * (1.0 - lam_init)
    o_ref[...] = od.astype(BF16)


def _diff_attn(qt, k, vt, lq1, lk1, lq2, lk2, subln_g_col, lam_init, batch, score_bound):
    n = k.shape[0]
    s = n // batch
    blocks = s // KEY_BLOCK
    lam_spec = _resident((1, DIFF_HEAD_DIM), lambda bi, hi, qi: (0, 0))
    k_spec = pl.BlockSpec((s, HEAD_LANES), lambda bi, hi, qi: (bi, hi))
    vt_spec = pl.BlockSpec((blocks, HEAD_LANES, KEY_BLOCK), lambda bi, hi, qi: (bi, hi, 0))
    gain_spec = _resident((HEAD_LANES, 1), lambda bi, hi, qi: (0, 0))

    def call(body, tq, scratch, name):
        per_batch = s // tq
        return pl.pallas_call(
            body,
            grid=(batch, N_HEADS, per_batch),
            in_specs=[pl.BlockSpec((HEAD_LANES, tq), lambda bi, hi, qi: (hi, bi * per_batch + qi)),
                      k_spec, vt_spec, lam_spec, lam_spec, lam_spec, lam_spec, gain_spec],
            out_specs=pl.BlockSpec((tq, HEAD_LANES), lambda bi, hi, qi: (bi * per_batch + qi, hi)),
            out_shape=jax.ShapeDtypeStruct((n, SEG), BF16),
            scratch_shapes=scratch,
            compiler_params=pltpu.CompilerParams(
                dimension_semantics=("arbitrary", "arbitrary", "arbitrary"),
                vmem_limit_bytes=VMEM_LIMIT_BYTES),
            name=name,
        )

    bounded = call(
        functools.partial(_attn_bounded_kernel, lam_init=lam_init), KEY_BLOCK,
        [pltpu.VMEM((KEY_BLOCK // ATTN_SUB, HEAD_LANES + ONES_ROWS, 2 * ATTN_SUB), F32)],
        "diff_attn_bounded")
    online = call(
        functools.partial(_attn_online_kernel, tq=ATTN_SUB, lam_init=lam_init), ATTN_SUB,
        [pltpu.VMEM((s, HEAD_LANES), BF16)], "diff_attn_online")
    return lax.cond(score_bound <= SCORE_BOUND_MAX, bounded, online,
                    qt, k, vt, lq1, lk1, lq2, lk2, subln_g_col)


def _hgrn_tables():
    c = CHUNK
    masks = np.zeros((N_LEVELS + 1, c, c), np.float32)
    masks[0] = np.eye(c, dtype=np.float32)
    for lev in range(1, N_LEVELS + 1):
        size = 1 << lev
        half = size // 2
        for r in range(c):
            if r % size >= half:
                mid = r - r % size + half
                masks[lev, r, mid - half:mid] = 1.0
    t_idx, s_idx = np.arange(c)[:, None], np.arange(c)[None, :]
    block16 = ((t_idx // BLOCK16 == s_idx // BLOCK16) & (s_idx <= t_idx)).astype(np.float32)
    masks = np.concatenate([masks, block16[None]], axis=0)
    padded = np.zeros((masks.shape[0], 2, c, 2 * c), np.float32)
    padded[:, 0, :, :c] = masks
    padded[:, 1, :, c:] = masks
    return np.tril(np.ones((c, c), np.float32)), padded


def _level_ref_row(lev, row):
    size = 1 << lev
    return row - row % size + size // 2 - 1


def _hgrn_kernel(hq_ref, hf_ref, hi_ref, hg_ref, lb_ref, g_ref, tril_ref, masks_ref,
                 o_ref, state_ref, b_ref, k_ref, lf_ref, *, layer, n_chunks):
    @pl.when(pl.program_id(1) == 0)
    def _():
        state_ref[...] = jnp.zeros_like(state_ref)

    if layer > 0:
        lbp = lb_ref[...]
        ex = jnp.exp(lbp - jnp.max(lbp, axis=0, keepdims=True))
        gamma = ex / jnp.sum(ex, axis=0, keepdims=True)
        lb = jnp.sum(gamma[1:layer + 1], axis=0, keepdims=True)

    tn = (((0,), (0,)), ((), ()))
    groups = CHUNK // SUBLANES
    sub = lax.broadcasted_iota(jnp.int32, (SUBLANES, SEG), 0)
    up_small = {lev: (sub & (1 << (lev - 1))) != 0 for lev in (1, 2, 3)}
    sgn_small = {lev: jnp.where(up_small[lev], 1.0, -1.0) for lev in (2, 3)}
    upper_groups = {lev: [g for g in range(groups)
                          if (g * SUBLANES) % (1 << lev) >= (1 << lev) // 2]
                    for lev in range(4, N_LEVELS + 1)}

    def grp(a, g):
        return a[g * SUBLANES:(g + 1) * SUBLANES, :]

    def stack(parts):
        return jnp.concatenate(parts, axis=0).astype(BF16)

    def heads_to_rows(a):
        return jnp.concatenate(
            [a[:, h * HEAD_LANES:(h + 1) * HEAD_LANES] for h in range(N_HEADS)], axis=0)

    worst = None
    for c in range(n_chunks):
        rows = slice(c * CHUNK, (c + 1) * CHUNK)
        z = hf_ref[rows, :]
        e = jnp.exp(-jnp.abs(z))
        r = 1.0 / (1.0 + e)
        sig_neg = jnp.where(z >= 0, e * r, r)
        if layer == 0:
            lf2 = jnp.minimum(z * LOG2E, 0.0) - jnp.log2(1.0 + e)
            kk = sig_neg
        else:
            sig_pos = jnp.where(z >= 0, r, e * r)
            lf2 = jnp.log2(lb + (1.0 - lb) * sig_pos)
            kk = (1.0 - lb) * sig_neg
        hi = lf2.astype(BF16)
        lo = (lf2 - hi.astype(F32)).astype(BF16)
        b2 = (jnp.dot(tril_ref[...], hi, preferred_element_type=F32)
              + jnp.dot(tril_ref[...], lo, preferred_element_type=F32))
        b_ref[c] = b2
        k_ref[c] = kk
        lf_ref[c] = lf2
        for blk in range(CHUNK // BLOCK16):
            last = (blk + 1) * BLOCK16 - 1
            span = -b2[last:last + 1, :]
            if blk:
                span = span + b2[blk * BLOCK16 - 1:blk * BLOCK16, :]
            worst = span if worst is None else jnp.maximum(worst, span)
    moderate = jnp.max(worst) <= BLOCK16_DECAY_MAX

    def chunk(c, single_reference):
        rows = slice(c * CHUNK, (c + 1) * CHUNK)
        b2 = b_ref[c]
        kk = k_ref[c]
        qq = hq_ref[rows, :].astype(F32)

        def ref_rows(row):
            return jnp.broadcast_to(b_ref[c, row:row + 1, :], (SUBLANES, SEG))

        x = {lev: [] for lev in range(1, N_LEVELS + 1)}
        xq16, xk16, q_pre, k_suf = [], [], [], []
        for g in range(groups):
            b_g, q_g, k_g = grp(b2, g), grp(qq, g), grp(kk, g)
            row0 = g * SUBLANES
            if single_reference:
                start = row0 - row0 % BLOCK16
                d = b_g - ref_rows(start - 1) if start else b_g
                xq16.append(q_g * jnp.exp2(d))
                xk16.append(k_g * jnp.exp2(-d))
                first_level = 5
            else:
                x[1].append(jnp.where(up_small[1], q_g * jnp.exp2(grp(lf_ref[c], g)), k_g))
                ref2 = jnp.where(sub < 4, ref_rows(_level_ref_row(2, row0)),
                                 ref_rows(_level_ref_row(2, row0 + 4)))
                x[2].append(jnp.where(up_small[2], q_g, k_g)
                            * jnp.exp2((b_g - ref2) * sgn_small[2]))
                ref3 = ref_rows(_level_ref_row(3, row0))
                x[3].append(jnp.where(up_small[3], q_g, k_g)
                            * jnp.exp2((b_g - ref3) * sgn_small[3]))
                first_level = 4
            for lev in range(first_level, N_LEVELS + 1):
                ref = ref_rows(_level_ref_row(lev, row0))
                if row0 % (1 << lev) >= (1 << lev) // 2:
                    x[lev].append(q_g * jnp.exp2(b_g - ref))
                else:
                    x[lev].append(k_g * jnp.exp2(ref - b_g))
            q_pre.append(q_g * jnp.exp2(b_g))
            k_suf.append(k_g * jnp.exp2(ref_rows(CHUNK - 1) - b_g))

        all_groups = list(range(groups))
        if single_reference:
            terms = [(MASK_BLOCK16, all_groups, xq16, xk16)]
        else:
            terms = [(lev, all_groups, x[lev], x[lev]) for lev in (1, 2, 3)]
        for lev in range(first_level, N_LEVELS + 1):
            terms.append((lev, upper_groups[lev], [x[lev][g] for g in upper_groups[lev]], x[lev]))

        scores = [lax.dot_general(heads_to_rows(stack(q_rows)), heads_to_rows(stack(k_rows)), _NT,
                                  preferred_element_type=F32)
                  for _, _, q_rows, k_rows in terms]

        pair_blocks = []
        for pair in range(N_HEADS // 2):
            lanes = slice(pair * HEAD_LANES, (pair + 1) * HEAD_LANES)
            a_rows = []
            for h in (2 * pair, 2 * pair + 1):
                for g in range(groups):
                    gs = slice(g * SUBLANES, (g + 1) * SUBLANES)
                    a_g = None
                    if not single_reference:
                        hs = slice(h * HEAD_LANES, (h + 1) * HEAD_LANES)
                        a_g = (jnp.sum(qq[gs, hs] * kk[gs, hs], axis=-1, keepdims=True)
                               * masks_ref[0, h % 2, gs, :])
                    for (mask_id, q_groups, _, _), sc in zip(terms, scores):
                        if g not in q_groups:
                            continue
                        r0 = (h * len(q_groups) + q_groups.index(g)) * SUBLANES
                        part = sc[r0:r0 + SUBLANES, lanes] * masks_ref[mask_id, h % 2, gs, :]
                        a_g = part if a_g is None else a_g + part
                    a_rows.append(a_g)
            pair_blocks.append(jnp.concatenate(a_rows, axis=0))
        zero = jnp.zeros_like(pair_blocks[0])
        a_full = jnp.concatenate(
            [jnp.concatenate([pair_blocks[0], zero], axis=1),
             jnp.concatenate([zero, pair_blocks[1]], axis=1)], axis=0).astype(BF16)
        o_intra = jnp.dot(a_full, heads_to_rows(hi_ref[rows, :]), preferred_element_type=F32)

        q_pre = stack(q_pre)
        k_suf = stack(k_suf)
        decay_last = jnp.exp2(b2[CHUNK - 1:CHUNK, :])
        for h in range(N_HEADS):
            hs = slice(h * HEAD_LANES, (h + 1) * HEAD_LANES)
            st = state_ref[h]
            o = (o_intra[h * CHUNK:(h + 1) * CHUNK, :]
                 + lax.dot_general(q_pre[:, hs], st.astype(BF16), _NT,
                                   preferred_element_type=F32))
            state_ref[h] = (st * decay_last[:, hs]
                            + lax.dot_general(hi_ref[rows, hs], k_suf[:, hs], tn,
                                              preferred_element_type=F32))
            o = o * _rsqrt_mean_sq(o) * g_ref[...] * hg_ref[rows, hs].astype(F32)
            o_ref[rows, hs] = o.astype(BF16)

    @pl.when(moderate)
    def _():
        for c in range(n_chunks):
            chunk(c, True)

    @pl.when(jnp.logical_not(moderate))
    def _():
        for c in range(n_chunks):
            chunk(c, False)


def _hgrn(hq, hf, hi, hg, lower_bounds, norm_g, tables, layer, tc, batch):
    n = hq.shape[0]
    tril, masks = tables
    per_batch = n // batch // tc
    tile = pl.BlockSpec((tc, SEG), lambda bi, ci: (bi * per_batch + ci, 0))
    const2 = lambda bi, ci: (0, 0)
    return pl.pallas_call(
        functools.partial(_hgrn_kernel, layer=layer, n_chunks=tc // CHUNK),
        grid=(batch, per_batch),
        in_specs=[tile, tile, tile, tile,
                  _resident(lower_bounds.shape, const2),
                  _resident((1, HEAD_LANES), const2),
                  _resident(tril.shape, const2),
                  _resident(masks.shape, lambda bi, ci: (0, 0, 0, 0))],
        out_specs=tile,
        out_shape=jax.ShapeDtypeStruct((n, SEG), BF16),
        scratch_shapes=[pltpu.VMEM((N_HEADS, HEAD_LANES, HEAD_LANES), F32)]
        + [pltpu.VMEM((tc // CHUNK, CHUNK, SEG), F32)] * 3,
        compiler_params=pltpu.CompilerParams(
            dimension_semantics=("arbitrary", "arbitrary"),
            vmem_limit_bytes=VMEM_LIMIT_BYTES),
        name="hgrn2",
    )(hq, hf, hi, hg, lower_bounds, norm_g, tril, masks)


def _out_ffn_kernel(x_ref, od_ref, or_ref, wo_ref, g_ref, wg_ref, wu_ref, wd_ref,
                    o_ref, ff_ref, *, d_ff):
    xn = (x_ref[...]
          + jnp.dot(od_ref[...], wo_ref[:SEG, :], preferred_element_type=F32)
          + jnp.dot(or_ref[...], wo_ref[SEG:, :], preferred_element_type=F32))
    h = (xn * _rsqrt_mean_sq(xn) * g_ref[...]).astype(BF16)
    for j in range(d_ff // FFN_TILE):
        cols = slice(j * FFN_TILE, (j + 1) * FFN_TILE)
        gate = jnp.dot(h, wg_ref[:, cols], preferred_element_type=F32)
        up = jnp.dot(h, wu_ref[:, cols], preferred_element_type=F32)
        ff_ref[:, cols] = (gate * _sigmoid(gate) * up).astype(BF16)
    o_ref[...] = xn + jnp.dot(ff_ref[...], wd_ref[...], preferred_element_type=F32)


def _out_ffn(x, od, orr, w_out, norm_g, w_gate, w_up, w_down, tm):
    n, d = x.shape
    d_ff = w_gate.shape[-1]
    row = lambda i: (i, 0)
    const2 = lambda i: (0, 0)
    return pl.pallas_call(
        functools.partial(_out_ffn_kernel, d_ff=d_ff),
        grid=(n // tm,),
        in_specs=[
            pl.BlockSpec((tm, d), row),
            pl.BlockSpec((tm, SEG), row),
            pl.BlockSpec((tm, SEG), row),
            _resident((d, d), const2),
            _resident((1, d), const2),
            _resident((d, d_ff), const2),
            _resident((d, d_ff), const2),
            _resident((d_ff, d), const2),
        ],
        out_specs=pl.BlockSpec((tm, d), row),
        out_shape=jax.ShapeDtypeStruct((n, d), F32),
        scratch_shapes=[pltpu.VMEM((tm, d_ff), BF16)],
        compiler_params=pltpu.CompilerParams(
            dimension_semantics=("arbitrary",), vmem_limit_bytes=VMEM_LIMIT_BYTES),
        name="out_ffn",
    )(x, od, orr, w_out, norm_g, w_gate, w_up, w_down)


def _tile(n, want):
    return min(n, want)


def kernel(x, attn_norm_g, w_in, q_norm_g, k_norm_g, lambda_q1, lambda_k1, lambda_q2, lambda_k2, subln_g, lower_bounds, hgrn_norm_g, w_out, ffn_norm_g, w_gate, w_up, w_down):
    b, s, d = x.shape
    depth = w_in.shape[0]
    n = b * s
    assert d == 2 * SEG and w_in.shape[-1] == N_SEG * SEG and s % KEY_BLOCK == 0

    w_in_b = w_in[0].astype(BF16)

    tril, masks = _hgrn_tables()
    tables = (jnp.asarray(tril, BF16), jnp.asarray(masks))
    lower_bounds = lower_bounds.astype(F32)

    tm_proj = _tile(s, PROJ_ROWS)
    tm_hgrn = _tile(s, HGRN_ROWS)
    tm_ffn = _tile(s, FFN_ROWS)

    xf = x.reshape(n, d)
    for l in range(depth):
        lam_init = 0.8 - 0.6 * math.exp(-0.3 * l)
        qg = jnp.tile(q_norm_g[l], 2)[:, None] * (DIFF_HEAD_DIM ** -0.5 * LOG2E)
        kg = jnp.tile(k_norm_g[l], 2)[:, None]
        score_bound = (1.02 * math.sqrt(DIFF_HEAD_DIM)
                       * jnp.max(jnp.abs(q_norm_g[l])) * jnp.max(jnp.abs(k_norm_g[l])))
        convert = [(w_out, l), (w_gate, l), (w_up, l), (w_down, l)]
        if l + 1 < depth:
            convert.append((w_in, l + 1))
        (qt, k, vt, hq, hf, hi, hg), converted = _in_proj(
            xf, attn_norm_g[l][None], w_in_b, qg, kg, tm_proj, convert)
        w_out_b, w_gate_b, w_up_b, w_down_b = converted[:4]
        w_in_b = converted[4] if l + 1 < depth else None
        orr = _hgrn(hq, hf, hi, hg, lower_bounds, hgrn_norm_g[l][None], tables, l, tm_hgrn, b)
        od = _diff_attn(qt, k, vt, lambda_q1[l][None], lambda_k1[l][None],
                        lambda_q2[l][None], lambda_k2[l][None], subln_g[l][:, None], lam_init, b,
                        score_bound)
        xf = _out_ffn(xf, od, orr, w_out_b, ffn_norm_g[l][None],
                      w_gate_b, w_up_b, w_down_b, tm_ffn)
    return xf.reshape(b, s, d)
```

```python
import functools
import math

import numpy as np
import jax
import jax.numpy as jnp
from jax import lax
from jax.experimental import pallas as pl
from jax.experimental.pallas import tpu as pltpu

F32 = jnp.float32
BF16 = jnp.bfloat16

NORM_EPS = 1e-6
CHUNK = 64
CHUNK_SHIFT = 6
HEAD_LANES = 128
N_HEADS = 4
SEG = N_HEADS * HEAD_LANES
DIFF_HEAD_DIM = 64
N_SEG = 7
N_LEVELS = CHUNK_SHIFT

VMEM_LIMIT_BYTES = 52 * 1024 * 1024
FFN_TILE = 256
PROJ_ROWS = 1024
HGRN_ROWS = 1024
FFN_ROWS = 1024
KEY_BLOCK = 4096
ATTN_SUB = 256
ONES_ROWS = 16
SCORE_BOUND_MAX = 40.0
_NT = (((1,), (1,)), ((), ()))
SUBLANES = 8
BF16_SUBLANES = 16
LOG2E = math.log2(math.e)
REF_BLOCK_LEVEL = 4
REF_BLOCK = 1 << REF_BLOCK_LEVEL
MASK_REF_BLOCK = N_LEVELS + 1
REF_BLOCK_DECAY_MAX = 60.0


def _rsqrt_mean_sq(y):
    return lax.rsqrt(jnp.mean(y * y, axis=-1, keepdims=True) + NORM_EPS)


def _sigmoid(z):
    return 1.0 / (1.0 + jnp.exp(-z))


def _resident(block_shape, index_map):
    return pl.BlockSpec(block_shape, index_map, pipeline_mode=pl.Buffered(1))


def _in_proj_stages(x_ref, g_ref, w_ref, qg_ref, kg_ref,
                    qt_ref, k_ref, vt_ref, hq_ref, hf_ref, hi_ref, hg_ref):
    x = x_ref[...]
    h = (x * _rsqrt_mean_sq(x) * g_ref[...]).astype(BF16)

    def seg(s):
        return jnp.dot(h, w_ref[:, s * SEG:(s + 1) * SEG], preferred_element_type=F32)

    def head_norm_t(y, gain_ref, c):
        t = y[:, c * HEAD_LANES:(c + 1) * HEAD_LANES].T
        halves = []
        for m in range(HEAD_LANES // DIFF_HEAD_DIM):
            tm_ = t[m * DIFF_HEAD_DIM:(m + 1) * DIFF_HEAD_DIM]
            ms = jnp.mean(tm_ * tm_, axis=0, keepdims=True)
            halves.append(tm_ * lax.rsqrt(ms + NORM_EPS))
        return jnp.concatenate(halves, axis=0) * gain_ref[...]

    def q_stage():
        yq = seg(0)
        for c in range(N_HEADS):
            lanes = slice(c * HEAD_LANES, (c + 1) * HEAD_LANES)
            qt_ref[lanes, :] = head_norm_t(yq, qg_ref, c).astype(BF16)

    def k_stage():
        yk = seg(1)
        for c in range(N_HEADS):
            lanes = slice(c * HEAD_LANES, (c + 1) * HEAD_LANES)
            k_ref[:, lanes] = head_norm_t(yk, kg_ref, c).T.astype(BF16)

    def v_stage():
        vt_ref[...] = seg(2).T.astype(BF16)

    def hq_stage():
        qr = seg(3)
        hq_ref[...] = (qr * _sigmoid(qr)).astype(BF16)

    def hf_stage():
        hf_ref[...] = seg(4)

    def hi_stage():
        hi_ref[...] = seg(5).astype(BF16)

    def hg_stage():
        gr = seg(6)
        hg_ref[...] = (gr * _sigmoid(gr)).astype(BF16)

    return [q_stage, k_stage, v_stage, hq_stage, hf_stage, hi_stage, hg_stage]


N_PROJ_IN = 5
N_PROJ_OUT = 7


def _in_proj_kernel(*refs, n_cast):
    cast_in = refs[N_PROJ_IN:N_PROJ_IN + n_cast]
    outs = refs[N_PROJ_IN + n_cast:N_PROJ_IN + n_cast + N_PROJ_OUT]
    cast_out = refs[N_PROJ_IN + n_cast + N_PROJ_OUT:]
    for stage in _in_proj_stages(*refs[:N_PROJ_IN], *outs):
        stage()
    for src, dst in zip(cast_in, cast_out):
        dst[...] = src[...].astype(BF16)


def _in_proj(x, norm_g, w_in, qg, kg, tm, convert):
    n, d = x.shape
    steps = n // tm
    row = lambda i: (i, 0)
    const2 = lambda i: (0, 0)
    out_bf = jax.ShapeDtypeStruct((n, SEG), BF16)
    out_f32 = jax.ShapeDtypeStruct((n, SEG), F32)
    seg_spec = pl.BlockSpec((tm, SEG), row)
    per_blk = KEY_BLOCK // tm
    cast_in_specs, cast_out_specs, cast_out_shapes = [], [], []
    for w, layer in convert:
        _, rows, cols = w.shape
        slabs = max(k for k in range(1, steps + 1)
                    if rows % k == 0 and (rows // k) % BF16_SUBLANES == 0)
        slab = lambda i, last=slabs - 1: jnp.minimum(i, last)
        cast_in_specs.append(pl.BlockSpec((None, rows // slabs, cols),
                                          lambda i, slab=slab, layer=layer: (layer, slab(i), 0)))
        cast_out_specs.append(pl.BlockSpec((rows // slabs, cols),
                                           lambda i, slab=slab: (slab(i), 0)))
        cast_out_shapes.append(jax.ShapeDtypeStruct((rows, cols), BF16))
    outs = pl.pallas_call(
        functools.partial(_in_proj_kernel, n_cast=len(convert)),
        grid=(steps,),
        in_specs=[
            pl.BlockSpec((tm, d), row),
            _resident((1, d), const2),
            _resident((d, N_SEG * SEG), const2),
            _resident((HEAD_LANES, 1), const2),
            _resident((HEAD_LANES, 1), const2),
        ] + cast_in_specs,
        out_specs=[pl.BlockSpec((SEG, tm), lambda i: (0, i)), seg_spec,
                   pl.BlockSpec((None, SEG, tm), lambda i: (i // per_blk, 0, i % per_blk)),
                   seg_spec, seg_spec, seg_spec, seg_spec] + cast_out_specs,
        out_shape=[jax.ShapeDtypeStruct((SEG, n), BF16), out_bf,
                   jax.ShapeDtypeStruct((n // KEY_BLOCK, SEG, KEY_BLOCK), BF16),
                   out_bf, out_f32, out_bf, out_bf] + cast_out_shapes,
        compiler_params=pltpu.CompilerParams(
            dimension_semantics=("arbitrary",), vmem_limit_bytes=VMEM_LIMIT_BYTES),
        name="in_proj",
    )(x, norm_g, w_in, qg, kg, *[w for w, _ in convert])
    return outs[:N_PROJ_OUT], outs[N_PROJ_OUT:]


def _lambda_full(lq1_ref, lk1_ref, lq2_ref, lk2_ref, lam_init):
    return (jnp.exp(jnp.sum(lq1_ref[...] * lk1_ref[...], axis=-1, keepdims=True))
            - jnp.exp(jnp.sum(lq2_ref[...] * lk2_ref[...], axis=-1, keepdims=True))
            + lam_init)


def _split_maps_t(qt):
    row = lax.broadcasted_iota(jnp.int32, qt.shape, 0)
    zero = jnp.zeros_like(qt)
    return jnp.concatenate([jnp.where(row < DIFF_HEAD_DIM, qt, zero),
                            jnp.where(row >= DIFF_HEAD_DIM, qt, zero)], axis=1)


def _attn_bounded_kernel(qt_ref, k_ref, vt_ref, lq1_ref, lk1_ref, lq2_ref, lk2_ref, sgc_ref,
                         o_ref, acc_ref, *, lam_init):
    blk = pl.program_id(2)
    n_sub = KEY_BLOCK // ATTN_SUB
    ones = jnp.ones((ONES_ROWS, KEY_BLOCK), BF16)

    def q2t(s):
        return _split_maps_t(qt_ref[:, s * ATTN_SUB:(s + 1) * ATTN_SUB])

    def attend(s, k_rows, vt, diagonal):
        n_keys = k_rows.shape[0]
        st = jnp.dot(k_rows, q2t(s), preferred_element_type=F32)
        p = jnp.exp2(st)
        if diagonal:
            tail = p[n_keys - ATTN_SUB:]
            krow = lax.broadcasted_iota(jnp.int32, tail.shape, 0)
            qcol = lax.broadcasted_iota(jnp.int32, tail.shape, 1)
            visible = (krow >> CHUNK_SHIFT) <= ((qcol & (ATTN_SUB - 1)) >> CHUNK_SHIFT)
            tail = jnp.where(visible, tail, 0.0)
            p = tail if n_keys == ATTN_SUB else jnp.concatenate([p[:n_keys - ATTN_SUB], tail], axis=0)
        vte = jnp.concatenate([vt, ones[:, :n_keys]], axis=0)
        return jnp.dot(vte, p.astype(BF16), preferred_element_type=F32)

    start = pl.multiple_of(blk * KEY_BLOCK, KEY_BLOCK)
    for s in range(n_sub):
        n_keys = (s + 1) * ATTN_SUB
        acc_ref[s] = attend(s, k_ref[pl.ds(start, n_keys), :], vt_ref[blk, :, :n_keys], True)

    def body(j, carry):
        k_rows = k_ref[pl.ds(pl.multiple_of(j * KEY_BLOCK, KEY_BLOCK), KEY_BLOCK), :]
        vt = vt_ref[j]
        for s in range(n_sub):
            acc_ref[s] += attend(s, k_rows, vt, False)
        return carry

    lax.fori_loop(0, blk, body, 0)

    lam = _lambda_full(lq1_ref, lk1_ref, lq2_ref, lk2_ref, lam_init)
    for s in range(n_sub):
        a = acc_ref[s]
        o = a[:HEAD_LANES] * (1.0 / a[HEAD_LANES:HEAD_LANES + 1])
        od = o[:, :ATTN_SUB] - lam * o[:, ATTN_SUB:]
        ms = jnp.mean(od * od, axis=0, keepdims=True)
        od = od * lax.rsqrt(ms + NORM_EPS) * sgc_ref[...] * (1.0 - lam_init)
        o_ref[s * ATTN_SUB:(s + 1) * ATTN_SUB, :] = od.T.astype(BF16)


def _attn_online_kernel(qt_ref, k_ref, vt_ref, lq1_ref, lk1_ref, lq2_ref, lk2_ref, sgc_ref,
                        o_ref, v_ref, *, tq, lam_init):
    i = pl.program_id(2)

    @pl.when(i == 0)
    def _():
        for j in range(vt_ref.shape[0]):
            v_ref[j * KEY_BLOCK:(j + 1) * KEY_BLOCK, :] = vt_ref[j].astype(F32).T.astype(BF16)

    q = qt_ref[...].astype(F32).T.astype(BF16)
    lane = lax.broadcasted_iota(jnp.int32, q.shape, 1)
    zero = jnp.zeros_like(q)
    q2 = jnp.concatenate([jnp.where(lane < DIFF_HEAD_DIM, q, zero),
                          jnp.where(lane >= DIFF_HEAD_DIM, q, zero)], axis=0)

    def scores(kj):
        return lax.dot_general(q2, kj, _NT, preferred_element_type=F32)

    start = pl.multiple_of(i * tq, tq)
    s = scores(k_ref[pl.ds(start, tq), :])
    row = lax.broadcasted_iota(jnp.int32, s.shape, 0)
    col = lax.broadcasted_iota(jnp.int32, s.shape, 1)
    visible = (col >> CHUNK_SHIFT) <= ((row & (tq - 1)) >> CHUNK_SHIFT)
    s = jnp.where(visible, s, -jnp.inf)
    m = jnp.max(s, axis=-1, keepdims=True)
    p = jnp.exp2(s - m)
    l = jnp.sum(p, axis=-1, keepdims=True)
    acc = jnp.dot(p.astype(BF16), v_ref[pl.ds(start, tq), :], preferred_element_type=F32)

    def body(j, carry):
        m, l, acc = carry
        off = pl.multiple_of(j * tq, tq)
        s = scores(k_ref[pl.ds(off, tq), :])
        m_new = jnp.maximum(m, jnp.max(s, axis=-1, keepdims=True))
        alpha = jnp.exp2(m - m_new)
        p = jnp.exp2(s - m_new)
        l = alpha * l + jnp.sum(p, axis=-1, keepdims=True)
        acc = alpha * acc + jnp.dot(p.astype(BF16), v_ref[pl.ds(off, tq), :],
                                    preferred_element_type=F32)
        return m_new, l, acc

    m, l, acc = lax.fori_loop(0, i, body, (m, l, acc))
    o = acc / l
    lam = _lambda_full(lq1_ref, lk1_ref, lq2_ref, lk2_ref, lam_init)
    od = o[:tq] - lam * o[tq:]
    od = od * _rsqrt_mean_sq(od) * sgc_ref[...].T * (1.0 - lam_init)
    o_ref[...] = od.astype(BF16)


def _diff_attn(qt, k, vt, lq1, lk1, lq2, lk2, subln_g_col, lam_init, batch, score_bound):
    n = k.shape[0]
    s = n // batch
    blocks = s // KEY_BLOCK
    lam_spec = _resident((1, DIFF_HEAD_DIM), lambda bi, hi, qi: (0, 0))
    k_spec = pl.BlockSpec((s, HEAD_LANES), lambda bi, hi, qi: (bi, hi))
    vt_spec = pl.BlockSpec((blocks, HEAD_LANES, KEY_BLOCK), lambda bi, hi, qi: (bi, hi, 0))
    gain_spec = _resident((HEAD_LANES, 1), lambda bi, hi, qi: (0, 0))

    def call(body, tq, scratch, name):
        per_batch = s // tq
        return pl.pallas_call(
            body,
            grid=(batch, N_HEADS, per_batch),
            in_specs=[pl.BlockSpec((HEAD_LANES, tq), lambda bi, hi, qi: (hi, bi * per_batch + qi)),
                      k_spec, vt_spec, lam_spec, lam_spec, lam_spec, lam_spec, gain_spec],
            out_specs=pl.BlockSpec((tq, HEAD_LANES), lambda bi, hi, qi: (bi * per_batch + qi, hi)),
            out_shape=jax.ShapeDtypeStruct((n, SEG), BF16),
            scratch_shapes=scratch,
            compiler_params=pltpu.CompilerParams(
                dimension_semantics=("arbitrary", "arbitrary", "arbitrary"),
                vmem_limit_bytes=VMEM_LIMIT_BYTES),
            name=name,
        )

    bounded = call(
        functools.partial(_attn_bounded_kernel, lam_init=lam_init), KEY_BLOCK,
        [pltpu.VMEM((KEY_BLOCK // ATTN_SUB, HEAD_LANES + ONES_ROWS, 2 * ATTN_SUB), F32)],
        "diff_attn_bounded")
    online = call(
        functools.partial(_attn_online_kernel, tq=ATTN_SUB, lam_init=lam_init), ATTN_SUB,
        [pltpu.VMEM((s, HEAD_LANES), BF16)], "diff_attn_online")
    return lax.cond(score_bound <= SCORE_BOUND_MAX, bounded, online,
                    qt, k, vt, lq1, lk1, lq2, lk2, subln_g_col)


def _hgrn_tables():
    c = CHUNK
    masks = np.zeros((N_LEVELS + 1, c, c), np.float32)
    masks[0] = np.eye(c, dtype=np.float32)
    for lev in range(1, N_LEVELS + 1):
        size = 1 << lev
        half = size // 2
        for r in range(c):
            if r % size >= half:
                mid = r - r % size + half
                masks[lev, r, mid - half:mid] = 1.0
    t_idx, s_idx = np.arange(c)[:, None], np.arange(c)[None, :]
    in_block = ((t_idx // REF_BLOCK == s_idx // REF_BLOCK) & (s_idx <= t_idx)).astype(np.float32)
    masks = np.concatenate([masks, in_block[None]], axis=0)
    padded = np.zeros((masks.shape[0], 2, c, 2 * c), np.float32)
    padded[:, 0, :, :c] = masks
    padded[:, 1, :, c:] = masks
    return np.tril(np.ones((c, c), np.float32)), padded


def _level_ref_row(lev, row):
    size = 1 << lev
    return row - row % size + size // 2 - 1


def _hgrn_kernel(hq_ref, hf_ref, hi_ref, hg_ref, lb_ref, g_ref, tril_ref, masks_ref,
                 o_ref, state_ref, b_ref, k_ref, lf_ref, *, layer, n_chunks):
    @pl.when(pl.program_id(1) == 0)
    def _():
        state_ref[...] = jnp.zeros_like(state_ref)

    if layer > 0:
        lbp = lb_ref[...]
        ex = jnp.exp(lbp - jnp.max(lbp, axis=0, keepdims=True))
        gamma = ex / jnp.sum(ex, axis=0, keepdims=True)
        lb = jnp.sum(gamma[1:layer + 1], axis=0, keepdims=True)

    tn = (((0,), (0,)), ((), ()))
    groups = CHUNK // SUBLANES
    sub = lax.broadcasted_iota(jnp.int32, (SUBLANES, SEG), 0)
    up_small = {lev: (sub & (1 << (lev - 1))) != 0 for lev in (1, 2, 3)}
    sgn_small = {lev: jnp.where(up_small[lev], 1.0, -1.0) for lev in (2, 3)}
    upper_groups = {lev: [g for g in range(groups)
                          if (g * SUBLANES) % (1 << lev) >= (1 << lev) // 2]
                    for lev in range(4, N_LEVELS + 1)}

    def grp(a, g):
        return a[g * SUBLANES:(g + 1) * SUBLANES, :]

    def stack(parts):
        return jnp.concatenate(parts, axis=0).astype(BF16)

    def heads_to_rows(a):
        return jnp.concatenate(
            [a[:, h * HEAD_LANES:(h + 1) * HEAD_LANES] for h in range(N_HEADS)], axis=0)

    worst = None
    for c in range(n_chunks):
        rows = slice(c * CHUNK, (c + 1) * CHUNK)
        z = hf_ref[rows, :]
        e = jnp.exp(-jnp.abs(z))
        r = 1.0 / (1.0 + e)
        sig_neg = jnp.where(z >= 0, e * r, r)
        if layer == 0:
            lf2 = jnp.minimum(z * LOG2E, 0.0) - jnp.log2(1.0 + e)
            kk = sig_neg
        else:
            sig_pos = jnp.where(z >= 0, r, e * r)
            lf2 = jnp.log2(lb + (1.0 - lb) * sig_pos)
            kk = (1.0 - lb) * sig_neg
        hi = lf2.astype(BF16)
        lo = (lf2 - hi.astype(F32)).astype(BF16)
        b2 = (jnp.dot(tril_ref[...], hi, preferred_element_type=F32)
              + jnp.dot(tril_ref[...], lo, preferred_element_type=F32))
        b_ref[c] = b2
        k_ref[c] = kk
        lf_ref[c] = lf2
        for blk in range(CHUNK // REF_BLOCK):
            last = (blk + 1) * REF_BLOCK - 1
            span = -b2[last:last + 1, :]
            if blk:
                span = span + b2[blk * REF_BLOCK - 1:blk * REF_BLOCK, :]
            worst = span if worst is None else jnp.maximum(worst, span)
    moderate = jnp.max(worst) <= REF_BLOCK_DECAY_MAX

    def chunk(c, single_reference):
        rows = slice(c * CHUNK, (c + 1) * CHUNK)
        b2 = b_ref[c]
        kk = k_ref[c]
        qq = hq_ref[rows, :].astype(F32)

        def ref_rows(row):
            return jnp.broadcast_to(b_ref[c, row:row + 1, :], (SUBLANES, SEG))

        x = {lev: [] for lev in range(1, N_LEVELS + 1)}
        xq_blk, xk_blk, q_pre, k_suf = [], [], [], []
        for g in range(groups):
            b_g, q_g, k_g = grp(b2, g), grp(qq, g), grp(kk, g)
            row0 = g * SUBLANES
            if single_reference:
                start = row0 - row0 % REF_BLOCK
                d = b_g - ref_rows(start - 1) if start else b_g
                xq_blk.append(q_g * jnp.exp2(d))
                xk_blk.append(k_g * jnp.exp2(-d))
                first_level = REF_BLOCK_LEVEL + 1
            else:
                x[1].append(jnp.where(up_small[1], q_g * jnp.exp2(grp(lf_ref[c], g)), k_g))
                ref2 = jnp.where(sub < 4, ref_rows(_level_ref_row(2, row0)),
                                 ref_rows(_level_ref_row(2, row0 + 4)))
                x[2].append(jnp.where(up_small[2], q_g, k_g)
                            * jnp.exp2((b_g - ref2) * sgn_small[2]))
                ref3 = ref_rows(_level_ref_row(3, row0))
                x[3].append(jnp.where(up_small[3], q_g, k_g)
                            * jnp.exp2((b_g - ref3) * sgn_small[3]))
                first_level = 4
            for lev in range(first_level, N_LEVELS + 1):
                ref = ref_rows(_level_ref_row(lev, row0))
                if row0 % (1 << lev) >= (1 << lev) // 2:
                    x[lev].append(q_g * jnp.exp2(b_g - ref))
                else:
                    x[lev].append(k_g * jnp.exp2(ref - b_g))
            q_pre.append(q_g * jnp.exp2(b_g))
            k_suf.append(k_g * jnp.exp2(ref_rows(CHUNK - 1) - b_g))

        all_groups = list(range(groups))
        if single_reference:
            terms = [(MASK_REF_BLOCK, all_groups, xq_blk, xk_blk)]
        else:
            terms = [(lev, all_groups, x[lev], x[lev]) for lev in (1, 2, 3)]
        for lev in range(first_level, N_LEVELS + 1):
            terms.append((lev, upper_groups[lev], [x[lev][g] for g in upper_groups[lev]], x[lev]))

        scores = [lax.dot_general(heads_to_rows(stack(q_rows)), heads_to_rows(stack(k_rows)), _NT,
                                  preferred_element_type=F32)
                  for _, _, q_rows, k_rows in terms]

        pair_blocks = []
        for pair in range(N_HEADS // 2):
            lanes = slice(pair * HEAD_LANES, (pair + 1) * HEAD_LANES)
            a_rows = []
            for h in (2 * pair, 2 * pair + 1):
                for g in range(groups):
                    gs = slice(g * SUBLANES, (g + 1) * SUBLANES)
                    a_g = None
                    if not single_reference:
                        hs = slice(h * HEAD_LANES, (h + 1) * HEAD_LANES)
                        a_g = (jnp.sum(qq[gs, hs] * kk[gs, hs], axis=-1, keepdims=True)
                               * masks_ref[0, h % 2, gs, :])
                    for (mask_id, q_groups, _, _), sc in zip(terms, scores):
                        if g not in q_groups:
                            continue
                        r0 = (h * len(q_groups) + q_groups.index(g)) * SUBLANES
                        part = sc[r0:r0 + SUBLANES, lanes] * masks_ref[mask_id, h % 2, gs, :]
                        a_g = part if a_g is None else a_g + part
                    a_rows.append(a_g)
            pair_blocks.append(jnp.concatenate(a_rows, axis=0))
        zero = jnp.zeros_like(pair_blocks[0])
        a_full = jnp.concatenate(
            [jnp.concatenate([pair_blocks[0], zero], axis=1),
             jnp.concatenate([zero, pair_blocks[1]], axis=1)], axis=0).astype(BF16)
        o_intra = jnp.dot(a_full, heads_to_rows(hi_ref[rows, :]), preferred_element_type=F32)

        q_pre = stack(q_pre)
        k_suf = stack(k_suf)
        decay_last = jnp.exp2(b2[CHUNK - 1:CHUNK, :])
        for h in range(N_HEADS):
            hs = slice(h * HEAD_LANES, (h + 1) * HEAD_LANES)
            st = state_ref[h]
            o = (o_intra[h * CHUNK:(h + 1) * CHUNK, :]
                 + lax.dot_general(q_pre[:, hs], st.astype(BF16), _NT,
                                   preferred_element_type=F32))
            state_ref[h] = (st * decay_last[:, hs]
                            + lax.dot_general(hi_ref[rows, hs], k_suf[:, hs], tn,
                                              preferred_element_type=F32))
            o = o * _rsqrt_mean_sq(o) * g_ref[...] * hg_ref[rows, hs].astype(F32)
            o_ref[rows, hs] = o.astype(BF16)

    @pl.when(moderate)
    def _():
        for c in range(n_chunks):
            chunk(c, True)

    @pl.when(jnp.logical_not(moderate))
    def _():
        for c in range(n_chunks):
            chunk(c, False)


def _hgrn(hq, hf, hi, hg, lower_bounds, norm_g, tables, layer, tc, batch):
    n = hq.shape[0]
    tril, masks = tables
    per_batch = n // batch // tc
    tile = pl.BlockSpec((tc, SEG), lambda bi, ci: (bi * per_batch + ci, 0))
    const2 = lambda bi, ci: (0, 0)
    return pl.pallas_call(
        functools.partial(_hgrn_kernel, layer=layer, n_chunks=tc // CHUNK),
        grid=(batch, per_batch),
        in_specs=[tile, tile, tile, tile,
                  _resident(lower_bounds.shape, const2),
                  _resident((1, HEAD_LANES), const2),
                  _resident(tril.shape, const2),
                  _resident(masks.shape, lambda bi, ci: (0, 0, 0, 0))],
        out_specs=tile,
        out_shape=jax.ShapeDtypeStruct((n, SEG), BF16),
        scratch_shapes=[pltpu.VMEM((N_HEADS, HEAD_LANES, HEAD_LANES), F32)]
        + [pltpu.VMEM((tc // CHUNK, CHUNK, SEG), F32)] * 3,
        compiler_params=pltpu.CompilerParams(
            dimension_semantics=("arbitrary", "arbitrary"),
            vmem_limit_bytes=VMEM_LIMIT_BYTES),
        name="hgrn2",
    )(hq, hf, hi, hg, lower_bounds, norm_g, tril, masks)


def _out_ffn_kernel(x_ref, od_ref, or_ref, wo_ref, g_ref, wg_ref, wu_ref, wd_ref,
                    o_ref, ff_ref, *, d_ff):
    xn = (x_ref[...]
          + jnp.dot(od_ref[...], wo_ref[:SEG, :], preferred_element_type=F32)
          + jnp.dot(or_ref[...], wo_ref[SEG:, :], preferred_element_type=F32))
    h = (xn * _rsqrt_mean_sq(xn) * g_ref[...]).astype(BF16)
    for j in range(d_ff // FFN_TILE):
        cols = slice(j * FFN_TILE, (j + 1) * FFN_TILE)
        gate = jnp.dot(h, wg_ref[:, cols], preferred_element_type=F32)
        up = jnp.dot(h, wu_ref[:, cols], preferred_element_type=F32)
        ff_ref[:, cols] = (gate * _sigmoid(gate) * up).astype(BF16)
    o_ref[...] = xn + jnp.dot(ff_ref[...], wd_ref[...], preferred_element_type=F32)


def _out_ffn(x, od, orr, w_out, norm_g, w_gate, w_up, w_down, tm):
    n, d = x.shape
    d_ff = w_gate.shape[-1]
    row = lambda i: (i, 0)
    const2 = lambda i: (0, 0)
    return pl.pallas_call(
        functools.partial(_out_ffn_kernel, d_ff=d_ff),
        grid=(n // tm,),
        in_specs=[
            pl.BlockSpec((tm, d), row),
            pl.BlockSpec((tm, SEG), row),
            pl.BlockSpec((tm, SEG), row),
            _resident((d, d), const2),
            _resident((1, d), const2),
            _resident((d, d_ff), const2),
            _resident((d, d_ff), const2),
            _resident((d_ff, d), const2),
        ],
        out_specs=pl.BlockSpec((tm, d), row),
        out_shape=jax.ShapeDtypeStruct((n, d), F32),
        scratch_shapes=[pltpu.VMEM((tm, d_ff), BF16)],
        compiler_params=pltpu.CompilerParams(
            dimension_semantics=("arbitrary",), vmem_limit_bytes=VMEM_LIMIT_BYTES),
        name="out_ffn",
    )(x, od, orr, w_out, norm_g, w_gate, w_up, w_down)


def _tile(n, want):
    return min(n, want)


def kernel(x, attn_norm_g, w_in, q_norm_g, k_norm_g, lambda_q1, lambda_k1, lambda_q2, lambda_k2, subln_g, lower_bounds, hgrn_norm_g, w_out, ffn_norm_g, w_gate, w_up, w_down):
    b, s, d = x.shape
    depth = w_in.shape[0]
    n = b * s
    assert d == 2 * SEG and w_in.shape[-1] == N_SEG * SEG and s % KEY_BLOCK == 0

    w_in_b = w_in[0].astype(BF16)

    tril, masks = _hgrn_tables()
    tables = (jnp.asarray(tril, BF16), jnp.asarray(masks))
    lower_bounds = lower_bounds.astype(F32)

    tm_proj = _tile(s, PROJ_ROWS)
    tm_hgrn = _tile(s, HGRN_ROWS)
    tm_ffn = _tile(s, FFN_ROWS)

    xf = x.reshape(n, d)
    for l in range(depth):
        lam_init = 0.8 - 0.6 * math.exp(-0.3 * l)
        qg = jnp.tile(q_norm_g[l], 2)[:, None] * (DIFF_HEAD_DIM ** -0.5 * LOG2E)
        kg = jnp.tile(k_norm_g[l], 2)[:, None]
        score_bound = (1.02 * math.sqrt(DIFF_HEAD_DIM)
                       * jnp.max(jnp.abs(q_norm_g[l])) * jnp.max(jnp.abs(k_norm_g[l])))
        convert = [(w_out, l), (w_gate, l), (w_up, l), (w_down, l)]
        if l + 1 < depth:
            convert.append((w_in, l + 1))
        (qt, k, vt, hq, hf, hi, hg), converted = _in_proj(
            xf, attn_norm_g[l][None], w_in_b, qg, kg, tm_proj, convert)
        w_out_b, w_gate_b, w_up_b, w_down_b = converted[:4]
        w_in_b = converted[4] if l + 1 < depth else None
        orr = _hgrn(hq, hf, hi, hg, lower_bounds, hgrn_norm_g[l][None], tables, l, tm_hgrn, b)
        od = _diff_attn(qt, k, vt, lambda_q1[l][None], lambda_k1[l][None],
                        lambda_q2[l][None], lambda_k2[l][None], subln_g[l][:, None], lam_init, b,
                        score_bound)
        xf = _out_ffn(xf, od, orr, w_out_b, ffn_norm_g[l][None],
                      w_gate_b, w_up_b, w_down_b, tm_ffn)
    return xf.reshape(b, s, d)
```

```python
import functools
import math

import numpy as np
import jax
import jax.numpy as jnp
from jax import lax
from jax.experimental import pallas as pl
from jax.experimental.pallas import tpu as pltpu

F32 = jnp.float32
BF16 = jnp.bfloat16

NORM_EPS = 1e-6
CHUNK = 64
CHUNK_SHIFT = 6
HEAD_LANES = 128
N_HEADS = 4
SEG = N_HEADS * HEAD_LANES
DIFF_HEAD_DIM = 64
N_SEG = 7
N_LEVELS = CHUNK_SHIFT

VMEM_LIMIT_BYTES = 52 * 1024 * 1024
FFN_TILE = 256
PROJ_ROWS = 1024
HGRN_ROWS = 1024
FFN_ROWS = 1024
KEY_BLOCK = 4096
ATTN_SUB = 256
ONES_ROWS = 16
SCORE_BOUND_MAX = 40.0
_NT = (((1,), (1,)), ((), ()))
SUBLANES = 8
BF16_SUBLANES = 16
LOG2E = math.log2(math.e)
REF_BLOCK_LEVEL = 4
REF_BLOCK = 1 << REF_BLOCK_LEVEL
MASK_REF_BLOCK = N_LEVELS + 1
REF_BLOCK_DECAY_MAX = 60.0


def _rsqrt_mean_sq(y):
    return lax.rsqrt(jnp.mean(y * y, axis=-1, keepdims=True) + NORM_EPS)


def _sigmoid(z):
    return 1.0 / (1.0 + jnp.exp(-z))


def _resident(block_shape, index_map):
    return pl.BlockSpec(block_shape, index_map, pipeline_mode=pl.Buffered(1))


def _in_proj_stages(layer, x_ref, g_ref, w_ref, qg_ref, kg_ref, lb_ref,
                    qt_ref, k_ref, vt_ref, hq_ref, hf_ref, hk_ref, hi_ref, hg_ref):
    x = x_ref[...]
    h = (x * _rsqrt_mean_sq(x) * g_ref[...]).astype(BF16)

    def seg(s):
        return jnp.dot(h, w_ref[:, s * SEG:(s + 1) * SEG], preferred_element_type=F32)

    def head_norm_t(y, gain_ref, c):
        t = y[:, c * HEAD_LANES:(c + 1) * HEAD_LANES].T
        halves = []
        for m in range(HEAD_LANES // DIFF_HEAD_DIM):
            tm_ = t[m * DIFF_HEAD_DIM:(m + 1) * DIFF_HEAD_DIM]
            ms = jnp.mean(tm_ * tm_, axis=0, keepdims=True)
            halves.append(tm_ * lax.rsqrt(ms + NORM_EPS))
        return jnp.concatenate(halves, axis=0) * gain_ref[...]

    def q_stage():
        yq = seg(0)
        for c in range(N_HEADS):
            lanes = slice(c * HEAD_LANES, (c + 1) * HEAD_LANES)
            qt_ref[lanes, :] = head_norm_t(yq, qg_ref, c).astype(BF16)

    def k_stage():
        yk = seg(1)
        for c in range(N_HEADS):
            lanes = slice(c * HEAD_LANES, (c + 1) * HEAD_LANES)
            k_ref[:, lanes] = head_norm_t(yk, kg_ref, c).T.astype(BF16)

    def v_stage():
        vt_ref[...] = seg(2).T.astype(BF16)

    def hq_stage():
        qr = seg(3)
        hq_ref[...] = (qr * _sigmoid(qr)).astype(BF16)

    def hf_stage():
        z = seg(4)
        e = jnp.exp(-jnp.abs(z))
        r = 1.0 / (1.0 + e)
        sig_neg = jnp.where(z >= 0, e * r, r)
        if layer == 0:
            hf_ref[...] = jnp.minimum(z * LOG2E, 0.0) - jnp.log2(1.0 + e)
            hk_ref[...] = sig_neg
        else:
            lbp = lb_ref[...]
            ex = jnp.exp(lbp - jnp.max(lbp, axis=0, keepdims=True))
            gamma = ex / jnp.sum(ex, axis=0, keepdims=True)
            lb = jnp.sum(gamma[1:layer + 1], axis=0, keepdims=True)
            sig_pos = jnp.where(z >= 0, r, e * r)
            hf_ref[...] = jnp.log2(lb + (1.0 - lb) * sig_pos)
            hk_ref[...] = (1.0 - lb) * sig_neg

    def hi_stage():
        hi_ref[...] = seg(5).astype(BF16)

    def hg_stage():
        gr = seg(6)
        hg_ref[...] = (gr * _sigmoid(gr)).astype(BF16)

    return [q_stage, k_stage, v_stage, hq_stage, hf_stage, hi_stage, hg_stage]


N_PROJ_IN = 6
N_PROJ_OUT = 8


def _in_proj_kernel(*refs, n_cast, layer):
    cast_in = refs[N_PROJ_IN:N_PROJ_IN + n_cast]
    outs = refs[N_PROJ_IN + n_cast:N_PROJ_IN + n_cast + N_PROJ_OUT]
    cast_out = refs[N_PROJ_IN + n_cast + N_PROJ_OUT:]
    for stage in _in_proj_stages(layer, *refs[:N_PROJ_IN], *outs):
        stage()
    for src, dst in zip(cast_in, cast_out):
        dst[...] = src[...].astype(BF16)


def _in_proj(x, norm_g, w_in, qg, kg, lower_bounds, layer, tm, convert):
    n, d = x.shape
    steps = n // tm
    row = lambda i: (i, 0)
    const2 = lambda i: (0, 0)
    out_bf = jax.ShapeDtypeStruct((n, SEG), BF16)
    out_f32 = jax.ShapeDtypeStruct((n, SEG), F32)
    seg_spec = pl.BlockSpec((tm, SEG), row)
    per_blk = KEY_BLOCK // tm
    cast_in_specs, cast_out_specs, cast_out_shapes = [], [], []
    for w, w_layer in convert:
        _, rows, cols = w.shape
        slabs = max(k for k in range(1, steps + 1)
                    if rows % k == 0 and (rows // k) % BF16_SUBLANES == 0)
        slab = lambda i, last=slabs - 1: jnp.minimum(i, last)
        cast_in_specs.append(pl.BlockSpec((None, rows // slabs, cols),
                                          lambda i, slab=slab, w_layer=w_layer: (w_layer, slab(i), 0)))
        cast_out_specs.append(pl.BlockSpec((rows // slabs, cols),
                                           lambda i, slab=slab: (slab(i), 0)))
        cast_out_shapes.append(jax.ShapeDtypeStruct((rows, cols), BF16))
    outs = pl.pallas_call(
        functools.partial(_in_proj_kernel, n_cast=len(convert), layer=layer),
        grid=(steps,),
        in_specs=[
            pl.BlockSpec((tm, d), row),
            _resident((1, d), const2),
            _resident((d, N_SEG * SEG), const2),
            _resident((HEAD_LANES, 1), const2),
            _resident((HEAD_LANES, 1), const2),
            _resident(lower_bounds.shape, const2),
        ] + cast_in_specs,
        out_specs=[pl.BlockSpec((SEG, tm), lambda i: (0, i)), seg_spec,
                   pl.BlockSpec((None, SEG, tm), lambda i: (i // per_blk, 0, i % per_blk)),
                   seg_spec, seg_spec, seg_spec, seg_spec, seg_spec] + cast_out_specs,
        out_shape=[jax.ShapeDtypeStruct((SEG, n), BF16), out_bf,
                   jax.ShapeDtypeStruct((n // KEY_BLOCK, SEG, KEY_BLOCK), BF16),
                   out_bf, out_f32, out_f32, out_bf, out_bf] + cast_out_shapes,
        compiler_params=pltpu.CompilerParams(
            dimension_semantics=("arbitrary",), vmem_limit_bytes=VMEM_LIMIT_BYTES),
        name="in_proj",
    )(x, norm_g, w_in, qg, kg, lower_bounds, *[w for w, _ in convert])
    return outs[:N_PROJ_OUT], outs[N_PROJ_OUT:]


def _lambda_full(lq1_ref, lk1_ref, lq2_ref, lk2_ref, lam_init):
    return (jnp.exp(jnp.sum(lq1_ref[...] * lk1_ref[...], axis=-1, keepdims=True))
            - jnp.exp(jnp.sum(lq2_ref[...] * lk2_ref[...], axis=-1, keepdims=True))
            + lam_init)


def _split_maps_t(qt):
    row = lax.broadcasted_iota(jnp.int32, qt.shape, 0)
    zero = jnp.zeros_like(qt)
    return jnp.concatenate([jnp.where(row < DIFF_HEAD_DIM, qt, zero),
                            jnp.where(row >= DIFF_HEAD_DIM, qt, zero)], axis=1)


def _attn_bounded_kernel(qt_ref, k_ref, vt_ref, lq1_ref, lk1_ref, lq2_ref, lk2_ref, sgc_ref,
                         o_ref, acc_ref, *, lam_init):
    blk = pl.program_id(2)
    n_sub = KEY_BLOCK // ATTN_SUB
    ones = jnp.ones((ONES_ROWS, KEY_BLOCK), BF16)

    def q2t(s):
        return _split_maps_t(qt_ref[:, s * ATTN_SUB:(s + 1) * ATTN_SUB])

    def attend(s, k_rows, vt, diagonal):
        n_keys = k_rows.shape[0]
        st = jnp.dot(k_rows, q2t(s), preferred_element_type=F32)
        p = jnp.exp2(st)
        if diagonal:
            tail = p[n_keys - ATTN_SUB:]
            krow = lax.broadcasted_iota(jnp.int32, tail.shape, 0)
            qcol = lax.broadcasted_iota(jnp.int32, tail.shape, 1)
            visible = (krow >> CHUNK_SHIFT) <= ((qcol & (ATTN_SUB - 1)) >> CHUNK_SHIFT)
            tail = jnp.where(visible, tail, 0.0)
            p = tail if n_keys == ATTN_SUB else jnp.concatenate([p[:n_keys - ATTN_SUB], tail], axis=0)
        vte = jnp.concatenate([vt, ones[:, :n_keys]], axis=0)
        return jnp.dot(vte, p.astype(BF16), preferred_element_type=F32)

    start = pl.multiple_of(blk * KEY_BLOCK, KEY_BLOCK)
    for s in range(n_sub):
        n_keys = (s + 1) * ATTN_SUB
        acc_ref[s] = attend(s, k_ref[pl.ds(start, n_keys), :], vt_ref[blk, :, :n_keys], True)

    def body(j, carry):
        k_rows = k_ref[pl.ds(pl.multiple_of(j * KEY_BLOCK, KEY_BLOCK), KEY_BLOCK), :]
        vt = vt_ref[j]
        for s in range(n_sub):
            acc_ref[s] += attend(s, k_rows, vt, False)
        return carry

    lax.fori_loop(0, blk, body, 0)

    lam = _lambda_full(lq1_ref, lk1_ref, lq2_ref, lk2_ref, lam_init)
    for s in range(n_sub):
        a = acc_ref[s]
        o = a[:HEAD_LANES] * (1.0 / a[HEAD_LANES:HEAD_LANES + 1])
        od = o[:, :ATTN_SUB] - lam * o[:, ATTN_SUB:]
        ms = jnp.mean(od * od, axis=0, keepdims=True)
        od = od * lax.rsqrt(ms + NORM_EPS) * sgc_ref[...] * (1.0 - lam_init)
        o_ref[s * ATTN_SUB:(s + 1) * ATTN_SUB, :] = od.T.astype(BF16)


def _attn_online_kernel(qt_ref, k_ref, vt_ref, lq1_ref, lk1_ref, lq2_ref, lk2_ref, sgc_ref,
                        o_ref, v_ref, *, tq, lam_init):
    i = pl.program_id(2)

    @pl.when(i == 0)
    def _():
        for j in range(vt_ref.shape[0]):
            v_ref[j * KEY_BLOCK:(j + 1) * KEY_BLOCK, :] = vt_ref[j].astype(F32).T.astype(BF16)

    q = qt_ref[...].astype(F32).T.astype(BF16)
    lane = lax.broadcasted_iota(jnp.int32, q.shape, 1)
    zero = jnp.zeros_like(q)
    q2 = jnp.concatenate([jnp.where(lane < DIFF_HEAD_DIM, q, zero),
                          jnp.where(lane >= DIFF_HEAD_DIM, q, zero)], axis=0)

    def scores(kj):
        return lax.dot_general(q2, kj, _NT, preferred_element_type=F32)

    start = pl.multiple_of(i * tq, tq)
    s = scores(k_ref[pl.ds(start, tq), :])
    row = lax.broadcasted_iota(jnp.int32, s.shape, 0)
    col = lax.broadcasted_iota(jnp.int32, s.shape, 1)
    visible = (col >> CHUNK_SHIFT) <= ((row & (tq - 1)) >> CHUNK_SHIFT)
    s = jnp.where(visible, s, -jnp.inf)
    m = jnp.max(s, axis=-1, keepdims=True)
    p = jnp.exp2(s - m)
    l = jnp.sum(p, axis=-1, keepdims=True)
    acc = jnp.dot(p.astype(BF16), v_ref[pl.ds(start, tq), :], preferred_element_type=F32)

    def body(j, carry):
        m, l, acc = carry
        off = pl.multiple_of(j * tq, tq)
        s = scores(k_ref[pl.ds(off, tq), :])
        m_new = jnp.maximum(m, jnp.max(s, axis=-1, keepdims=True))
        alpha = jnp.exp2(m - m_new)
        p = jnp.exp2(s - m_new)
        l = alpha * l + jnp.sum(p, axis=-1, keepdims=True)
        acc = alpha * acc + jnp.dot(p.astype(BF16), v_ref[pl.ds(off, tq), :],
                                    preferred_element_type=F32)
        return m_new, l, acc

    m, l, acc = lax.fori_loop(0, i, body, (m, l, acc))
    o = acc / l
    lam = _lambda_full(lq1_ref, lk1_ref, lq2_ref, lk2_ref, lam_init)
    od = o[:tq] - lam * o[tq:]
    od = od * _rsqrt_mean_sq(od) * sgc_ref[...].T * (1.0 - lam_init)
    o_ref[...] = od.astype(BF16)


def _diff_attn(qt, k, vt, lq1, lk1, lq2, lk2, subln_g_col, lam_init, batch, score_bound):
    n = k.shape[0]
    s = n // batch
    blocks = s // KEY_BLOCK
    lam_spec = _resident((1, DIFF_HEAD_DIM), lambda bi, hi, qi: (0, 0))
    k_spec = pl.BlockSpec((s, HEAD_LANES), lambda bi, hi, qi: (bi, hi))
    vt_spec = pl.BlockSpec((blocks, HEAD_LANES, KEY_BLOCK), lambda bi, hi, qi: (bi, hi, 0))
    gain_spec = _resident((HEAD_LANES, 1), lambda bi, hi, qi: (0, 0))

    def call(body, tq, scratch, name):
        per_batch = s // tq
        return pl.pallas_call(
            body,
            grid=(batch, N_HEADS, per_batch),
            in_specs=[pl.BlockSpec((HEAD_LANES, tq), lambda bi, hi, qi: (hi, bi * per_batch + qi)),
                      k_spec, vt_spec, lam_spec, lam_spec, lam_spec, lam_spec, gain_spec],
            out_specs=pl.BlockSpec((tq, HEAD_LANES), lambda bi, hi, qi: (bi * per_batch + qi, hi)),
            out_shape=jax.ShapeDtypeStruct((n, SEG), BF16),
            scratch_shapes=scratch,
            compiler_params=pltpu.CompilerParams(
                dimension_semantics=("arbitrary", "arbitrary", "arbitrary"),
                vmem_limit_bytes=VMEM_LIMIT_BYTES),
            name=name,
        )

    bounded = call(
        functools.partial(_attn_bounded_kernel, lam_init=lam_init), KEY_BLOCK,
        [pltpu.VMEM((KEY_BLOCK // ATTN_SUB, HEAD_LANES + ONES_ROWS, 2 * ATTN_SUB), F32)],
        "diff_attn_bounded")
    online = call(
        functools.partial(_attn_online_kernel, tq=ATTN_SUB, lam_init=lam_init), ATTN_SUB,
        [pltpu.VMEM((s, HEAD_LANES), BF16)], "diff_attn_online")
    return lax.cond(score_bound <= SCORE_BOUND_MAX, bounded, online,
                    qt, k, vt, lq1, lk1, lq2, lk2, subln_g_col)


def _hgrn_tables():
    c = CHUNK
    masks = np.zeros((N_LEVELS + 1, c, c), np.float32)
    masks[0] = np.eye(c, dtype=np.float32)
    for lev in range(1, N_LEVELS + 1):
        size = 1 << lev
        half = size // 2
        for r in range(c):
            if r % size >= half:
                mid = r - r % size + half
                masks[lev, r, mid - half:mid] = 1.0
    t_idx, s_idx = np.arange(c)[:, None], np.arange(c)[None, :]
    in_block = ((t_idx // REF_BLOCK == s_idx // REF_BLOCK) & (s_idx <= t_idx)).astype(np.float32)
    masks = np.concatenate([masks, in_block[None]], axis=0)
    padded = np.zeros((masks.shape[0], 2, c, 2 * c), np.float32)
    padded[:, 0, :, :c] = masks
    padded[:, 1, :, c:] = masks
    return np.tril(np.ones((c, c), np.float32)), padded


def _level_ref_row(lev, row):
    size = 1 << lev
    return row - row % size + size // 2 - 1


def _hgrn_kernel(hq_ref, lf_ref, k_ref, hi_ref, hg_ref, g_ref, tril_ref, masks_ref,
                 o_ref, state_ref, b_ref, *, n_chunks):
    @pl.when(pl.program_id(1) == 0)
    def _():
        state_ref[...] = jnp.zeros_like(state_ref)

    tn = (((0,), (0,)), ((), ()))
    groups = CHUNK // SUBLANES
    sub = lax.broadcasted_iota(jnp.int32, (SUBLANES, SEG), 0)
    up_small = {lev: (sub & (1 << (lev - 1))) != 0 for lev in (1, 2, 3)}
    sgn_small = {lev: jnp.where(up_small[lev], 1.0, -1.0) for lev in (2, 3)}
    upper_groups = {lev: [g for g in range(groups)
                          if (g * SUBLANES) % (1 << lev) >= (1 << lev) // 2]
                    for lev in range(4, N_LEVELS + 1)}

    def grp(a, g):
        return a[g * SUBLANES:(g + 1) * SUBLANES, :]

    def stack(parts):
        return jnp.concatenate(parts, axis=0).astype(BF16)

    def heads_to_rows(a):
        return jnp.concatenate(
            [a[:, h * HEAD_LANES:(h + 1) * HEAD_LANES] for h in range(N_HEADS)], axis=0)

    worst = None
    for c in range(n_chunks):
        rows = slice(c * CHUNK, (c + 1) * CHUNK)
        lf2 = lf_ref[rows, :]
        hi = lf2.astype(BF16)
        lo = (lf2 - hi.astype(F32)).astype(BF16)
        b2 = (jnp.dot(tril_ref[...], hi, preferred_element_type=F32)
              + jnp.dot(tril_ref[...], lo, preferred_element_type=F32))
        b_ref[c] = b2
        for blk in range(CHUNK // REF_BLOCK):
            last = (blk + 1) * REF_BLOCK - 1
            span = -b2[last:last + 1, :]
            if blk:
                span = span + b2[blk * REF_BLOCK - 1:blk * REF_BLOCK, :]
            worst = span if worst is None else jnp.maximum(worst, span)
    moderate = jnp.max(worst) <= REF_BLOCK_DECAY_MAX

    def chunk(c, single_reference):
        rows = slice(c * CHUNK, (c + 1) * CHUNK)
        b2 = b_ref[c]
        kk = k_ref[rows, :]
        qq = hq_ref[rows, :].astype(F32)

        def ref_rows(row):
            return jnp.broadcast_to(b_ref[c, row:row + 1, :], (SUBLANES, SEG))

        x = {lev: [] for lev in range(1, N_LEVELS + 1)}
        xq_blk, xk_blk, q_pre, k_suf = [], [], [], []
        for g in range(groups):
            b_g, q_g, k_g = grp(b2, g), grp(qq, g), grp(kk, g)
            row0 = g * SUBLANES
            if single_reference:
                start = row0 - row0 % REF_BLOCK
                d = b_g - ref_rows(start - 1) if start else b_g
                xq_blk.append(q_g * jnp.exp2(d))
                xk_blk.append(k_g * jnp.exp2(-d))
                first_level = REF_BLOCK_LEVEL + 1
            else:
                x[1].append(jnp.where(up_small[1], q_g * jnp.exp2(grp(lf_ref[rows, :], g)), k_g))
                ref2 = jnp.where(sub < 4, ref_rows(_level_ref_row(2, row0)),
                                 ref_rows(_level_ref_row(2, row0 + 4)))
                x[2].append(jnp.where(up_small[2], q_g, k_g)
                            * jnp.exp2((b_g - ref2) * sgn_small[2]))
                ref3 = ref_rows(_level_ref_row(3, row0))
                x[3].append(jnp.where(up_small[3], q_g, k_g)
                            * jnp.exp2((b_g - ref3) * sgn_small[3]))
                first_level = 4
            for lev in range(first_level, N_LEVELS + 1):
                ref = ref_rows(_level_ref_row(lev, row0))
                if row0 % (1 << lev) >= (1 << lev) // 2:
                    x[lev].append(q_g * jnp.exp2(b_g - ref))
                else:
                    x[lev].append(k_g * jnp.exp2(ref - b_g))
            q_pre.append(q_g * jnp.exp2(b_g))
            k_suf.append(k_g * jnp.exp2(ref_rows(CHUNK - 1) - b_g))

        all_groups = list(range(groups))
        if single_reference:
            terms = [(MASK_REF_BLOCK, all_groups, xq_blk, xk_blk)]
        else:
            terms = [(lev, all_groups, x[lev], x[lev]) for lev in (1, 2, 3)]
        for lev in range(first_level, N_LEVELS + 1):
            terms.append((lev, upper_groups[lev], [x[lev][g] for g in upper_groups[lev]], x[lev]))

        scores = [lax.dot_general(heads_to_rows(stack(q_rows)), heads_to_rows(stack(k_rows)), _NT,
                                  preferred_element_type=F32)
                  for _, _, q_rows, k_rows in terms]

        pair_blocks = []
        for pair in range(N_HEADS // 2):
            lanes = slice(pair * HEAD_LANES, (pair + 1) * HEAD_LANES)
            a_rows = []
            for h in (2 * pair, 2 * pair + 1):
                for g in range(groups):
                    gs = slice(g * SUBLANES, (g + 1) * SUBLANES)
                    a_g = None
                    if not single_reference:
                        hs = slice(h * HEAD_LANES, (h + 1) * HEAD_LANES)
                        a_g = (jnp.sum(qq[gs, hs] * kk[gs, hs], axis=-1, keepdims=True)
                               * masks_ref[0, h % 2, gs, :])
                    for (mask_id, q_groups, _, _), sc in zip(terms, scores):
                        if g not in q_groups:
                            continue
                        r0 = (h * len(q_groups) + q_groups.index(g)) * SUBLANES
                        part = sc[r0:r0 + SUBLANES, lanes] * masks_ref[mask_id, h % 2, gs, :]
                        a_g = part if a_g is None else a_g + part
                    a_rows.append(a_g)
            pair_blocks.append(jnp.concatenate(a_rows, axis=0))
        zero = jnp.zeros_like(pair_blocks[0])
        a_full = jnp.concatenate(
            [jnp.concatenate([pair_blocks[0], zero], axis=1),
             jnp.concatenate([zero, pair_blocks[1]], axis=1)], axis=0).astype(BF16)
        o_intra = jnp.dot(a_full, heads_to_rows(hi_ref[rows, :]), preferred_element_type=F32)

        q_pre = stack(q_pre)
        k_suf = stack(k_suf)
        decay_last = jnp.exp2(b2[CHUNK - 1:CHUNK, :])
        for h in range(N_HEADS):
            hs = slice(h * HEAD_LANES, (h + 1) * HEAD_LANES)
            st = state_ref[h]
            o = (o_intra[h * CHUNK:(h + 1) * CHUNK, :]
                 + lax.dot_general(q_pre[:, hs], st.astype(BF16), _NT,
                                   preferred_element_type=F32))
            state_ref[h] = (st * decay_last[:, hs]
                            + lax.dot_general(hi_ref[rows, hs], k_suf[:, hs], tn,
                                              preferred_element_type=F32))
            o = o * _rsqrt_mean_sq(o) * g_ref[...] * hg_ref[rows, hs].astype(F32)
            o_ref[rows, hs] = o.astype(BF16)

    @pl.when(moderate)
    def _():
        for c in range(n_chunks):
            chunk(c, True)

    @pl.when(jnp.logical_not(moderate))
    def _():
        for c in range(n_chunks):
            chunk(c, False)


def _hgrn(hq, hf, hk, hi, hg, norm_g, tables, tc, batch):
    n = hq.shape[0]
    tril, masks = tables
    per_batch = n // batch // tc
    tile = pl.BlockSpec((tc, SEG), lambda bi, ci: (bi * per_batch + ci, 0))
    const2 = lambda bi, ci: (0, 0)
    return pl.pallas_call(
        functools.partial(_hgrn_kernel, n_chunks=tc // CHUNK),
        grid=(batch, per_batch),
        in_specs=[tile, tile, tile, tile, tile,
                  _resident((1, HEAD_LANES), const2),
                  _resident(tril.shape, const2),
                  _resident(masks.shape, lambda bi, ci: (0, 0, 0, 0))],
        out_specs=tile,
        out_shape=jax.ShapeDtypeStruct((n, SEG), BF16),
        scratch_shapes=[pltpu.VMEM((N_HEADS, HEAD_LANES, HEAD_LANES), F32),
                        pltpu.VMEM((tc // CHUNK, CHUNK, SEG), F32)],
        compiler_params=pltpu.CompilerParams(
            dimension_semantics=("arbitrary", "arbitrary"),
            vmem_limit_bytes=VMEM_LIMIT_BYTES),
        name="hgrn2",
    )(hq, hf, hk, hi, hg, norm_g, tril, masks)


def _out_ffn_kernel(x_ref, od_ref, or_ref, wo_ref, g_ref, wg_ref, wu_ref, wd_ref,
                    o_ref, ff_ref, *, d_ff):
    xn = (x_ref[...]
          + jnp.dot(od_ref[...], wo_ref[:SEG, :], preferred_element_type=F32)
          + jnp.dot(or_ref[...], wo_ref[SEG:, :], preferred_element_type=F32))
    h = (xn * _rsqrt_mean_sq(xn) * g_ref[...]).astype(BF16)
    for j in range(d_ff // FFN_TILE):
        cols = slice(j * FFN_TILE, (j + 1) * FFN_TILE)
        gate = jnp.dot(h, wg_ref[:, cols], preferred_element_type=F32)
        up = jnp.dot(h, wu_ref[:, cols], preferred_element_type=F32)
        ff_ref[:, cols] = (gate * _sigmoid(gate) * up).astype(BF16)
    o_ref[...] = xn + jnp.dot(ff_ref[...], wd_ref[...], preferred_element_type=F32)


def _out_ffn(x, od, orr, w_out, norm_g, w_gate, w_up, w_down, tm):
    n, d = x.shape
    d_ff = w_gate.shape[-1]
    row = lambda i: (i, 0)
    const2 = lambda i: (0, 0)
    return pl.pallas_call(
        functools.partial(_out_ffn_kernel, d_ff=d_ff),
        grid=(n // tm,),
        in_specs=[
            pl.BlockSpec((tm, d), row),
            pl.BlockSpec((tm, SEG), row),
            pl.BlockSpec((tm, SEG), row),
            _resident((d, d), const2),
            _resident((1, d), const2),
            _resident((d, d_ff), const2),
            _resident((d, d_ff), const2),
            _resident((d_ff, d), const2),
        ],
        out_specs=pl.BlockSpec((tm, d), row),
        out_shape=jax.ShapeDtypeStruct((n, d), F32),
        scratch_shapes=[pltpu.VMEM((tm, d_ff), BF16)],
        compiler_params=pltpu.CompilerParams(
            dimension_semantics=("arbitrary",), vmem_limit_bytes=VMEM_LIMIT_BYTES),
        name="out_ffn",
    )(x, od, orr, w_out, norm_g, w_gate, w_up, w_down)


def _tile(n, want):
    return min(n, want)


def kernel(x, attn_norm_g, w_in, q_norm_g, k_norm_g, lambda_q1, lambda_k1, lambda_q2, lambda_k2, subln_g, lower_bounds, hgrn_norm_g, w_out, ffn_norm_g, w_gate, w_up, w_down):
    b, s, d = x.shape
    depth = w_in.shape[0]
    n = b * s
    assert d == 2 * SEG and w_in.shape[-1] == N_SEG * SEG and s % KEY_BLOCK == 0

    w_in_b = w_in[0].astype(BF16)

    tril, masks = _hgrn_tables()
    tables = (jnp.asarray(tril, BF16), jnp.asarray(masks))
    lower_bounds = lower_bounds.astype(F32)

    tm_proj = _tile(s, PROJ_ROWS)
    tm_hgrn = _tile(s, HGRN_ROWS)
    tm_ffn = _tile(s, FFN_ROWS)

    xf = x.reshape(n, d)
    for l in range(depth):
        lam_init = 0.8 - 0.6 * math.exp(-0.3 * l)
        qg = jnp.tile(q_norm_g[l], 2)[:, None] * (DIFF_HEAD_DIM ** -0.5 * LOG2E)
        kg = jnp.tile(k_norm_g[l], 2)[:, None]
        score_bound = (1.02 * math.sqrt(DIFF_HEAD_DIM)
                       * jnp.max(jnp.abs(q_norm_g[l])) * jnp.max(jnp.abs(k_norm_g[l])))
        convert = [(w_out, l), (w_gate, l), (w_up, l), (w_down, l)]
        if l + 1 < depth:
            convert.append((w_in, l + 1))
        (qt, k, vt, hq, hf, hk, hi, hg), converted = _in_proj(
            xf, attn_norm_g[l][None], w_in_b, qg, kg, lower_bounds, l, tm_proj, convert)
        w_out_b, w_gate_b, w_up_b, w_down_b = converted[:4]
        w_in_b = converted[4] if l + 1 < depth else None
        orr = _hgrn(hq, hf, hk, hi, hg, hgrn_norm_g[l][None], tables, tm_hgrn, b)
        od = _diff_attn(qt, k, vt, lambda_q1[l][None], lambda_k1[l][None],
                        lambda_q2[l][None], lambda_k2[l][None], subln_g[l][:, None], lam_init, b,
                        score_bound)
        xf = _out_ffn(xf, od, orr, w_out_b, ffn_norm_g[l][None],
                      w_gate_b, w_up_b, w_down_b, tm_ffn)
    return xf.reshape(b, s, d)
```

```python
import functools
import math

import numpy as np
import jax
import jax.numpy as jnp
from jax import lax
from jax.experimental import pallas as pl
from jax.experimental.pallas import tpu as pltpu

F32 = jnp.float32
BF16 = jnp.bfloat16

NORM_EPS = 1e-6
CHUNK = 64
CHUNK_SHIFT = 6
HEAD_LANES = 128
N_HEADS = 4
SEG = N_HEADS * HEAD_LANES
DIFF_HEAD_DIM = 64
N_SEG = 7
N_LEVELS = CHUNK_SHIFT

VMEM_LIMIT_BYTES = 52 * 1024 * 1024
FFN_TILE = 256
PROJ_ROWS = 1024
HGRN_ROWS = 1024
FFN_ROWS = 1024
KEY_BLOCK = 4096
ATTN_SUB = 256
ONES_ROWS = 16
SCORE_BOUND_MAX = 40.0
_NT = (((1,), (1,)), ((), ()))
SUBLANES = 8
BF16_SUBLANES = 16
LOG2E = math.log2(math.e)
REF_BLOCK_LEVEL = 4
REF_BLOCK = 1 << REF_BLOCK_LEVEL
MASK_REF_BLOCK = N_LEVELS + 1
REF_BLOCK_DECAY_MAX = 60.0


def _rsqrt_mean_sq(y):
    return lax.rsqrt(jnp.mean(y * y, axis=-1, keepdims=True) + NORM_EPS)


def _sigmoid(z):
    return 1.0 / (1.0 + jnp.exp(-z))


def _resident(block_shape, index_map):
    return pl.BlockSpec(block_shape, index_map, pipeline_mode=pl.Buffered(1))


def _in_proj_stages(layer, x_ref, g_ref, w_ref, qg_ref, kg_ref, lb_ref,
                    qt_ref, k_ref, vt_ref, hq_ref, hf_ref, hk_ref, hi_ref, hg_ref):
    x = x_ref[...]
    h = (x * _rsqrt_mean_sq(x) * g_ref[...]).astype(BF16)

    def seg(s):
        return jnp.dot(h, w_ref[:, s * SEG:(s + 1) * SEG], preferred_element_type=F32)

    def head_norm_t(y, gain_ref, c):
        t = y[:, c * HEAD_LANES:(c + 1) * HEAD_LANES].T
        halves = []
        for m in range(HEAD_LANES // DIFF_HEAD_DIM):
            tm_ = t[m * DIFF_HEAD_DIM:(m + 1) * DIFF_HEAD_DIM]
            ms = jnp.mean(tm_ * tm_, axis=0, keepdims=True)
            halves.append(tm_ * lax.rsqrt(ms + NORM_EPS))
        return jnp.concatenate(halves, axis=0) * gain_ref[...]

    def q_stage():
        yq = seg(0)
        for c in range(N_HEADS):
            lanes = slice(c * HEAD_LANES, (c + 1) * HEAD_LANES)
            qt_ref[lanes, :] = head_norm_t(yq, qg_ref, c).astype(BF16)

    def k_stage():
        yk = seg(1)
        for c in range(N_HEADS):
            lanes = slice(c * HEAD_LANES, (c + 1) * HEAD_LANES)
            k_ref[:, lanes] = head_norm_t(yk, kg_ref, c).T.astype(BF16)

    def v_stage():
        vt_ref[...] = seg(2).T.astype(BF16)

    def hq_stage():
        qr = seg(3)
        hq_ref[...] = (qr * _sigmoid(qr)).astype(BF16)

    def hf_stage():
        z = seg(4)
        e = jnp.exp(-jnp.abs(z))
        r = 1.0 / (1.0 + e)
        sig_neg = jnp.where(z >= 0, e * r, r)
        if layer == 0:
            hf_ref[...] = jnp.minimum(z * LOG2E, 0.0) - jnp.log2(1.0 + e)
            hk_ref[...] = sig_neg
        else:
            lbp = lb_ref[...]
            ex = jnp.exp(lbp - jnp.max(lbp, axis=0, keepdims=True))
            gamma = ex / jnp.sum(ex, axis=0, keepdims=True)
            lb = jnp.sum(gamma[1:layer + 1], axis=0, keepdims=True)
            sig_pos = jnp.where(z >= 0, r, e * r)
            hf_ref[...] = jnp.log2(lb + (1.0 - lb) * sig_pos)
            hk_ref[...] = (1.0 - lb) * sig_neg

    def hi_stage():
        hi_ref[...] = seg(5).astype(BF16)

    def hg_stage():
        gr = seg(6)
        hg_ref[...] = (gr * _sigmoid(gr)).astype(BF16)

    return [q_stage, k_stage, v_stage, hq_stage, hf_stage, hi_stage, hg_stage]


N_PROJ_IN = 6
N_PROJ_OUT = 8


def _in_proj_kernel(*refs, n_cast, layer):
    cast_in = refs[N_PROJ_IN:N_PROJ_IN + n_cast]
    outs = refs[N_PROJ_IN + n_cast:N_PROJ_IN + n_cast + N_PROJ_OUT]
    cast_out = refs[N_PROJ_IN + n_cast + N_PROJ_OUT:]
    for stage in _in_proj_stages(layer, *refs[:N_PROJ_IN], *outs):
        stage()
    for src, dst in zip(cast_in, cast_out):
        dst[...] = src[...].astype(BF16)


def _in_proj(x, norm_g, w_in, qg, kg, lower_bounds, layer, tm, convert):
    n, d = x.shape
    steps = n // tm
    row = lambda i: (i, 0)
    const2 = lambda i: (0, 0)
    out_bf = jax.ShapeDtypeStruct((n, SEG), BF16)
    out_f32 = jax.ShapeDtypeStruct((n, SEG), F32)
    seg_spec = pl.BlockSpec((tm, SEG), row)
    per_blk = KEY_BLOCK // tm
    cast_in_specs, cast_out_specs, cast_out_shapes = [], [], []
    for w, w_layer in convert:
        _, rows, cols = w.shape
        slabs = max(k for k in range(1, steps + 1)
                    if rows % k == 0 and (rows // k) % BF16_SUBLANES == 0)
        slab = lambda i, last=slabs - 1: jnp.minimum(i, last)
        cast_in_specs.append(pl.BlockSpec((None, rows // slabs, cols),
                                          lambda i, slab=slab, w_layer=w_layer: (w_layer, slab(i), 0)))
        cast_out_specs.append(pl.BlockSpec((rows // slabs, cols),
                                           lambda i, slab=slab: (slab(i), 0)))
        cast_out_shapes.append(jax.ShapeDtypeStruct((rows, cols), BF16))
    outs = pl.pallas_call(
        functools.partial(_in_proj_kernel, n_cast=len(convert), layer=layer),
        grid=(steps,),
        in_specs=[
            pl.BlockSpec((tm, d), row),
            _resident((1, d), const2),
            _resident((d, N_SEG * SEG), const2),
            _resident((HEAD_LANES, 1), const2),
            _resident((HEAD_LANES, 1), const2),
            _resident(lower_bounds.shape, const2),
        ] + cast_in_specs,
        out_specs=[pl.BlockSpec((SEG, tm), lambda i: (0, i)), seg_spec,
                   pl.BlockSpec((None, SEG, tm), lambda i: (i // per_blk, 0, i % per_blk)),
                   seg_spec, seg_spec, seg_spec, seg_spec, seg_spec] + cast_out_specs,
        out_shape=[jax.ShapeDtypeStruct((SEG, n), BF16), out_bf,
                   jax.ShapeDtypeStruct((n // KEY_BLOCK, SEG, KEY_BLOCK), BF16),
                   out_bf, out_f32, out_f32, out_bf, out_bf] + cast_out_shapes,
        compiler_params=pltpu.CompilerParams(
            dimension_semantics=("arbitrary",), vmem_limit_bytes=VMEM_LIMIT_BYTES),
        name="in_proj",
    )(x, norm_g, w_in, qg, kg, lower_bounds, *[w for w, _ in convert])
    return outs[:N_PROJ_OUT], outs[N_PROJ_OUT:]


def _lambda_full(lq1_ref, lk1_ref, lq2_ref, lk2_ref, lam_init):
    return (jnp.exp(jnp.sum(lq1_ref[...] * lk1_ref[...], axis=-1, keepdims=True))
            - jnp.exp(jnp.sum(lq2_ref[...] * lk2_ref[...], axis=-1, keepdims=True))
            + lam_init)


def _split_maps_t(qt):
    row = lax.broadcasted_iota(jnp.int32, qt.shape, 0)
    zero = jnp.zeros_like(qt)
    return jnp.concatenate([jnp.where(row < DIFF_HEAD_DIM, qt, zero),
                            jnp.where(row >= DIFF_HEAD_DIM, qt, zero)], axis=1)


def _attn_bounded_kernel(qt_ref, k_ref, vt_ref, lq1_ref, lk1_ref, lq2_ref, lk2_ref, sgc_ref,
                         o_ref, acc_ref, *, lam_init):
    blk = pl.program_id(2)
    n_sub = KEY_BLOCK // ATTN_SUB
    ones = jnp.ones((ONES_ROWS, KEY_BLOCK), BF16)

    def q2t(s):
        return _split_maps_t(qt_ref[:, s * ATTN_SUB:(s + 1) * ATTN_SUB])

    def attend(s, k_rows, vt, diagonal):
        n_keys = k_rows.shape[0]
        st = jnp.dot(k_rows, q2t(s), preferred_element_type=F32)
        p = jnp.exp2(st)
        if diagonal:
            tail = p[n_keys - ATTN_SUB:]
            krow = lax.broadcasted_iota(jnp.int32, tail.shape, 0)
            qcol = lax.broadcasted_iota(jnp.int32, tail.shape, 1)
            visible = (krow >> CHUNK_SHIFT) <= ((qcol & (ATTN_SUB - 1)) >> CHUNK_SHIFT)
            tail = jnp.where(visible, tail, 0.0)
            p = tail if n_keys == ATTN_SUB else jnp.concatenate([p[:n_keys - ATTN_SUB], tail], axis=0)
        vte = jnp.concatenate([vt, ones[:, :n_keys]], axis=0)
        return jnp.dot(vte, p.astype(BF16), preferred_element_type=F32)

    start = pl.multiple_of(blk * KEY_BLOCK, KEY_BLOCK)
    for s in range(n_sub):
        n_keys = (s + 1) * ATTN_SUB
        acc_ref[s] = attend(s, k_ref[pl.ds(start, n_keys), :], vt_ref[blk, :, :n_keys], True)

    def body(j, carry):
        k_rows = k_ref[pl.ds(pl.multiple_of(j * KEY_BLOCK, KEY_BLOCK), KEY_BLOCK), :]
        vt = vt_ref[j]
        for s in range(n_sub):
            acc_ref[s] += attend(s, k_rows, vt, False)
        return carry

    lax.fori_loop(0, blk, body, 0)

    lam = _lambda_full(lq1_ref, lk1_ref, lq2_ref, lk2_ref, lam_init)
    for s in range(n_sub):
        a = acc_ref[s]
        o = a[:HEAD_LANES] * (1.0 / a[HEAD_LANES:HEAD_LANES + 1])
        od = o[:, :ATTN_SUB] - lam * o[:, ATTN_SUB:]
        ms = jnp.mean(od * od, axis=0, keepdims=True)
        od = od * lax.rsqrt(ms + NORM_EPS) * sgc_ref[...] * (1.0 - lam_init)
        o_ref[s * ATTN_SUB:(s + 1) * ATTN_SUB, :] = od.T.astype(BF16)


def _attn_online_kernel(qt_ref, k_ref, vt_ref, lq1_ref, lk1_ref, lq2_ref, lk2_ref, sgc_ref,
                        o_ref, v_ref, *, tq, lam_init):
    i = pl.program_id(2)

    @pl.when(i == 0)
    def _():
        for j in range(vt_ref.shape[0]):
            v_ref[j * KEY_BLOCK:(j + 1) * KEY_BLOCK, :] = vt_ref[j].astype(F32).T.astype(BF16)

    q = qt_ref[...].astype(F32).T.astype(BF16)
    lane = lax.broadcasted_iota(jnp.int32, q.shape, 1)
    zero = jnp.zeros_like(q)
    q2 = jnp.concatenate([jnp.where(lane < DIFF_HEAD_DIM, q, zero),
                          jnp.where(lane >= DIFF_HEAD_DIM, q, zero)], axis=0)

    def scores(kj):
        return lax.dot_general(q2, kj, _NT, preferred_element_type=F32)

    start = pl.multiple_of(i * tq, tq)
    s = scores(k_ref[pl.ds(start, tq), :])
    row = lax.broadcasted_iota(jnp.int32, s.shape, 0)
    col = lax.broadcasted_iota(jnp.int32, s.shape, 1)
    visible = (col >> CHUNK_SHIFT) <= ((row & (tq - 1)) >> CHUNK_SHIFT)
    s = jnp.where(visible, s, -jnp.inf)
    m = jnp.max(s, axis=-1, keepdims=True)
    p = jnp.exp2(s - m)
    l = jnp.sum(p, axis=-1, keepdims=True)
    acc = jnp.dot(p.astype(BF16), v_ref[pl.ds(start, tq), :], preferred_element_type=F32)

    def body(j, carry):
        m, l, acc = carry
        off = pl.multiple_of(j * tq, tq)
        s = scores(k_ref[pl.ds(off, tq), :])
        m_new = jnp.maximum(m, jnp.max(s, axis=-1, keepdims=True))
        alpha = jnp.exp2(m - m_new)
        p = jnp.exp2(s - m_new)
        l = alpha * l + jnp.sum(p, axis=-1, keepdims=True)
        acc = alpha * acc + jnp.dot(p.astype(BF16), v_ref[pl.ds(off, tq), :],
                                    preferred_element_type=F32)
        return m_new, l, acc

    m, l, acc = lax.fori_loop(0, i, body, (m, l, acc))
    o = acc / l
    lam = _lambda_full(lq1_ref, lk1_ref, lq2_ref, lk2_ref, lam_init)
    od = o[:tq] - lam * o[tq:]
    od = od * _rsqrt_mean_sq(od) * sgc_ref[...].T * (1.0 - lam_init)
    o_ref[...] = od.astype(BF16)


def _diff_attn(qt, k, vt, lq1, lk1, lq2, lk2, subln_g_col, lam_init, batch, score_bound):
    n = k.shape[0]
    s = n // batch
    blocks = s // KEY_BLOCK
    lam_spec = _resident((1, DIFF_HEAD_DIM), lambda bi, hi, qi: (0, 0))
    k_spec = pl.BlockSpec((s, HEAD_LANES), lambda bi, hi, qi: (bi, hi))
    vt_spec = pl.BlockSpec((blocks, HEAD_LANES, KEY_BLOCK), lambda bi, hi, qi: (bi, hi, 0))
    gain_spec = _resident((HEAD_LANES, 1), lambda bi, hi, qi: (0, 0))

    def call(body, tq, scratch, name):
        per_batch = s // tq
        return pl.pallas_call(
            body,
            grid=(batch, N_HEADS, per_batch),
            in_specs=[pl.BlockSpec((HEAD_LANES, tq), lambda bi, hi, qi: (hi, bi * per_batch + qi)),
                      k_spec, vt_spec, lam_spec, lam_spec, lam_spec, lam_spec, gain_spec],
            out_specs=pl.BlockSpec((tq, HEAD_LANES), lambda bi, hi, qi: (bi * per_batch + qi, hi)),
            out_shape=jax.ShapeDtypeStruct((n, SEG), BF16),
            scratch_shapes=scratch,
            compiler_params=pltpu.CompilerParams(
                dimension_semantics=("arbitrary", "arbitrary", "arbitrary"),
                vmem_limit_bytes=VMEM_LIMIT_BYTES),
            name=name,
        )

    bounded = call(
        functools.partial(_attn_bounded_kernel, lam_init=lam_init), KEY_BLOCK,
        [pltpu.VMEM((KEY_BLOCK // ATTN_SUB, HEAD_LANES + ONES_ROWS, 2 * ATTN_SUB), F32)],
        "diff_attn_bounded")
    online = call(
        functools.partial(_attn_online_kernel, tq=ATTN_SUB, lam_init=lam_init), ATTN_SUB,
        [pltpu.VMEM((s, HEAD_LANES), BF16)], "diff_attn_online")
    return lax.cond(score_bound <= SCORE_BOUND_MAX, bounded, online,
                    qt, k, vt, lq1, lk1, lq2, lk2, subln_g_col)


def _hgrn_tables():
    c = CHUNK
    masks = np.zeros((N_LEVELS + 1, c, c), np.float32)
    masks[0] = np.eye(c, dtype=np.float32)
    for lev in range(1, N_LEVELS + 1):
        size = 1 << lev
        half = size // 2
        for r in range(c):
            if r % size >= half:
                mid = r - r % size + half
                masks[lev, r, mid - half:mid] = 1.0
    t_idx, s_idx = np.arange(c)[:, None], np.arange(c)[None, :]
    in_block = ((t_idx // REF_BLOCK == s_idx // REF_BLOCK) & (s_idx <= t_idx)).astype(np.float32)
    masks = np.concatenate([masks, in_block[None]], axis=0)
    padded = np.zeros((masks.shape[0], 2, c, 2 * c), np.float32)
    padded[:, 0, :, :c] = masks
    padded[:, 1, :, c:] = masks
    return np.tril(np.ones((c, c), np.float32)), padded


def _level_ref_row(lev, row):
    size = 1 << lev
    return row - row % size + size // 2 - 1


def _hgrn_kernel(hq_ref, lf_ref, k_ref, hi_ref, hg_ref, g_ref, tril_ref, masks_ref,
                 o_ref, state_ref, b_ref, *, n_chunks):
    @pl.when(pl.program_id(1) == 0)
    def _():
        state_ref[...] = jnp.zeros_like(state_ref)

    tn = (((0,), (0,)), ((), ()))
    groups = CHUNK // SUBLANES
    sub = lax.broadcasted_iota(jnp.int32, (SUBLANES, SEG), 0)
    up_small = {lev: (sub & (1 << (lev - 1))) != 0 for lev in (1, 2, 3)}
    sgn_small = {lev: jnp.where(up_small[lev], 1.0, -1.0) for lev in (2, 3)}
    upper_groups = {lev: [g for g in range(groups)
                          if (g * SUBLANES) % (1 << lev) >= (1 << lev) // 2]
                    for lev in range(4, N_LEVELS + 1)}

    def grp(a, g):
        return a[g * SUBLANES:(g + 1) * SUBLANES, :]

    def stack(parts):
        return jnp.concatenate(parts, axis=0).astype(BF16)

    def heads_to_rows(a):
        return jnp.concatenate(
            [a[:, h * HEAD_LANES:(h + 1) * HEAD_LANES] for h in range(N_HEADS)], axis=0)

    worst = None
    for c in range(n_chunks):
        rows = slice(c * CHUNK, (c + 1) * CHUNK)
        lf2 = lf_ref[rows, :]
        hi = lf2.astype(BF16)
        lo = (lf2 - hi.astype(F32)).astype(BF16)
        b2 = (jnp.dot(tril_ref[...], hi, preferred_element_type=F32)
              + jnp.dot(tril_ref[...], lo, preferred_element_type=F32))
        b_ref[c] = b2
        for blk in range(CHUNK // REF_BLOCK):
            last = (blk + 1) * REF_BLOCK - 1
            span = -b2[last:last + 1, :]
            if blk:
                span = span + b2[blk * REF_BLOCK - 1:blk * REF_BLOCK, :]
            worst = span if worst is None else jnp.maximum(worst, span)
    moderate = jnp.max(worst) <= REF_BLOCK_DECAY_MAX

    def chunk(c, single_reference):
        rows = slice(c * CHUNK, (c + 1) * CHUNK)
        b2 = b_ref[c]
        kk = k_ref[rows, :]
        qq = hq_ref[rows, :].astype(F32)

        def ref_rows(row):
            return jnp.broadcast_to(b_ref[c, row:row + 1, :], (SUBLANES, SEG))

        x = {lev: [] for lev in range(1, N_LEVELS + 1)}
        xq_blk, xk_blk, q_pre, k_suf = [], [], [], []
        for g in range(groups):
            b_g, q_g, k_g = grp(b2, g), grp(qq, g), grp(kk, g)
            row0 = g * SUBLANES
            if single_reference:
                start = row0 - row0 % REF_BLOCK
                d = b_g - ref_rows(start - 1) if start else b_g
                xq_blk.append(q_g * jnp.exp2(d))
                xk_blk.append(k_g * jnp.exp2(-d))
                first_level = REF_BLOCK_LEVEL + 1
            else:
                x[1].append(jnp.where(up_small[1], q_g * jnp.exp2(grp(lf_ref[rows, :], g)), k_g))
                ref2 = jnp.where(sub < 4, ref_rows(_level_ref_row(2, row0)),
                                 ref_rows(_level_ref_row(2, row0 + 4)))
                x[2].append(jnp.where(up_small[2], q_g, k_g)
                            * jnp.exp2((b_g - ref2) * sgn_small[2]))
                ref3 = ref_rows(_level_ref_row(3, row0))
                x[3].append(jnp.where(up_small[3], q_g, k_g)
                            * jnp.exp2((b_g - ref3) * sgn_small[3]))
                first_level = 4
            for lev in range(first_level, N_LEVELS + 1):
                ref = ref_rows(_level_ref_row(lev, row0))
                if row0 % (1 << lev) >= (1 << lev) // 2:
                    x[lev].append(q_g * jnp.exp2(b_g - ref))
                else:
                    x[lev].append(k_g * jnp.exp2(ref - b_g))
            q_pre.append(q_g * jnp.exp2(b_g))
            k_suf.append(k_g * jnp.exp2(ref_rows(CHUNK - 1) - b_g))

        all_groups = list(range(groups))
        if single_reference:
            terms = [(MASK_REF_BLOCK, all_groups, xq_blk, xk_blk)]
        else:
            terms = [(lev, all_groups, x[lev], x[lev]) for lev in (1, 2, 3)]
        for lev in range(first_level, N_LEVELS + 1):
            terms.append((lev, upper_groups[lev], [x[lev][g] for g in upper_groups[lev]], x[lev]))

        scores = [lax.dot_general(heads_to_rows(stack(q_rows)), heads_to_rows(stack(k_rows)), _NT,
                                  preferred_element_type=F32)
                  for _, _, q_rows, k_rows in terms]

        pair_blocks = []
        for pair in range(N_HEADS // 2):
            lanes = slice(pair * HEAD_LANES, (pair + 1) * HEAD_LANES)
            a_rows = []
            for h in (2 * pair, 2 * pair + 1):
                for g in range(groups):
                    gs = slice(g * SUBLANES, (g + 1) * SUBLANES)
                    a_g = None
                    if not single_reference:
                        hs = slice(h * HEAD_LANES, (h + 1) * HEAD_LANES)
                        a_g = (jnp.sum(qq[gs, hs] * kk[gs, hs], axis=-1, keepdims=True)
                               * masks_ref[0, h % 2, gs, :])
                    for (mask_id, q_groups, _, _), sc in zip(terms, scores):
                        if g not in q_groups:
                            continue
                        r0 = (h * len(q_groups) + q_groups.index(g)) * SUBLANES
                        part = sc[r0:r0 + SUBLANES, lanes] * masks_ref[mask_id, h % 2, gs, :]
                        a_g = part if a_g is None else a_g + part
                    a_rows.append(a_g)
            pair_blocks.append(jnp.concatenate(a_rows, axis=0))
        zero = jnp.zeros_like(pair_blocks[0])
        a_full = jnp.concatenate(
            [jnp.concatenate([pair_blocks[0], zero], axis=1),
             jnp.concatenate([zero, pair_blocks[1]], axis=1)], axis=0).astype(BF16)
        o_intra = jnp.dot(a_full, heads_to_rows(hi_ref[rows, :]), preferred_element_type=F32)

        q_pre = stack(q_pre)
        k_suf = stack(k_suf)
        decay_last = jnp.exp2(b2[CHUNK - 1:CHUNK, :])
        for h in range(N_HEADS):
            hs = slice(h * HEAD_LANES, (h + 1) * HEAD_LANES)
            st = state_ref[h]
            o = (o_intra[h * CHUNK:(h + 1) * CHUNK, :]
                 + lax.dot_general(q_pre[:, hs], st.astype(BF16), _NT,
                                   preferred_element_type=F32))
            state_ref[h] = (st * decay_last[:, hs]
                            + lax.dot_general(hi_ref[rows, hs], k_suf[:, hs], tn,
                                              preferred_element_type=F32))
            o = o * _rsqrt_mean_sq(o) * g_ref[...] * hg_ref[rows, hs].astype(F32)
            o_ref[rows, hs] = o.astype(BF16)

    @pl.when(moderate)
    def _():
        for c in range(n_chunks):
            chunk(c, True)

    @pl.when(jnp.logical_not(moderate))
    def _():
        for c in range(n_chunks):
            chunk(c, False)


def _hgrn(hq, hf, hk, hi, hg, norm_g, tables, tc, batch):
    n = hq.shape[0]
    tril, masks = tables
    per_batch = n // batch // tc
    tile = pl.BlockSpec((tc, SEG), lambda bi, ci: (bi * per_batch + ci, 0))
    const2 = lambda bi, ci: (0, 0)
    return pl.pallas_call(
        functools.partial(_hgrn_kernel, n_chunks=tc // CHUNK),
        grid=(batch, per_batch),
        in_specs=[tile, tile, tile, tile, tile,
                  _resident((1, HEAD_LANES), const2),
                  _resident(tril.shape, const2),
                  _resident(masks.shape, lambda bi, ci: (0, 0, 0, 0))],
        out_specs=tile,
        out_shape=jax.ShapeDtypeStruct((n, SEG), BF16),
        scratch_shapes=[pltpu.VMEM((N_HEADS, HEAD_LANES, HEAD_LANES), F32),
                        pltpu.VMEM((tc // CHUNK, CHUNK, SEG), F32)],
        compiler_params=pltpu.CompilerParams(
            dimension_semantics=("arbitrary", "arbitrary"),
            vmem_limit_bytes=VMEM_LIMIT_BYTES),
        name="hgrn2",
    )(hq, hf, hk, hi, hg, norm_g, tril, masks)


def _out_ffn_kernel(x_ref, od_ref, or_ref, wo_hbm, g_ref, wg_hbm, wu_hbm, wd_hbm,
                    o_ref, ff_ref, wo_ref, wg_ref, wu_ref, wd_ref, sem, *, d_ff):
    first = pl.program_id(0) == 0
    copies = [pltpu.make_async_copy(src, dst, sem.at[k])
              for k, (src, dst) in enumerate(((wo_hbm, wo_ref), (wg_hbm, wg_ref),
                                              (wu_hbm, wu_ref), (wd_hbm, wd_ref)))]

    def compute(fetch_weights):
        if fetch_weights:
            for cp in copies:
                cp.start()
            copies[0].wait()
        xn = (x_ref[...]
              + jnp.dot(od_ref[...], wo_ref[:SEG, :], preferred_element_type=F32)
              + jnp.dot(or_ref[...], wo_ref[SEG:, :], preferred_element_type=F32))
        h = (xn * _rsqrt_mean_sq(xn) * g_ref[...]).astype(BF16)
        if fetch_weights:
            copies[1].wait()
            copies[2].wait()
        for j in range(d_ff // FFN_TILE):
            cols = slice(j * FFN_TILE, (j + 1) * FFN_TILE)
            gate = jnp.dot(h, wg_ref[:, cols], preferred_element_type=F32)
            up = jnp.dot(h, wu_ref[:, cols], preferred_element_type=F32)
            ff_ref[:, cols] = (gate * _sigmoid(gate) * up).astype(BF16)
        if fetch_weights:
            copies[3].wait()
        o_ref[...] = xn + jnp.dot(ff_ref[...], wd_ref[...], preferred_element_type=F32)

    pl.when(first)(lambda: compute(True))
    pl.when(jnp.logical_not(first))(lambda: compute(False))


def _out_ffn(x, od, orr, w_out, norm_g, w_gate, w_up, w_down, tm):
    n, d = x.shape
    d_ff = w_gate.shape[-1]
    row = lambda i: (i, 0)
    const2 = lambda i: (0, 0)
    return pl.pallas_call(
        functools.partial(_out_ffn_kernel, d_ff=d_ff),
        grid=(n // tm,),
        in_specs=[
            pl.BlockSpec((tm, d), row),
            pl.BlockSpec((tm, SEG), row),
            pl.BlockSpec((tm, SEG), row),
            pl.BlockSpec(memory_space=pl.ANY),
            _resident((1, d), const2),
            pl.BlockSpec(memory_space=pl.ANY),
            pl.BlockSpec(memory_space=pl.ANY),
            pl.BlockSpec(memory_space=pl.ANY),
        ],
        out_specs=pl.BlockSpec((tm, d), row),
        out_shape=jax.ShapeDtypeStruct((n, d), F32),
        scratch_shapes=[pltpu.VMEM((tm, d_ff), BF16),
                        pltpu.VMEM((d, d), BF16), pltpu.VMEM((d, d_ff), BF16),
                        pltpu.VMEM((d, d_ff), BF16), pltpu.VMEM((d_ff, d), BF16),
                        pltpu.SemaphoreType.DMA((4,))],
        compiler_params=pltpu.CompilerParams(
            dimension_semantics=("arbitrary",), vmem_limit_bytes=VMEM_LIMIT_BYTES),
        name="out_ffn",
    )(x, od, orr, w_out, norm_g, w_gate, w_up, w_down)


def _tile(n, want):
    return min(n, want)


def kernel(x, attn_norm_g, w_in, q_norm_g, k_norm_g, lambda_q1, lambda_k1, lambda_q2, lambda_k2, subln_g, lower_bounds, hgrn_norm_g, w_out, ffn_norm_g, w_gate, w_up, w_down):
    b, s, d = x.shape
    depth = w_in.shape[0]
    n = b * s
    assert d == 2 * SEG and w_in.shape[-1] == N_SEG * SEG and s % KEY_BLOCK == 0

    w_in_b = w_in[0].astype(BF16)

    tril, masks = _hgrn_tables()
    tables = (jnp.asarray(tril, BF16), jnp.asarray(masks))
    lower_bounds = lower_bounds.astype(F32)

    tm_proj = _tile(s, PROJ_ROWS)
    tm_hgrn = _tile(s, HGRN_ROWS)
    tm_ffn = _tile(s, FFN_ROWS)

    xf = x.reshape(n, d)
    for l in range(depth):
        lam_init = 0.8 - 0.6 * math.exp(-0.3 * l)
        qg = jnp.tile(q_norm_g[l], 2)[:, None] * (DIFF_HEAD_DIM ** -0.5 * LOG2E)
        kg = jnp.tile(k_norm_g[l], 2)[:, None]
        score_bound = (1.02 * math.sqrt(DIFF_HEAD_DIM)
                       * jnp.max(jnp.abs(q_norm_g[l])) * jnp.max(jnp.abs(k_norm_g[l])))
        convert = [(w_out, l), (w_gate, l), (w_up, l), (w_down, l)]
        if l + 1 < depth:
            convert.append((w_in, l + 1))
        (qt, k, vt, hq, hf, hk, hi, hg), converted = _in_proj(
            xf, attn_norm_g[l][None], w_in_b, qg, kg, lower_bounds, l, tm_proj, convert)
        w_out_b, w_gate_b, w_up_b, w_down_b = converted[:4]
        w_in_b = converted[4] if l + 1 < depth else None
        orr = _hgrn(hq, hf, hk, hi, hg, hgrn_norm_g[l][None], tables, tm_hgrn, b)
        od = _diff_attn(qt, k, vt, lambda_q1[l][None], lambda_k1[l][None],
                        lambda_q2[l][None], lambda_k2[l][None], subln_g[l][:, None], lam_init, b,
                        score_bound)
        xf = _out_ffn(xf, od, orr, w_out_b, ffn_norm_g[l][None],
                      w_gate_b, w_up_b, w_down_b, tm_ffn)
    return xf.reshape(b, s, d)
```

```python
import functools
import math

import numpy as np
import jax
import jax.numpy as jnp
from jax import lax
from jax.experimental import pallas as pl
from jax.experimental.pallas import tpu as pltpu

F32 = jnp.float32
BF16 = jnp.bfloat16

NORM_EPS = 1e-6
CHUNK = 64
CHUNK_SHIFT = 6
HEAD_LANES = 128
N_HEADS = 4
SEG = N_HEADS * HEAD_LANES
DIFF_HEAD_DIM = 64
N_SEG = 7
N_LEVELS = CHUNK_SHIFT

VMEM_LIMIT_BYTES = 52 * 1024 * 1024
FFN_TILE = 256
PROJ_ROWS = 1024
HGRN_ROWS = 1024
FFN_ROWS = 1024
KEY_BLOCK = 4096
ATTN_SUB = 256
ONES_ROWS = 16
SCORE_BOUND_MAX = 40.0
_NT = (((1,), (1,)), ((), ()))
SUBLANES = 8
BF16_SUBLANES = 16
LOG2E = math.log2(math.e)
REF_BLOCK_LEVEL = 4
REF_BLOCK = 1 << REF_BLOCK_LEVEL
MASK_REF_BLOCK = N_LEVELS + 1
REF_BLOCK_DECAY_MAX = 60.0


def _rsqrt_mean_sq(y):
    return lax.rsqrt(jnp.mean(y * y, axis=-1, keepdims=True) + NORM_EPS)


def _sigmoid(z):
    return 1.0 / (1.0 + jnp.exp(-z))


def _resident(block_shape, index_map):
    return pl.BlockSpec(block_shape, index_map, pipeline_mode=pl.Buffered(1))


def _in_proj_stages(layer, x_ref, g_ref, w_ref, qg_ref, kg_ref, lb_ref,
                    qt_ref, k_ref, vt_ref, hq_ref, hf_ref, hk_ref, hi_ref, hg_ref):
    x = x_ref[...]
    h = (x * _rsqrt_mean_sq(x) * g_ref[...]).astype(BF16)

    def seg(s):
        return jnp.dot(h, w_ref[:, s * SEG:(s + 1) * SEG], preferred_element_type=F32)

    def head_norm_t(y, gain_ref, c):
        t = y[:, c * HEAD_LANES:(c + 1) * HEAD_LANES].T
        halves = []
        for m in range(HEAD_LANES // DIFF_HEAD_DIM):
            tm_ = t[m * DIFF_HEAD_DIM:(m + 1) * DIFF_HEAD_DIM]
            ms = jnp.mean(tm_ * tm_, axis=0, keepdims=True)
            halves.append(tm_ * lax.rsqrt(ms + NORM_EPS))
        return jnp.concatenate(halves, axis=0) * gain_ref[...]

    def q_stage():
        yq = seg(0)
        for c in range(N_HEADS):
            lanes = slice(c * HEAD_LANES, (c + 1) * HEAD_LANES)
            qt_ref[lanes, :] = head_norm_t(yq, qg_ref, c).astype(BF16)

    def k_stage():
        yk = seg(1)
        for c in range(N_HEADS):
            lanes = slice(c * HEAD_LANES, (c + 1) * HEAD_LANES)
            k_ref[:, lanes] = head_norm_t(yk, kg_ref, c).T.astype(BF16)

    def v_stage():
        vt_ref[...] = seg(2).T.astype(BF16)

    def hq_stage():
        qr = seg(3)
        hq_ref[...] = (qr * _sigmoid(qr)).astype(BF16)

    def hf_stage():
        z = seg(4)
        e = jnp.exp(-jnp.abs(z))
        r = 1.0 / (1.0 + e)
        sig_neg = jnp.where(z >= 0, e * r, r)
        if layer == 0:
            hf_ref[...] = jnp.minimum(z * LOG2E, 0.0) - jnp.log2(1.0 + e)
            hk_ref[...] = sig_neg
        else:
            lbp = lb_ref[...]
            ex = jnp.exp(lbp - jnp.max(lbp, axis=0, keepdims=True))
            gamma = ex / jnp.sum(ex, axis=0, keepdims=True)
            lb = jnp.sum(gamma[1:layer + 1], axis=0, keepdims=True)
            sig_pos = jnp.where(z >= 0, r, e * r)
            hf_ref[...] = jnp.log2(lb + (1.0 - lb) * sig_pos)
            hk_ref[...] = (1.0 - lb) * sig_neg

    def hi_stage():
        hi_ref[...] = seg(5).astype(BF16)

    def hg_stage():
        gr = seg(6)
        hg_ref[...] = (gr * _sigmoid(gr)).astype(BF16)

    return [q_stage, k_stage, v_stage, hq_stage, hf_stage, hi_stage, hg_stage]


N_PROJ_IN = 6
N_PROJ_OUT = 8


def _in_proj_kernel(*refs, n_cast, layer):
    cast_in = refs[N_PROJ_IN:N_PROJ_IN + n_cast]
    outs = refs[N_PROJ_IN + n_cast:N_PROJ_IN + n_cast + N_PROJ_OUT]
    cast_out = refs[N_PROJ_IN + n_cast + N_PROJ_OUT:]
    for stage in _in_proj_stages(layer, *refs[:N_PROJ_IN], *outs):
        stage()
    for src, dst in zip(cast_in, cast_out):
        dst[...] = src[...].astype(BF16)


def _in_proj(x, norm_g, w_in, qg, kg, lower_bounds, layer, tm, convert):
    n, d = x.shape
    steps = n // tm
    row = lambda i: (i, 0)
    const2 = lambda i: (0, 0)
    out_bf = jax.ShapeDtypeStruct((n, SEG), BF16)
    out_f32 = jax.ShapeDtypeStruct((n, SEG), F32)
    seg_spec = pl.BlockSpec((tm, SEG), row)
    per_blk = KEY_BLOCK // tm
    cast_in_specs, cast_out_specs, cast_out_shapes = [], [], []
    for w, w_layer in convert:
        _, rows, cols = w.shape
        slabs = max(k for k in range(1, steps + 1)
                    if rows % k == 0 and (rows // k) % BF16_SUBLANES == 0)
        slab = lambda i, last=slabs - 1: jnp.minimum(i, last)
        cast_in_specs.append(pl.BlockSpec((None, rows // slabs, cols),
                                          lambda i, slab=slab, w_layer=w_layer: (w_layer, slab(i), 0)))
        cast_out_specs.append(pl.BlockSpec((rows // slabs, cols),
                                           lambda i, slab=slab: (slab(i), 0)))
        cast_out_shapes.append(jax.ShapeDtypeStruct((rows, cols), BF16))
    outs = pl.pallas_call(
        functools.partial(_in_proj_kernel, n_cast=len(convert), layer=layer),
        grid=(steps,),
        in_specs=[
            pl.BlockSpec((tm, d), row),
            _resident((1, d), const2),
            _resident((d, N_SEG * SEG), const2),
            _resident((HEAD_LANES, 1), const2),
            _resident((HEAD_LANES, 1), const2),
            _resident(lower_bounds.shape, const2),
        ] + cast_in_specs,
        out_specs=[pl.BlockSpec((SEG, tm), lambda i: (0, i)), seg_spec,
                   pl.BlockSpec((None, SEG, tm), lambda i: (i // per_blk, 0, i % per_blk)),
                   seg_spec, seg_spec, seg_spec, seg_spec, seg_spec] + cast_out_specs,
        out_shape=[jax.ShapeDtypeStruct((SEG, n), BF16), out_bf,
                   jax.ShapeDtypeStruct((n // KEY_BLOCK, SEG, KEY_BLOCK), BF16),
                   out_bf, out_f32, out_f32, out_bf, out_bf] + cast_out_shapes,
        compiler_params=pltpu.CompilerParams(
            dimension_semantics=("arbitrary",), vmem_limit_bytes=VMEM_LIMIT_BYTES),
        name="in_proj",
    )(x, norm_g, w_in, qg, kg, lower_bounds, *[w for w, _ in convert])
    return outs[:N_PROJ_OUT], outs[N_PROJ_OUT:]


def _lambda_full(lq1_ref, lk1_ref, lq2_ref, lk2_ref, lam_init):
    return (jnp.exp(jnp.sum(lq1_ref[...] * lk1_ref[...], axis=-1, keepdims=True))
            - jnp.exp(jnp.sum(lq2_ref[...] * lk2_ref[...], axis=-1, keepdims=True))
            + lam_init)


def _split_maps_t(qt):
    row = lax.broadcasted_iota(jnp.int32, qt.shape, 0)
    zero = jnp.zeros_like(qt)
    return jnp.concatenate([jnp.where(row < DIFF_HEAD_DIM, qt, zero),
                            jnp.where(row >= DIFF_HEAD_DIM, qt, zero)], axis=1)


def _attn_bounded_kernel(qt_ref, k_ref, vt_ref, lq1_ref, lk1_ref, lq2_ref, lk2_ref, sgc_ref,
                         o_ref, acc_ref, *, lam_init):
    blk = pl.program_id(2)
    n_sub = KEY_BLOCK // ATTN_SUB
    ones = jnp.ones((ONES_ROWS, KEY_BLOCK), BF16)

    def q2t(s):
        return _split_maps_t(qt_ref[:, s * ATTN_SUB:(s + 1) * ATTN_SUB])

    def attend(s, k_rows, vt, diagonal):
        n_keys = k_rows.shape[0]
        st = jnp.dot(k_rows, q2t(s), preferred_element_type=F32)
        p = jnp.exp2(st)
        if diagonal:
            tail = p[n_keys - ATTN_SUB:]
            krow = lax.broadcasted_iota(jnp.int32, tail.shape, 0)
            qcol = lax.broadcasted_iota(jnp.int32, tail.shape, 1)
            visible = (krow >> CHUNK_SHIFT) <= ((qcol & (ATTN_SUB - 1)) >> CHUNK_SHIFT)
            tail = jnp.where(visible, tail, 0.0)
            p = tail if n_keys == ATTN_SUB else jnp.concatenate([p[:n_keys - ATTN_SUB], tail], axis=0)
        vte = jnp.concatenate([vt, ones[:, :n_keys]], axis=0)
        return jnp.dot(vte, p.astype(BF16), preferred_element_type=F32)

    start = pl.multiple_of(blk * KEY_BLOCK, KEY_BLOCK)
    for s in range(n_sub):
        n_keys = (s + 1) * ATTN_SUB
        acc_ref[s] = attend(s, k_ref[pl.ds(start, n_keys), :], vt_ref[blk, :, :n_keys], True)

    def body(j, carry):
        k_rows = k_ref[pl.ds(pl.multiple_of(j * KEY_BLOCK, KEY_BLOCK), KEY_BLOCK), :]
        vt = vt_ref[j]
        for s in range(n_sub):
            acc_ref[s] += attend(s, k_rows, vt, False)
        return carry

    lax.fori_loop(0, blk, body, 0)

    lam = _lambda_full(lq1_ref, lk1_ref, lq2_ref, lk2_ref, lam_init)
    for s in range(n_sub):
        a = acc_ref[s]
        o = a[:HEAD_LANES] * (1.0 / a[HEAD_LANES:HEAD_LANES + 1])
        od = o[:, :ATTN_SUB] - lam * o[:, ATTN_SUB:]
        ms = jnp.mean(od * od, axis=0, keepdims=True)
        od = od * lax.rsqrt(ms + NORM_EPS) * sgc_ref[...] * (1.0 - lam_init)
        o_ref[s * ATTN_SUB:(s + 1) * ATTN_SUB, :] = od.T.astype(BF16)


def _attn_online_kernel(qt_ref, k_ref, vt_ref, lq1_ref, lk1_ref, lq2_ref, lk2_ref, sgc_ref,
                        o_ref, v_ref, *, tq, lam_init):
    i = pl.program_id(2)

    @pl.when(i == 0)
    def _():
        for j in range(vt_ref.shape[0]):
            v_ref[j * KEY_BLOCK:(j + 1) * KEY_BLOCK, :] = vt_ref[j].astype(F32).T.astype(BF16)

    q = qt_ref[...].astype(F32).T.astype(BF16)
    lane = lax.broadcasted_iota(jnp.int32, q.shape, 1)
    zero = jnp.zeros_like(q)
    q2 = jnp.concatenate([jnp.where(lane < DIFF_HEAD_DIM, q, zero),
                          jnp.where(lane >= DIFF_HEAD_DIM, q, zero)], axis=0)

    def scores(kj):
        return lax.dot_general(q2, kj, _NT, preferred_element_type=F32)

    start = pl.multiple_of(i * tq, tq)
    s = scores(k_ref[pl.ds(start, tq), :])
    row = lax.broadcasted_iota(jnp.int32, s.shape, 0)
    col = lax.broadcasted_iota(jnp.int32, s.shape, 1)
    visible = (col >> CHUNK_SHIFT) <= ((row & (tq - 1)) >> CHUNK_SHIFT)
    s = jnp.where(visible, s, -jnp.inf)
    m = jnp.max(s, axis=-1, keepdims=True)
    p = jnp.exp2(s - m)
    l = jnp.sum(p, axis=-1, keepdims=True)
    acc = jnp.dot(p.astype(BF16), v_ref[pl.ds(start, tq), :], preferred_element_type=F32)

    def body(j, carry):
        m, l, acc = carry
        off = pl.multiple_of(j * tq, tq)
        s = scores(k_ref[pl.ds(off, tq), :])
        m_new = jnp.maximum(m, jnp.max(s, axis=-1, keepdims=True))
        alpha = jnp.exp2(m - m_new)
        p = jnp.exp2(s - m_new)
        l = alpha * l + jnp.sum(p, axis=-1, keepdims=True)
        acc = alpha * acc + jnp.dot(p.astype(BF16), v_ref[pl.ds(off, tq), :],
                                    preferred_element_type=F32)
        return m_new, l, acc

    m, l, acc = lax.fori_loop(0, i, body, (m, l, acc))
    o = acc / l
    lam = _lambda_full(lq1_ref, lk1_ref, lq2_ref, lk2_ref, lam_init)
    od = o[:tq] - lam * o[tq:]
    od = od * _rsqrt_mean_sq(od) * sgc_ref[...].T * (1.0 - lam_init)
    o_ref[...] = od.astype(BF16)


def _diff_attn(qt, k, vt, lq1, lk1, lq2, lk2, subln_g_col, lam_init, batch, score_bound):
    n = k.shape[0]
    s = n // batch
    blocks = s // KEY_BLOCK
    lam_spec = _resident((1, DIFF_HEAD_DIM), lambda bi, hi, qi: (0, 0))
    k_spec = pl.BlockSpec((s, HEAD_LANES), lambda bi, hi, qi: (bi, hi))
    vt_spec = pl.BlockSpec((blocks, HEAD_LANES, KEY_BLOCK), lambda bi, hi, qi: (bi, hi, 0))
    gain_spec = _resident((HEAD_LANES, 1), lambda bi, hi, qi: (0, 0))

    def call(body, tq, scratch, name):
        per_batch = s // tq
        return pl.pallas_call(
            body,
            grid=(batch, N_HEADS, per_batch),
            in_specs=[pl.BlockSpec((HEAD_LANES, tq), lambda bi, hi, qi: (hi, bi * per_batch + qi)),
                      k_spec, vt_spec, lam_spec, lam_spec, lam_spec, lam_spec, gain_spec],
            out_specs=pl.BlockSpec((tq, HEAD_LANES), lambda bi, hi, qi: (bi * per_batch + qi, hi)),
            out_shape=jax.ShapeDtypeStruct((n, SEG), BF16),
            scratch_shapes=scratch,
            compiler_params=pltpu.CompilerParams(
                dimension_semantics=("arbitrary", "arbitrary", "arbitrary"),
                vmem_limit_bytes=VMEM_LIMIT_BYTES),
            name=name,
        )

    bounded = call(
        functools.partial(_attn_bounded_kernel, lam_init=lam_init), KEY_BLOCK,
        [pltpu.VMEM((KEY_BLOCK // ATTN_SUB, HEAD_LANES + ONES_ROWS, 2 * ATTN_SUB), F32)],
        "diff_attn_bounded")
    online = call(
        functools.partial(_attn_online_kernel, tq=ATTN_SUB, lam_init=lam_init), ATTN_SUB,
        [pltpu.VMEM((s, HEAD_LANES), BF16)], "diff_attn_online")
    return lax.cond(score_bound <= SCORE_BOUND_MAX, bounded, online,
                    qt, k, vt, lq1, lk1, lq2, lk2, subln_g_col)


def _hgrn_tables():
    c = CHUNK
    masks = np.zeros((N_LEVELS + 1, c, c), np.float32)
    masks[0] = np.eye(c, dtype=np.float32)
    for lev in range(1, N_LEVELS + 1):
        size = 1 << lev
        half = size // 2
        for r in range(c):
            if r % size >= half:
                mid = r - r % size + half
                masks[lev, r, mid - half:mid] = 1.0
    t_idx, s_idx = np.arange(c)[:, None], np.arange(c)[None, :]
    in_block = ((t_idx // REF_BLOCK == s_idx // REF_BLOCK) & (s_idx <= t_idx)).astype(np.float32)
    masks = np.concatenate([masks, in_block[None]], axis=0)
    padded = np.zeros((masks.shape[0], 2, c, 2 * c), np.float32)
    padded[:, 0, :, :c] = masks
    padded[:, 1, :, c:] = masks
    return np.tril(np.ones((c, c), np.float32)), padded


def _level_ref_row(lev, row):
    size = 1 << lev
    return row - row % size + size // 2 - 1


def _hgrn_kernel(hq_ref, lf_ref, k_ref, hi_ref, hg_ref, g_ref, tril_ref, masks_ref,
                 o_ref, state_ref, b_ref, *, n_chunks):
    @pl.when(pl.program_id(1) == 0)
    def _():
        state_ref[...] = jnp.zeros_like(state_ref)

    tn = (((0,), (0,)), ((), ()))
    groups = CHUNK // SUBLANES
    sub = lax.broadcasted_iota(jnp.int32, (SUBLANES, SEG), 0)
    up_small = {lev: (sub & (1 << (lev - 1))) != 0 for lev in (1, 2, 3)}
    sgn_small = {lev: jnp.where(up_small[lev], 1.0, -1.0) for lev in (2, 3)}
    upper_groups = {lev: [g for g in range(groups)
                          if (g * SUBLANES) % (1 << lev) >= (1 << lev) // 2]
                    for lev in range(4, N_LEVELS + 1)}

    def grp(a, g):
        return a[g * SUBLANES:(g + 1) * SUBLANES, :]

    def stack(parts):
        return jnp.concatenate(parts, axis=0).astype(BF16)

    def heads_to_rows(a):
        return jnp.concatenate(
            [a[:, h * HEAD_LANES:(h + 1) * HEAD_LANES] for h in range(N_HEADS)], axis=0)

    worst = None
    for c in range(n_chunks):
        rows = slice(c * CHUNK, (c + 1) * CHUNK)
        lf2 = lf_ref[rows, :]
        hi = lf2.astype(BF16)
        lo = (lf2 - hi.astype(F32)).astype(BF16)
        b2 = (jnp.dot(tril_ref[...], hi, preferred_element_type=F32)
              + jnp.dot(tril_ref[...], lo, preferred_element_type=F32))
        b_ref[c] = b2
        for blk in range(CHUNK // REF_BLOCK):
            last = (blk + 1) * REF_BLOCK - 1
            span = -b2[last:last + 1, :]
            if blk:
                span = span + b2[blk * REF_BLOCK - 1:blk * REF_BLOCK, :]
            worst = span if worst is None else jnp.maximum(worst, span)
    moderate = jnp.max(worst) <= REF_BLOCK_DECAY_MAX

    def chunk(c, single_reference):
        rows = slice(c * CHUNK, (c + 1) * CHUNK)
        b2 = b_ref[c]
        kk = k_ref[rows, :]
        qq = hq_ref[rows, :].astype(F32)

        def ref_rows(row):
            return jnp.broadcast_to(b_ref[c, row:row + 1, :], (SUBLANES, SEG))

        x = {lev: [] for lev in range(1, N_LEVELS + 1)}
        xq_blk, xk_blk, q_pre, k_suf = [], [], [], []
        for g in range(groups):
            b_g, q_g, k_g = grp(b2, g), grp(qq, g), grp(kk, g)
            row0 = g * SUBLANES
            if single_reference:
                start = row0 - row0 % REF_BLOCK
                d = b_g - ref_rows(start - 1) if start else b_g
                xq_blk.append(q_g * jnp.exp2(d))
                xk_blk.append(k_g * jnp.exp2(-d))
                first_level = REF_BLOCK_LEVEL + 1
            else:
                x[1].append(jnp.where(up_small[1], q_g * jnp.exp2(grp(lf_ref[rows, :], g)), k_g))
                ref2 = jnp.where(sub < 4, ref_rows(_level_ref_row(2, row0)),
                                 ref_rows(_level_ref_row(2, row0 + 4)))
                x[2].append(jnp.where(up_small[2], q_g, k_g)
                            * jnp.exp2((b_g - ref2) * sgn_small[2]))
                ref3 = ref_rows(_level_ref_row(3, row0))
                x[3].append(jnp.where(up_small[3], q_g, k_g)
                            * jnp.exp2((b_g - ref3) * sgn_small[3]))
                first_level = 4
            for lev in range(first_level, N_LEVELS + 1):
                ref = ref_rows(_level_ref_row(lev, row0))
                if row0 % (1 << lev) >= (1 << lev) // 2:
                    x[lev].append(q_g * jnp.exp2(b_g - ref))
                else:
                    x[lev].append(k_g * jnp.exp2(ref - b_g))
            q_pre.append(q_g * jnp.exp2(b_g))
            k_suf.append(k_g * jnp.exp2(ref_rows(CHUNK - 1) - b_g))

        all_groups = list(range(groups))
        if single_reference:
            terms = [(MASK_REF_BLOCK, all_groups, xq_blk, xk_blk)]
        else:
            terms = [(lev, all_groups, x[lev], x[lev]) for lev in (1, 2, 3)]
        for lev in range(first_level, N_LEVELS + 1):
            terms.append((lev, upper_groups[lev], [x[lev][g] for g in upper_groups[lev]], x[lev]))

        scores = [lax.dot_general(heads_to_rows(stack(q_rows)), heads_to_rows(stack(k_rows)), _NT,
                                  preferred_element_type=F32)
                  for _, _, q_rows, k_rows in terms]

        pair_blocks = []
        for pair in range(N_HEADS // 2):
            lanes = slice(pair * HEAD_LANES, (pair + 1) * HEAD_LANES)
            a_rows = []
            for h in (2 * pair, 2 * pair + 1):
                for g in range(groups):
                    gs = slice(g * SUBLANES, (g + 1) * SUBLANES)
                    a_g = None
                    if not single_reference:
                        hs = slice(h * HEAD_LANES, (h + 1) * HEAD_LANES)
                        a_g = (jnp.sum(qq[gs, hs] * kk[gs, hs], axis=-1, keepdims=True)
                               * masks_ref[0, h % 2, gs, :])
                    for (mask_id, q_groups, _, _), sc in zip(terms, scores):
                        if g not in q_groups:
                            continue
                        r0 = (h * len(q_groups) + q_groups.index(g)) * SUBLANES
                        part = sc[r0:r0 + SUBLANES, lanes] * masks_ref[mask_id, h % 2, gs, :]
                        a_g = part if a_g is None else a_g + part
                    a_rows.append(a_g)
            pair_blocks.append(jnp.concatenate(a_rows, axis=0))
        zero = jnp.zeros_like(pair_blocks[0])
        a_full = jnp.concatenate(
            [jnp.concatenate([pair_blocks[0], zero], axis=1),
             jnp.concatenate([zero, pair_blocks[1]], axis=1)], axis=0).astype(BF16)
        o_intra = jnp.dot(a_full, heads_to_rows(hi_ref[rows, :]), preferred_element_type=F32)

        q_pre = stack(q_pre)
        k_suf = stack(k_suf)
        decay_last = jnp.exp2(b2[CHUNK - 1:CHUNK, :])
        for h in range(N_HEADS):
            hs = slice(h * HEAD_LANES, (h + 1) * HEAD_LANES)
            st = state_ref[h]
            o = (o_intra[h * CHUNK:(h + 1) * CHUNK, :]
                 + lax.dot_general(q_pre[:, hs], st.astype(BF16), _NT,
                                   preferred_element_type=F32))
            state_ref[h] = (st * decay_last[:, hs]
                            + lax.dot_general(hi_ref[rows, hs], k_suf[:, hs], tn,
                                              preferred_element_type=F32))
            o = o * _rsqrt_mean_sq(o) * g_ref[...] * hg_ref[rows, hs].astype(F32)
            o_ref[rows, hs] = o.astype(BF16)

    @pl.when(moderate)
    def _():
        for c in range(n_chunks):
            chunk(c, True)

    @pl.when(jnp.logical_not(moderate))
    def _():
        for c in range(n_chunks):
            chunk(c, False)


def _hgrn(hq, hf, hk, hi, hg, norm_g, tables, tc, batch):
    n = hq.shape[0]
    tril, masks = tables
    per_batch = n // batch // tc
    tile = pl.BlockSpec((tc, SEG), lambda bi, ci: (bi * per_batch + ci, 0))
    const2 = lambda bi, ci: (0, 0)
    return pl.pallas_call(
        functools.partial(_hgrn_kernel, n_chunks=tc // CHUNK),
        grid=(batch, per_batch),
        in_specs=[tile, tile, tile, tile, tile,
                  _resident((1, HEAD_LANES), const2),
                  _resident(tril.shape, const2),
                  _resident(masks.shape, lambda bi, ci: (0, 0, 0, 0))],
        out_specs=tile,
        out_shape=jax.ShapeDtypeStruct((n, SEG), BF16),
        scratch_shapes=[pltpu.VMEM((N_HEADS, HEAD_LANES, HEAD_LANES), F32),
                        pltpu.VMEM((tc // CHUNK, CHUNK, SEG), F32)],
        compiler_params=pltpu.CompilerParams(
            dimension_semantics=("arbitrary", "arbitrary"),
            vmem_limit_bytes=VMEM_LIMIT_BYTES),
        name="hgrn2",
    )(hq, hf, hk, hi, hg, norm_g, tril, masks)


def _out_ffn_kernel(x_ref, od_ref, or_ref, wo_hbm, g_ref, wg_hbm, wu_hbm, wd_hbm,
                    o_ref, ff_ref, wo_ref, wg_ref, wu_ref, wd_ref, sem, *, d_ff):
    first = pl.program_id(0) == 0
    copies = [pltpu.make_async_copy(src, dst, sem.at[k])
              for k, (src, dst) in enumerate(((wo_hbm, wo_ref), (wg_hbm, wg_ref),
                                              (wu_hbm, wu_ref), (wd_hbm, wd_ref)))]

    def compute(fetch_weights):
        if fetch_weights:
            for cp in copies:
                cp.start()
            copies[0].wait()
        xn = (x_ref[...]
              + jnp.dot(od_ref[...], wo_ref[:SEG, :], preferred_element_type=F32)
              + jnp.dot(or_ref[...], wo_ref[SEG:, :], preferred_element_type=F32))
        h = (xn * _rsqrt_mean_sq(xn) * g_ref[...]).astype(BF16)
        if fetch_weights:
            copies[1].wait()
            copies[2].wait()
        for j in range(d_ff // FFN_TILE):
            cols = slice(j * FFN_TILE, (j + 1) * FFN_TILE)
            gate = jnp.dot(h, wg_ref[:, cols], preferred_element_type=F32)
            up = jnp.dot(h, wu_ref[:, cols], preferred_element_type=F32)
            ff_ref[:, cols] = (gate * _sigmoid(gate) * up).astype(BF16)
        if fetch_weights:
            copies[3].wait()
        o_ref[...] = xn + jnp.dot(ff_ref[...], wd_ref[...], preferred_element_type=F32)

    pl.when(first)(lambda: compute(True))
    pl.when(jnp.logical_not(first))(lambda: compute(False))


def _out_ffn(x, od, orr, w_out, norm_g, w_gate, w_up, w_down, tm):
    n, d = x.shape
    d_ff = w_gate.shape[-1]
    row = lambda i: (i, 0)
    const2 = lambda i: (0, 0)
    return pl.pallas_call(
        functools.partial(_out_ffn_kernel, d_ff=d_ff),
        grid=(n // tm,),
        in_specs=[
            pl.BlockSpec((tm, d), row),
            pl.BlockSpec((tm, SEG), row),
            pl.BlockSpec((tm, SEG), row),
            pl.BlockSpec(memory_space=pltpu.HBM),
            _resident((1, d), const2),
            pl.BlockSpec(memory_space=pltpu.HBM),
            pl.BlockSpec(memory_space=pltpu.HBM),
            pl.BlockSpec(memory_space=pltpu.HBM),
        ],
        out_specs=pl.BlockSpec((tm, d), row),
        out_shape=jax.ShapeDtypeStruct((n, d), F32),
        scratch_shapes=[pltpu.VMEM((tm, d_ff), BF16),
                        pltpu.VMEM((d, d), BF16), pltpu.VMEM((d, d_ff), BF16),
                        pltpu.VMEM((d, d_ff), BF16), pltpu.VMEM((d_ff, d), BF16),
                        pltpu.SemaphoreType.DMA((4,))],
        compiler_params=pltpu.CompilerParams(
            dimension_semantics=("arbitrary",), vmem_limit_bytes=VMEM_LIMIT_BYTES),
        name="out_ffn",
    )(x, od, orr, w_out, norm_g, w_gate, w_up, w_down)


def _tile(n, want):
    return min(n, want)


def kernel(x, attn_norm_g, w_in, q_norm_g, k_norm_g, lambda_q1, lambda_k1, lambda_q2, lambda_k2, subln_g, lower_bounds, hgrn_norm_g, w_out, ffn_norm_g, w_gate, w_up, w_down):
    b, s, d = x.shape
    depth = w_in.shape[0]
    n = b * s
    assert d == 2 * SEG and w_in.shape[-1] == N_SEG * SEG and s % KEY_BLOCK == 0

    w_in_b = w_in[0].astype(BF16)

    tril, masks = _hgrn_tables()
    tables = (jnp.asarray(tril, BF16), jnp.asarray(masks))
    lower_bounds = lower_bounds.astype(F32)

    tm_proj = _tile(s, PROJ_ROWS)
    tm_hgrn = _tile(s, HGRN_ROWS)
    tm_ffn = _tile(s, FFN_ROWS)

    xf = x.reshape(n, d)
    for l in range(depth):
        lam_init = 0.8 - 0.6 * math.exp(-0.3 * l)
        qg = jnp.tile(q_norm_g[l], 2)[:, None] * (DIFF_HEAD_DIM ** -0.5 * LOG2E)
        kg = jnp.tile(k_norm_g[l], 2)[:, None]
        score_bound = (1.02 * math.sqrt(DIFF_HEAD_DIM)
                       * jnp.max(jnp.abs(q_norm_g[l])) * jnp.max(jnp.abs(k_norm_g[l])))
        convert = [(w_out, l), (w_gate, l), (w_up, l), (w_down, l)]
        if l + 1 < depth:
            convert.append((w_in, l + 1))
        (qt, k, vt, hq, hf, hk, hi, hg), converted = _in_proj(
            xf, attn_norm_g[l][None], w_in_b, qg, kg, lower_bounds, l, tm_proj, convert)
        w_out_b, w_gate_b, w_up_b, w_down_b = converted[:4]
        w_in_b = converted[4] if l + 1 < depth else None
        orr = _hgrn(hq, hf, hk, hi, hg, hgrn_norm_g[l][None], tables, tm_hgrn, b)
        od = _diff_attn(qt, k, vt, lambda_q1[l][None], lambda_k1[l][None],
                        lambda_q2[l][None], lambda_k2[l][None], subln_g[l][:, None], lam_init, b,
                        score_bound)
        xf = _out_ffn(xf, od, orr, w_out_b, ffn_norm_g[l][None],
                      w_gate_b, w_up_b, w_down_b, tm_ffn)
    return xf.reshape(b, s, d)
```
